```python
import jax, jax.numpy as jnp
from jax import lax
import numpy as np

D_MODEL = 4096
BATCH = 4
SEQ = 4096
DEPTH = 1

CHUNK = 64
N_META = 16
POOL_WINDOWS = (2, 4, 8, 16)
N_POOL_GROUPS = 4
POOL_GROUP = 512
POOL_WIDTH = N_POOL_GROUPS * POOL_GROUP
N_HEADS = 32
Q_RANK = 1024
KV_RANK = 512
V_HEAD_DIM = 128
IDX_HEADS = 32
IDX_DIM = 128
TOPK_MAX = 256
Q_BLOCK = 128
ATTN_SCALE = KV_RANK ** -0.5
IDX_SCALE = (IDX_DIM ** -0.5) * (IDX_HEADS ** -0.5)
N_EXPERTS = 32
TOP_K_EXPERTS = 4
D_EXPERT = 1024
SWIGLU_LIMIT = 7.0
SWIGLU_ALPHA = 1.702
EPS = 1e-6
NEG = -1e30
IN_SPLITS = (Q_RANK, KV_RANK, IDX_DIM, IDX_HEADS, POOL_WIDTH, 2 * D_MODEL)
IN_WIDTH = sum(IN_SPLITS)

kernel_name = "hybrid_pool_dsa_moe_block"


def rmsnorm(z, g):
    zf = z.astype(jnp.float32)
    zf = zf * lax.rsqrt(jnp.mean(zf * zf, axis=-1, keepdims=True) + EPS)
    return (zf * g.astype(jnp.float32)).astype(z.dtype)


def split_cols(z, widths):
    offs, acc = [], 0
    for w in widths[:-1]:
        acc += w
        offs.append(acc)
    return jnp.split(z, offs, axis=-1)


def chunk_ids(n):
    p = jnp.arange(n)
    return jnp.where(p < N_META, 0, 1 + (p - N_META) // CHUNK)


def pool_mixer(a, w_pool, pool_scale):
    B, T, _ = a.shape
    a_g = a.reshape(B, T, N_POOL_GROUPS, POOL_GROUP).astype(jnp.float32)
    cs = jnp.cumsum(a_g, axis=1)
    t = jnp.arange(T)
    means = []
    for g, w in enumerate(POOL_WINDOWS):
        c = cs[:, :, g]
        lag = jnp.pad(c, ((0, 0), (w, 0), (0, 0)))[:, :T]
        cnt = jnp.minimum(t + 1, w).astype(jnp.float32)[None, :, None]
        means.append((c - lag) / cnt)
    pooled = (jnp.stack(means, axis=2) - a_g).astype(a.dtype)
    y = jnp.einsum('btgc,gcd->btgd', pooled, w_pool)
    y = y * pool_scale.reshape(N_POOL_GROUPS, POOL_GROUP)
    return y.reshape(B, T, POOL_WIDTH)


def sparse_attention(c_q, c_kv, k_idx, w_idx, k_sel, q_norm_g, w_uq, w_iq, kv_norm_g,
                     q_head_norm_g, k_head_norm_g, idx_k_norm_g, w_uv):
    B, T, _ = c_q.shape
    cq = rmsnorm(c_q, q_norm_g)
    q = rmsnorm((cq @ w_uq).reshape(B, T, N_HEADS, KV_RANK), q_head_norm_g)
    qi = (cq @ w_iq).reshape(B, T, IDX_HEADS, IDX_DIM)
    latent = rmsnorm(c_kv, kv_norm_g)
    ki = rmsnorm(k_idx, idx_k_norm_g).astype(jnp.float32)
    cid_k = chunk_ids(T)
    n_blk = -(-T // Q_BLOCK)
    T_pad = n_blk * Q_BLOCK
    cid_q = chunk_ids(T_pad).reshape(n_blk, Q_BLOCK)

    def to_blocks(z):
        zp = jnp.pad(z, [(0, 0), (0, T_pad - T)] + [(0, 0)] * (z.ndim - 2))
        return jnp.moveaxis(zp.reshape(B, n_blk, Q_BLOCK, *z.shape[2:]), 1, 0)

    def block(args):
        qb, qib, wb, cqb = args
        s = jnp.einsum('bqhd,bsd->bqhs', qib.astype(jnp.float32), ki)
        score = jnp.einsum('bqhs,bqh->bqs', jax.nn.relu(s), wb.astype(jnp.float32)) * IDX_SCALE
        admissible = cid_k[None, :] <= cqb[:, None]
        score = jnp.where(admissible[None], score, NEG)
        _, idx = lax.top_k(score, k_sel)
        valid = cid_k[idx] <= cqb[None, :, None]
        kv = jax.vmap(lambda lat, ix: lat[ix])(latent, idx)
        kk = rmsnorm(kv, k_head_norm_g)
        logits = jnp.einsum('bqhc,bqkc->bqhk', qb.astype(jnp.float32), kk.astype(jnp.float32)) * ATTN_SCALE
        logits = jnp.where(valid[:, :, None, :], logits, NEG)
        p = jax.nn.softmax(logits, axis=-1)
        o = jnp.einsum('bqhk,bqkc->bqhc', p.astype(kv.dtype), kv)
        return jnp.einsum('bqhc,hcv->bqhv', o, w_uv)

    out = lax.map(block, (to_blocks(q), to_blocks(qi), to_blocks(w_idx), cid_q))
    out = jnp.moveaxis(out, 0, 1).reshape(B, T_pad, N_HEADS * V_HEAD_DIM)
    return out[:, :T]


def moe(xn, router_w, router_b, w_gate_up, b_gate_up, w_down, b_down):
    B, T, D = xn.shape
    xt = xn.reshape(B * T, D)
    logits = (xt @ router_w + router_b).astype(jnp.float32)
    top_val, top_idx = lax.top_k(logits, TOP_K_EXPERTS)
    top_w = jax.nn.softmax(top_val, axis=-1)
    combine = jnp.sum(jax.nn.one_hot(top_idx, N_EXPERTS, dtype=jnp.float32) * top_w[..., None], axis=1)
    combine = combine.astype(xt.dtype)
    y = jnp.zeros_like(xt)
    for e in range(N_EXPERTS):
        gu = xt @ w_gate_up[e] + b_gate_up[e]
        gate = jnp.minimum(gu[:, ::2], SWIGLU_LIMIT)
        up = jnp.clip(gu[:, 1::2], -SWIGLU_LIMIT, SWIGLU_LIMIT)
        act = (up + 1.0) * gate * jax.nn.sigmoid(SWIGLU_ALPHA * gate)
        y = y + combine[:, e:e + 1] * (act @ w_down[e] + b_down[e])
    return y.reshape(B, T, D)


def setup_inputs(seed: int = 0) -> dict:
    key = jax.random.key(seed)
    ks = jax.random.split(key, 24)
    f32 = jnp.float32
    L = DEPTH

    def nrm(k, shape, fan_in):
        return jax.random.normal(k, shape, f32) * fan_in ** -0.5

    def gain(k, shape):
        return 1.0 + 0.05 * jax.random.normal(k, shape, f32)

    def small(k, shape):
        return 0.01 * jax.random.normal(k, shape, f32)

    return {
        "x": jax.random.normal(ks[0], (BATCH, SEQ, D_MODEL), f32),
        "meta_tokens": jax.random.normal(ks[1], (N_META, D_MODEL), f32),
        "norm1_g": gain(ks[2], (L, D_MODEL)),
        "w_in": nrm(ks[3], (L, D_MODEL, IN_WIDTH), D_MODEL),
        "q_norm_g": gain(ks[4], (L, Q_RANK)),
        "w_uq": nrm(ks[5], (L, Q_RANK, N_HEADS * KV_RANK), Q_RANK),
        "w_iq": nrm(ks[6], (L, Q_RANK, IDX_HEADS * IDX_DIM), Q_RANK),
        "kv_norm_g": gain(ks[7], (L, KV_RANK)),
        "q_head_norm_g": gain(ks[8], (L, KV_RANK)),
        "k_head_norm_g": gain(ks[9], (L, KV_RANK)),
        "idx_k_norm_g": gain(ks[10], (L, IDX_DIM)),
        "w_uv": nrm(ks[11], (L, N_HEADS, KV_RANK, V_HEAD_DIM), KV_RANK),
        "w_branch_attn": nrm(ks[12], (L, N_HEADS * V_HEAD_DIM, D_MODEL), N_HEADS * V_HEAD_DIM),
        "w_pool": nrm(ks[13], (L, N_POOL_GROUPS, POOL_GROUP, POOL_GROUP), POOL_GROUP),
        "pool_scale": gain(ks[14], (L, POOL_WIDTH)),
        "w_branch_pool": nrm(ks[15], (L, POOL_WIDTH, D_MODEL), POOL_WIDTH),
        "w_out": nrm(ks[16], (L, D_MODEL, D_MODEL), D_MODEL),
        "norm2_g": gain(ks[17], (L, D_MODEL)),
        "router_w": nrm(ks[18], (L, D_MODEL, N_EXPERTS), D_MODEL),
        "router_b": small(ks[19], (L, N_EXPERTS)),
        "w_gate_up": nrm(ks[20], (L, N_EXPERTS, D_MODEL, 2 * D_EXPERT), D_MODEL),
        "b_gate_up": small(ks[21], (L, N_EXPERTS, 2 * D_EXPERT)),
        "w_down": nrm(ks[22], (L, N_EXPERTS, D_EXPERT, D_MODEL), D_EXPERT),
        "b_down": small(ks[23], (L, N_EXPERTS, D_MODEL)),
    }


def reference(x, meta_tokens, norm1_g, w_in, q_norm_g, w_uq, w_iq, kv_norm_g, q_head_norm_g,
              k_head_norm_g, idx_k_norm_g, w_uv, w_branch_attn, w_pool, pool_scale, w_branch_pool,
              w_out, norm2_g, router_w, router_b, w_gate_up, b_gate_up, w_down, b_down):
    B, S, D = x.shape
    k_sel = min(TOPK_MAX, S // 4)
    meta = jnp.broadcast_to(meta_tokens[None].astype(x.dtype), (B, N_META, D))
    h = jnp.concatenate([meta, x], axis=1)
    for l in range(DEPTH):
        xn = rmsnorm(h, norm1_g[l])
        c_q, c_kv, k_idx, w_idx, a_pool, gate_logits = split_cols(xn @ w_in[l], IN_SPLITS)
        y_pool = pool_mixer(a_pool, w_pool[l], pool_scale[l])
        y_attn = sparse_attention(c_q, c_kv, k_idx, w_idx, k_sel, q_norm_g[l], w_uq[l], w_iq[l],
                                  kv_norm_g[l], q_head_norm_g[l], k_head_norm_g[l],
                                  idx_k_norm_g[l], w_uv[l])
        g_pool, g_attn = jnp.split(jax.nn.sigmoid(gate_logits), 2, axis=-1)
        merged = g_pool * (y_pool @ w_branch_pool[l]) + g_attn * (y_attn @ w_branch_attn[l])
        h = h + merged @ w_out[l]
        h = h + moe(rmsnorm(h, norm2_g[l]), router_w[l], router_b[l], w_gate_up[l],
                    b_gate_up[l], w_down[l], b_down[l])
    return h[:, N_META:]
```

```python
import functools

import jax
import jax.numpy as jnp
from jax import lax
from jax.experimental import pallas as pl
from jax.experimental.pallas import tpu as pltpu

F32 = jnp.float32
BF16 = jnp.bfloat16
I32 = jnp.int32

CHUNK = 64
POOL_WINDOWS = (2, 4, 8, 16)
TOPK_MAX = 256
TOP_K_EXPERTS = 4
SWIGLU_LIMIT = 7.0
SWIGLU_ALPHA = 1.702
EPS = 1e-6
NEG = -1e30

Q_BLOCK = 128
KEY_TILE = 512
HEADS_PER_STEP = 8
HALO = 16
EXPERT_TILE = 256
ROUTE_TILE = 256
LANES = 128
INT_MIN = -2 ** 31
VMEM_LIMIT = 56 * 1024 * 1024


def _tile(n, target, mult):
    best = None
    for t in range(mult, min(n, target) + 1, mult):
        if n % t == 0:
            best = t
    assert best is not None, (n, target, mult)
    return best


def _params(*sem):
    return pltpu.CompilerParams(dimension_semantics=sem, vmem_limit_bytes=VMEM_LIMIT)


def _rmsnorm_kernel(x_ref, g_ref, o_ref):
    x = x_ref[...]
    ms = jnp.mean(x * x, axis=-1, keepdims=True)
    o_ref[...] = (x * lax.rsqrt(ms + EPS) * g_ref[...]).astype(o_ref.dtype)


def _rmsnorm_rows(x, g, out_dtype):
    n, d = x.shape
    tm = _tile(n, 256, 8)
    return pl.pallas_call(
        _rmsnorm_kernel,
        grid=(n // tm,),
        in_specs=[pl.BlockSpec((tm, d), lambda i: (i, 0)), pl.BlockSpec((1, d), lambda i: (0, 0))],
        out_specs=pl.BlockSpec((tm, d), lambda i: (i, 0)),
        out_shape=jax.ShapeDtypeStruct((n, d), out_dtype),
        compiler_params=_params("parallel"),
    )(x, g.reshape(1, d))


def _matmul_kernel(x_ref, w_ref, o_ref, *, sigmoid):
    acc = jnp.dot(x_ref[...], w_ref[...], preferred_element_type=F32)
    if sigmoid:
        acc = jax.nn.sigmoid(acc)
    o_ref[...] = acc.astype(o_ref.dtype)


def _matmul(x, w, out_dtype, *, tn, sigmoid=False):
    m, k = x.shape
    n = w.shape[1]
    tm = _tile(m, 768, 128)
    return pl.pallas_call(
        functools.partial(_matmul_kernel, sigmoid=sigmoid),
        grid=(m // tm, n // tn),
        in_specs=[pl.BlockSpec((tm, k), lambda i, j: (i, 0)), pl.BlockSpec((k, tn), lambda i, j: (0, j))],
        out_specs=pl.BlockSpec((tm, tn), lambda i, j: (i, j)),
        out_shape=jax.ShapeDtypeStruct((m, n), out_dtype),
        compiler_params=_params("parallel", "arbitrary"),
    )(x, w)


def _prep_q_kernel(cq_ref, wi_ref, g_ref, cq_o, wi_o, *, idx_scale):
    x = cq_ref[...]
    ms = jnp.mean(x * x, axis=-1, keepdims=True)
    cq_o[...] = (x * lax.rsqrt(ms + EPS) * g_ref[...]).astype(cq_o.dtype)
    wi_o[...] = wi_ref[...] * idx_scale


def _prep_q(small, q_norm_g, qr, wi_col, idx_scale):
    n = small.shape[0]
    tm = _tile(n, 768, 128)
    return pl.pallas_call(
        functools.partial(_prep_q_kernel, idx_scale=idx_scale),
        grid=(n // tm,),
        in_specs=[pl.BlockSpec((tm, qr), lambda i: (i, 0)),
                  pl.BlockSpec((tm, LANES), lambda i: (i, wi_col // LANES)),
                  pl.BlockSpec((1, qr), lambda i: (0, 0))],
        out_specs=[pl.BlockSpec((tm, qr), lambda i: (i, 0)), pl.BlockSpec((tm, LANES), lambda i: (i, 0))],
        out_shape=[jax.ShapeDtypeStruct((n, qr), BF16), jax.ShapeDtypeStruct((n, LANES), F32)],
        compiler_params=_params("parallel"),
    )(small, small, q_norm_g.reshape(1, qr))


def _prep_kv_kernel(ckv_ref, kidx_ref, gkv_ref, gkh_ref, gki_ref, lat_o, kk_o, ki_o, *, n_valid):
    i = pl.program_id(1)

    @pl.when(i < n_valid)
    def _():
        c = ckv_ref[...]
        lat = c * lax.rsqrt(jnp.mean(c * c, axis=-1, keepdims=True) + EPS) * gkv_ref[...]
        kk = lat * lax.rsqrt(jnp.mean(lat * lat, axis=-1, keepdims=True) + EPS) * gkh_ref[...]
        k = kidx_ref[...]
        ki = k * lax.rsqrt(jnp.mean(k * k, axis=-1, keepdims=True) + EPS) * gki_ref[...]
        lat_o[0] = lat.astype(lat_o.dtype)
        kk_o[0] = kk.astype(kk_o.dtype)
        ki_o[0] = ki.astype(ki_o.dtype)

    @pl.when(i >= n_valid)
    def _():
        lat_o[...] = jnp.zeros_like(lat_o)
        kk_o[...] = jnp.zeros_like(kk_o)
        ki_o[...] = jnp.zeros_like(ki_o)


def _prep_kv(small, gkv, gkh, gki, *, batch, tp, tk, qr, kvr, di):
    nq = tp // Q_BLOCK
    nk = tk // Q_BLOCK

    def row(b, i):
        return b * nq + jnp.minimum(i, nq - 1)

    return pl.pallas_call(
        functools.partial(_prep_kv_kernel, n_valid=nq),
        grid=(batch, nk),
        in_specs=[pl.BlockSpec((Q_BLOCK, kvr), lambda b, i: (row(b, i), qr // kvr)),
                  pl.BlockSpec((Q_BLOCK, di), lambda b, i: (row(b, i), (qr + kvr) // di)),
                  pl.BlockSpec((1, kvr), lambda b, i: (0, 0)),
                  pl.BlockSpec((1, kvr), lambda b, i: (0, 0)),
                  pl.BlockSpec((1, di), lambda b, i: (0, 0))],
        out_specs=[pl.BlockSpec((1, Q_BLOCK, kvr), lambda b, i: (b, i, 0)),
                   pl.BlockSpec((1, Q_BLOCK, kvr), lambda b, i: (b, i, 0)),
                   pl.BlockSpec((1, Q_BLOCK, di), lambda b, i: (b, i, 0))],
        out_shape=[jax.ShapeDtypeStruct((batch, tk, kvr), BF16),
                   jax.ShapeDtypeStruct((batch, tk, kvr), BF16),
                   jax.ShapeDtypeStruct((batch, tk, di), BF16)],
        compiler_params=_params("parallel", "arbitrary"),
    )(small, small, gkv.reshape(1, kvr), gkh.reshape(1, kvr), gki.reshape(1, di))


def _pool_kernel(a_ref, halo_ref, w_ref, sc_ref, o_ref, xs_ref, *, tp_tile, pad_front, pg):
    i = pl.program_id(1)
    t = i * tp_tile + lax.broadcasted_iota(I32, (tp_tile, 1), 0) - pad_front
    for g, win in enumerate(POOL_WINDOWS):
        cols = slice(g * pg, (g + 1) * pg)
        xs_ref[0:HALO, :] = halo_ref[0, :, cols].astype(F32)
        xs_ref[HALO:, :] = a_ref[0, :, cols].astype(F32)
        cur = xs_ref[pl.ds(HALO, tp_tile), :]
        acc = cur
        for k in range(1, win):
            acc = acc + xs_ref[pl.ds(HALO - k, tp_tile), :]
        cnt = jnp.clip(t + 1, 1, win).astype(F32)
        pooled = (acc / cnt - cur).astype(BF16)
        y = jnp.dot(pooled, w_ref[g], preferred_element_type=F32) * sc_ref[:, cols]
        o_ref[0, :, cols] = y.astype(o_ref.dtype)


def _pool_mixer(a, w_pool, pool_scale, *, batch, tp, pad_front):
    pw = a.shape[-1]
    n_groups, pg, _ = w_pool.shape
    tpt = _tile(tp, 1536, HALO)
    a3 = a.reshape(batch, tp, pw)
    halo_blocks = tpt // HALO
    out = pl.pallas_call(
        functools.partial(_pool_kernel, tp_tile=tpt, pad_front=pad_front, pg=pg),
        grid=(batch, tp // tpt),
        in_specs=[pl.BlockSpec((1, tpt, pw), lambda b, i: (b, i, 0)),
                  pl.BlockSpec((1, HALO, pw), lambda b, i: (b, jnp.maximum(i * halo_blocks - 1, 0), 0)),
                  pl.BlockSpec((n_groups, pg, pg), lambda b, i: (0, 0, 0)),
                  pl.BlockSpec((1, pw), lambda b, i: (0, 0))],
        out_specs=pl.BlockSpec((1, tpt, pw), lambda b, i: (b, i, 0)),
        out_shape=jax.ShapeDtypeStruct((batch, tp, pw), BF16),
        scratch_shapes=[pltpu.VMEM((tpt + HALO, pg), F32)],
        compiler_params=_params("parallel", "arbitrary"),
    )(a3, a3, w_pool, pool_scale.reshape(1, pw))
    return out.reshape(batch * tp, pw)


def _qproj_kernel(x_ref, w_ref, g_ref, o_ref, *, scale, norm, heads, hd):
    acc = jnp.dot(x_ref[...], w_ref[...], preferred_element_type=F32)
    nblk = o_ref.shape[0]
    for hh in range(heads):
        a = acc[:, hh * hd:(hh + 1) * hd]
        if norm:
            a = a * lax.rsqrt(jnp.mean(a * a, axis=-1, keepdims=True) + EPS) * (g_ref[...] * scale)
        a = a.astype(o_ref.dtype)
        for r in range(nblk):
            o_ref[r, hh] = a[r * Q_BLOCK:(r + 1) * Q_BLOCK]


def _qproj(cq, w, g, *, n_heads, hd, heads_per_tile, scale, norm):
    n, r = cq.shape
    tm = _tile(n, 768, Q_BLOCK)
    nblk = tm // Q_BLOCK
    tn = heads_per_tile * hd
    return pl.pallas_call(
        functools.partial(_qproj_kernel, scale=scale, norm=norm, heads=heads_per_tile, hd=hd),
        grid=(n // tm, n_heads // heads_per_tile),
        in_specs=[pl.BlockSpec((tm, r), lambda i, j: (i, 0)),
                  pl.BlockSpec((r, tn), lambda i, j: (0, j)),
                  pl.BlockSpec((1, hd), lambda i, j: (0, 0))],
        out_specs=pl.BlockSpec((nblk, heads_per_tile, Q_BLOCK, hd), lambda i, j: (i, j, 0, 0)),
        out_shape=jax.ShapeDtypeStruct((n // Q_BLOCK, n_heads, Q_BLOCK, hd), BF16),
        compiler_params=_params("parallel", "arbitrary"),
    )(cq, w, g.reshape(1, hd))


def _attn_kernel(q_ref, qi_ref, wi_ref, ki_ref, kk_ref, lat_ref, wuv_ref, o_ref,
                 bias_ref, keyp_ref, acc_ref, m_ref, l_ref, *, k_sel, pad_front, idx_heads):
    j = pl.program_id(1)
    hg = pl.program_id(2)
    hps, qb, c = q_ref.shape[1], q_ref.shape[2], q_ref.shape[3]
    di = qi_ref.shape[3]
    v = wuv_ref.shape[2]
    rows = hps * qb
    n_t = (j * qb + qb + KEY_TILE - 1) // KEY_TILE
    nt_dims = (((1,), (1,)), ((), ()))

    @pl.when(hg == 0)
    def _select():
        tq = j * qb + lax.broadcasted_iota(I32, (qb, 1), 0)
        limit = ((jnp.maximum(tq, CHUNK) + CHUNK) // CHUNK) * CHUNK

        def score_tile(kt, carry):
            off = pl.multiple_of(kt * KEY_TILE, KEY_TILE)
            ki_t = ki_ref[0, pl.ds(off, KEY_TILE), :]
            part = jnp.zeros((qb, KEY_TILE), F32)
            hpi = min(HEADS_PER_STEP, idx_heads)
            for g in range(idx_heads // hpi):
                qg = qi_ref[0, g * hpi:(g + 1) * hpi].reshape(hpi * qb, di)
                s = lax.dot_general(qg, ki_t, nt_dims, preferred_element_type=F32)
                s = jnp.maximum(s, 0.0)
                for hh in range(hpi):
                    h = g * hpi + hh
                    part = part + s[hh * qb:(hh + 1) * qb] * wi_ref[:, h:h + 1]
            bits = pltpu.bitcast(part, I32)
            key = bits ^ ((bits >> 31) & 0x7FFFFFFF)
            s_idx = off + lax.broadcasted_iota(I32, (qb, KEY_TILE), 1)
            adm = (s_idx >= pad_front) & (s_idx < limit)
            keyp_ref[:, pl.ds(off, KEY_TILE)] = jnp.where(adm, key, INT_MIN)
            return carry

        lax.fori_loop(0, n_t, score_tile, 0)

        def count_ge(thr):
            def body(kt, cnt):
                off = pl.multiple_of(kt * KEY_TILE, KEY_TILE)
                ge = jnp.where(keyp_ref[:, pl.ds(off, KEY_TILE)] >= thr, 1.0, 0.0)
                for u in range(KEY_TILE // LANES):
                    cnt = cnt + ge[:, u * LANES:(u + 1) * LANES]
                return cnt
            cnt = lax.fori_loop(0, n_t, body, jnp.zeros((qb, LANES), F32))
            return jnp.sum(cnt, axis=1, keepdims=True)

        def bit_step(bi, thr):
            cand = thr + lax.shift_left(jnp.int32(1), 31 - bi)
            return jnp.where(count_ge(cand) >= k_sel, cand, thr)

        thr = lax.fori_loop(0, 32, bit_step, jnp.full((qb, 1), INT_MIN, I32))
        thr = jnp.maximum(thr, INT_MIN + 1)

        def bias_tile(kt, carry):
            off = pl.multiple_of(kt * KEY_TILE, KEY_TILE)
            sel = keyp_ref[:, pl.ds(off, KEY_TILE)] >= thr
            bias_ref[:, pl.ds(off, KEY_TILE)] = jnp.where(sel, 0.0, NEG)
            return carry

        lax.fori_loop(0, n_t, bias_tile, 0)

        @pl.when(jnp.max(count_ge(thr)) > k_sel)
        def _ties():
            quota = k_sel - count_ge(thr + 1)
            upper = (lax.broadcasted_iota(I32, (LANES, LANES), 0)
                     < lax.broadcasted_iota(I32, (LANES, LANES), 1)).astype(BF16)

            def chunk(ci, seen):
                off = pl.multiple_of(ci * LANES, LANES)
                kp = keyp_ref[:, pl.ds(off, LANES)]
                tie = jnp.where(kp == thr, 1.0, 0.0)
                rank = jnp.dot(tie.astype(BF16), upper, preferred_element_type=F32) + seen
                sel = jnp.where(kp > thr, 1.0, tie * jnp.where(rank < quota, 1.0, 0.0))
                bias_ref[:, pl.ds(off, LANES)] = jnp.where(sel > 0.5, 0.0, NEG)
                return seen + jnp.sum(tie, axis=1, keepdims=True)

            lax.fori_loop(0, n_t * (KEY_TILE // LANES), chunk, jnp.zeros((qb, 1), F32))

    q_g = q_ref[0].reshape(rows, c)
    m_ref[...] = jnp.full(m_ref.shape, -3e38, F32)
    l_ref[...] = jnp.zeros(l_ref.shape, F32)
    acc_ref[...] = jnp.zeros(acc_ref.shape, F32)

    def kv_step(kt, carry):
        off = pl.multiple_of(kt * KEY_TILE, KEY_TILE)
        kk_t = kk_ref[0, pl.ds(off, KEY_TILE), :]
        lat_t = lat_ref[0, pl.ds(off, KEY_TILE), :]
        s = lax.dot_general(q_g, kk_t, nt_dims, preferred_element_type=F32)
        s = (s.reshape(hps, qb, KEY_TILE) + bias_ref[:, pl.ds(off, KEY_TILE)][None]).reshape(rows, KEY_TILE)
        m_prev = m_ref[...]
        m_new = jnp.maximum(m_prev, jnp.max(s, axis=1, keepdims=True))
        alpha = jnp.exp(m_prev - m_new)
        p = jnp.exp(s - m_new)
        l_ref[...] = alpha * l_ref[...] + jnp.sum(p, axis=1, keepdims=True)
        acc_ref[...] = alpha * acc_ref[...] + jnp.dot(p.astype(BF16), lat_t, preferred_element_type=F32)
        m_ref[...] = m_new
        return carry

    lax.fori_loop(0, n_t, kv_step, 0)

    o = (acc_ref[...] / l_ref[...]).astype(BF16)
    for hh in range(hps):
        y = jnp.dot(o[hh * qb:(hh + 1) * qb], wuv_ref[hh], preferred_element_type=F32)
        o_ref[:, hh * v:(hh + 1) * v] = y.astype(o_ref.dtype)


def _sparse_attention(q, qi, widx, ki, kk, lat, w_uv, *, batch, tp, k_sel, pad_front):
    nblk, n_heads, qb, c = q.shape
    idx_heads, di = qi.shape[1], qi.shape[3]
    tk = ki.shape[1]
    v = w_uv.shape[2]
    nq = tp // qb
    hps = min(HEADS_PER_STEP, n_heads)
    rows = hps * qb
    return pl.pallas_call(
        functools.partial(_attn_kernel, k_sel=k_sel, pad_front=pad_front, idx_heads=idx_heads),
        grid=(batch, nq, n_heads // hps),
        in_specs=[pl.BlockSpec((1, hps, qb, c), lambda b, j, h: (b * nq + j, h, 0, 0)),
                  pl.BlockSpec((1, idx_heads, qb, di), lambda b, j, h: (b * nq + j, 0, 0, 0)),
                  pl.BlockSpec((qb, LANES), lambda b, j, h: (b * nq + j, 0)),
                  pl.BlockSpec((1, tk, di), lambda b, j, h: (b, 0, 0)),
                  pl.BlockSpec((1, tk, c), lambda b, j, h: (b, 0, 0)),
                  pl.BlockSpec((1, tk, c), lambda b, j, h: (b, 0, 0)),
                  pl.BlockSpec((hps, c, v), lambda b, j, h: (h, 0, 0))],
        out_specs=pl.BlockSpec((qb, hps * v), lambda b, j, h: (b * nq + j, h)),
        out_shape=jax.ShapeDtypeStruct((nblk * qb, n_heads * v), BF16),
        scratch_shapes=[pltpu.VMEM((qb, tk), F32), pltpu.VMEM((qb, tk), I32),
                        pltpu.VMEM((rows, c), F32), pltpu.VMEM((rows, 1), F32), pltpu.VMEM((rows, 1), F32)],
        compiler_params=_params("parallel", "arbitrary", "arbitrary"),
    )(q, qi, widx, ki, kk, lat, w_uv)


def _merge_kernel(yp_ref, ya_ref, wp_ref, wa_ref, gp_ref, ga_ref, o_ref):
    pool = jnp.dot(yp_ref[...], wp_ref[...], preferred_element_type=F32)
    attn = jnp.dot(ya_ref[...], wa_ref[...], preferred_element_type=F32)
    o_ref[...] = (gp_ref[...].astype(F32) * pool + ga_ref[...].astype(F32) * attn).astype(o_ref.dtype)


def _merge(y_pool, y_attn, w_bp, w_ba, gates):
    n, pw = y_pool.shape
    aw = y_attn.shape[1]
    d = w_bp.shape[1]
    tm = _tile(n, 768, 128)
    tn = 512
    return pl.pallas_call(
        _merge_kernel,
        grid=(n // tm, d // tn),
        in_specs=[pl.BlockSpec((tm, pw), lambda i, j: (i, 0)),
                  pl.BlockSpec((tm, aw), lambda i, j: (i, 0)),
                  pl.BlockSpec((pw, tn), lambda i, j: (0, j)),
                  pl.BlockSpec((aw, tn), lambda i, j: (0, j)),
                  pl.BlockSpec((tm, tn), lambda i, j: (i, j)),
                  pl.BlockSpec((tm, tn), lambda i, j: (i, j + d // tn))],
        out_specs=pl.BlockSpec((tm, tn), lambda i, j: (i, j)),
        out_shape=jax.ShapeDtypeStruct((n, d), BF16),
        compiler_params=_params("parallel", "arbitrary"),
    )(y_pool, y_attn, w_bp, w_ba, gates, gates)


def _out_proj_kernel(x_ref, w_ref, r_ref, o_ref):
    o_ref[...] = r_ref[...] + jnp.dot(x_ref[...], w_ref[...], preferred_element_type=F32)


def _out_proj(merged, w_out, resid):
    n, k = merged.shape
    d = w_out.shape[1]
    tm = _tile(n, 768, 128)
    tn = 512
    return pl.pallas_call(
        _out_proj_kernel,
        grid=(n // tm, d // tn),
        in_specs=[pl.BlockSpec((tm, k), lambda i, j: (i, 0)),
                  pl.BlockSpec((k, tn), lambda i, j: (0, j)),
                  pl.BlockSpec((tm, tn), lambda i, j: (i, j))],
        out_specs=pl.BlockSpec((tm, tn), lambda i, j: (i, j)),
        out_shape=jax.ShapeDtypeStruct((n, d), F32),
        compiler_params=_params("parallel", "arbitrary"),
    )(merged, w_out, resid)


def _pack_halves(x):
    w = x.shape[1] // 2
    lo = pltpu.bitcast(x[:, :w].astype(BF16).astype(F32), I32)
    hi = pltpu.bitcast(x[:, w:].astype(BF16).astype(F32), I32)
    return ((lo >> 16) & 0xFFFF) | (hi & -65536)


def _unpack_halves(p):
    return pltpu.bitcast(p << 16, F32), pltpu.bitcast(p & -65536, F32)


def _router_kernel(h_ref, g_ref, rw_ref, rb_ref, xp_o, e_o, w_o, rank_o, cnt_o, carry_ref):
    i = pl.program_id(0)

    @pl.when(i == 0)
    def _():
        carry_ref[...] = jnp.zeros_like(carry_ref)

    x = h_ref[...]
    xn = x * lax.rsqrt(jnp.mean(x * x, axis=-1, keepdims=True) + EPS) * g_ref[...]
    xp_o[...] = _pack_halves(xn)
    logits = jnp.dot(xn.astype(BF16), rw_ref[...], preferred_element_type=F32) + rb_ref[...]
    tm = logits.shape[0]
    lane = lax.broadcasted_iota(I32, (tm, LANES), 1).astype(F32)
    vals, hots = [], []
    cur = logits
    for k in range(TOP_K_EXPERTS):
        m = jnp.max(cur, axis=1, keepdims=True)
        idx = jnp.min(jnp.where(cur == m, lane, float(LANES)), axis=1, keepdims=True)
        hot = lane == idx
        vals.append(m)
        hots.append(hot)
        e_o[:, k:k + 1] = idx.astype(I32)
        cur = jnp.where(hot, -jnp.inf, cur)
    exps = [jnp.exp(vk - vals[0]) for vk in vals]
    denom = exps[0]
    for ek in exps[1:]:
        denom = denom + ek
    for k in range(TOP_K_EXPERTS):
        w_o[:, k:k + 1] = exps[k] / denom
    onehot = jnp.zeros((tm, LANES), F32)
    for hot in hots:
        onehot = onehot + jnp.where(hot, 1.0, 0.0)
    lower = (lax.broadcasted_iota(I32, (tm, tm), 0) > lax.broadcasted_iota(I32, (tm, tm), 1)).astype(BF16)
    before = jnp.dot(lower, onehot.astype(BF16), preferred_element_type=F32) + carry_ref[...]
    for k in range(TOP_K_EXPERTS):
        rank_o[:, k:k + 1] = jnp.sum(jnp.where(hots[k], before, 0.0), axis=1, keepdims=True).astype(I32)
    carry_ref[...] = carry_ref[...] + jnp.sum(onehot, axis=0, keepdims=True)
    cnt_o[...] = carry_ref[...]


def _router(h2, norm2_g, rw, rb):
    n, d = h2.shape
    tm = ROUTE_TILE
    kk = TOP_K_EXPERTS
    return pl.pallas_call(
        _router_kernel,
        grid=(n // tm,),
        in_specs=[pl.BlockSpec((tm, d), lambda i: (i, 0)),
                  pl.BlockSpec((1, d), lambda i: (0, 0)),
                  pl.BlockSpec((d, LANES), lambda i: (0, 0)),
                  pl.BlockSpec((1, LANES), lambda i: (0, 0))],
        out_specs=[pl.BlockSpec((tm, d // 2), lambda i: (i, 0)),
                   pl.BlockSpec((tm, kk), lambda i: (i, 0)),
                   pl.BlockSpec((tm, kk), lambda i: (i, 0)),
                   pl.BlockSpec((tm, kk), lambda i: (i, 0)),
                   pl.BlockSpec((1, LANES), lambda i: (0, 0))],
        out_shape=[jax.ShapeDtypeStruct((n, d // 2), I32),
                   jax.ShapeDtypeStruct((n, kk), I32),
                   jax.ShapeDtypeStruct((n, kk), F32),
                   jax.ShapeDtypeStruct((n, kk), I32),
                   jax.ShapeDtypeStruct((1, LANES), F32)],
        scratch_shapes=[pltpu.VMEM((1, LANES), F32)],
        compiler_params=_params("arbitrary"),
    )(h2, norm2_g.reshape(1, d), rw, rb)


def _row_copy(src_ref, src_row, dst_ref, dst_row, sem):
    return pltpu.make_async_copy(src_ref.at[pl.ds(src_row, 1), :], dst_ref.at[pl.ds(dst_row, 1), :], sem)


def _dispatch_kernel(dest_ref, xp_ref, init_ref, xs_ref, sem):
    del init_ref
    tm = xp_ref.shape[0]
    base = pl.program_id(0) * tm * TOP_K_EXPERTS

    def issue(r, carry):
        for k in range(TOP_K_EXPERTS):
            _row_copy(xp_ref, r, xs_ref, dest_ref[base + r * TOP_K_EXPERTS + k], sem).start()
        return carry

    def drain(r, carry):
        for k in range(TOP_K_EXPERTS):
            _row_copy(xp_ref, 0, xs_ref, 0, sem).wait()
        return carry

    lax.fori_loop(0, tm, issue, 0)
    lax.fori_loop(0, tm, drain, 0)


def _dispatch(dest_flat, xp, n_slots):
    n, w = xp.shape
    tm = ROUTE_TILE
    return pl.pallas_call(
        _dispatch_kernel,
        grid_spec=pltpu.PrefetchScalarGridSpec(
            num_scalar_prefetch=1,
            grid=(n // tm,),
            in_specs=[pl.BlockSpec((tm, w), lambda i, dest: (i, 0)),
                      pl.BlockSpec(memory_space=pl.ANY)],
            out_specs=pl.BlockSpec(memory_space=pl.ANY),
            scratch_shapes=[pltpu.SemaphoreType.DMA(())]),
        out_shape=jax.ShapeDtypeStruct((n_slots, w), I32),
        input_output_aliases={2: 0},
        compiler_params=_params("arbitrary"),
    )(dest_flat, xp, jnp.zeros((n_slots, w), I32))


def _expert_up_kernel(te_ref, tv_ref, xs_ref, w_ref, b_ref, o_ref):
    t = pl.program_id(0)

    @pl.when(tv_ref[t] > 0)
    def _():
        lo, hi = _unpack_halves(xs_ref[...])
        x = jnp.concatenate([lo.astype(BF16), hi.astype(BF16)], axis=1)
        gu = jnp.dot(x, w_ref[0], preferred_element_type=F32) + b_ref[0]
        de = gu.shape[1] // 2
        gate = jnp.minimum(gu[:, :de], SWIGLU_LIMIT)
        up = jnp.clip(gu[:, de:], -SWIGLU_LIMIT, SWIGLU_LIMIT)
        o_ref[...] = ((up + 1.0) * gate * jax.nn.sigmoid(SWIGLU_ALPHA * gate)).astype(o_ref.dtype)

    @pl.when(tv_ref[t] == 0)
    def _():
        o_ref[...] = jnp.zeros_like(o_ref)


def _expert_up(tile_expert, tile_valid, xs, w_gu, b_gu):
    n_slots, w = xs.shape
    n_exp, d, de2 = w_gu.shape
    tm = EXPERT_TILE
    return pl.pallas_call(
        _expert_up_kernel,
        grid_spec=pltpu.PrefetchScalarGridSpec(
            num_scalar_prefetch=2,
            grid=(n_slots // tm,),
            in_specs=[pl.BlockSpec((tm, w), lambda t, te, tv: (t, 0)),
                      pl.BlockSpec((1, d, de2), lambda t, te, tv: (te[t], 0, 0)),
                      pl.BlockSpec((1, 1, de2), lambda t, te, tv: (te[t], 0, 0))],
            out_specs=pl.BlockSpec((tm, de2 // 2), lambda t, te, tv: (t, 0))),
        out_shape=jax.ShapeDtypeStruct((n_slots, de2 // 2), BF16),
        compiler_params=_params("arbitrary"),
    )(tile_expert, tile_valid, xs, w_gu, b_gu)


def _expert_down_kernel(te_ref, tv_ref, a_ref, w_ref, b_ref, o_ref):
    t = pl.program_id(0)

    @pl.when(tv_ref[t] > 0)
    def _():
        y = jnp.dot(a_ref[...], w_ref[0], preferred_element_type=F32) + b_ref[0]
        o_ref[...] = _pack_halves(y)

    @pl.when(tv_ref[t] == 0)
    def _():
        o_ref[...] = jnp.zeros_like(o_ref)


def _expert_down(tile_expert, tile_valid, act, w_down, b_down):
    n_slots, de = act.shape
    n_exp, _, d = w_down.shape
    tm = EXPERT_TILE
    return pl.pallas_call(
        _expert_down_kernel,
        grid_spec=pltpu.PrefetchScalarGridSpec(
            num_scalar_prefetch=2,
            grid=(n_slots // tm,),
            in_specs=[pl.BlockSpec((tm, de), lambda t, te, tv: (t, 0)),
                      pl.BlockSpec((1, de, d), lambda t, te, tv: (te[t], 0, 0)),
                      pl.BlockSpec((1, 1, d), lambda t, te, tv: (te[t], 0, 0))],
            out_specs=pl.BlockSpec((tm, d // 2), lambda t, te, tv: (t, 0))),
        out_shape=jax.ShapeDtypeStruct((n_slots, d // 2), I32),
        compiler_params=_params("arbitrary"),
    )(tile_expert, tile_valid, act, w_down, b_down)


def _combine_kernel(dest_ref, w_ref, h_ref, ye_ref, o_ref, buf_ref, sem, *, nq):
    b = pl.program_id(0)
    i = pl.program_id(1)
    tm = h_ref.shape[0]
    half = buf_ref.shape[2]
    base = ((b * nq + 1 + i) * tm) * TOP_K_EXPERTS

    def issue(r, carry):
        for k in range(TOP_K_EXPERTS):
            _row_copy(ye_ref, dest_ref[base + r * TOP_K_EXPERTS + k], buf_ref.at[k], r, sem).start()
        return carry

    def drain(r, carry):
        for k in range(TOP_K_EXPERTS):
            _row_copy(ye_ref, 0, buf_ref.at[k], 0, sem).wait()
        return carry

    lax.fori_loop(0, tm, issue, 0)
    lax.fori_loop(0, tm, drain, 0)

    lo_sum = h_ref[:, :half]
    hi_sum = h_ref[:, half:]
    for k in range(TOP_K_EXPERTS):
        lo, hi = _unpack_halves(buf_ref[k])
        wk = w_ref[:, k:k + 1]
        lo_sum = lo_sum + wk * lo
        hi_sum = hi_sum + wk * hi
    o_ref[0, :, :half] = lo_sum
    o_ref[0, :, half:] = hi_sum


def _combine(dest_flat, sel_w, h2, ye, *, batch, seq, tp):
    d = h2.shape[1]
    tm = Q_BLOCK
    nq = tp // tm
    return pl.pallas_call(
        functools.partial(_combine_kernel, nq=nq),
        grid_spec=pltpu.PrefetchScalarGridSpec(
            num_scalar_prefetch=1,
            grid=(batch, seq // tm),
            in_specs=[pl.BlockSpec((tm, TOP_K_EXPERTS), lambda b, i, dest: (b * nq + 1 + i, 0)),
                      pl.BlockSpec((tm, d), lambda b, i, dest: (b * nq + 1 + i, 0)),
                      pl.BlockSpec(memory_space=pl.ANY)],
            out_specs=pl.BlockSpec((1, tm, d), lambda b, i, dest: (b, i, 0)),
            scratch_shapes=[pltpu.VMEM((TOP_K_EXPERTS, tm, d // 2), I32), pltpu.SemaphoreType.DMA(())]),
        out_shape=jax.ShapeDtypeStruct((batch, seq, d), F32),
        compiler_params=_params("arbitrary", "arbitrary"),
    )(dest_flat, sel_w, h2, ye)


def _moe(h2, norm2_g, router_w, router_b, w_gate_up, b_gate_up, w_down, b_down, *, batch, seq, tp):
    n, d = h2.shape
    n_exp = router_w.shape[1]
    de = w_down.shape[1]
    assert n_exp <= LANES and n % ROUTE_TILE == 0
    rw = jnp.pad(router_w, ((0, 0), (0, LANES - n_exp))).astype(BF16)
    rb = jnp.pad(router_b, (0, LANES - n_exp), constant_values=NEG).reshape(1, LANES)
    xp, sel_e, sel_w, rank, counts = _router(h2, norm2_g, rw, rb)

    cnt = counts[0, :n_exp].astype(I32)
    padded = ((cnt + EXPERT_TILE - 1) // EXPERT_TILE) * EXPERT_TILE
    ends = jnp.cumsum(padded)
    starts = ends - padded
    dest_flat = (starts[sel_e] + rank).reshape(-1)
    n_tiles = (n * TOP_K_EXPERTS) // EXPERT_TILE + n_exp
    tile_start = jnp.arange(n_tiles, dtype=I32) * EXPERT_TILE
    tile_valid = (tile_start < ends[-1]).astype(I32)
    last_expert = jnp.searchsorted(ends, ends[-1] - 1, side="right").astype(I32)
    tile_expert = jnp.minimum(jnp.searchsorted(ends, tile_start, side="right").astype(I32), last_expert)

    w_gu = jnp.concatenate([w_gate_up[..., 0::2], w_gate_up[..., 1::2]], axis=-1).astype(BF16)
    b_gu = jnp.concatenate([b_gate_up[..., 0::2], b_gate_up[..., 1::2]], axis=-1).reshape(n_exp, 1, 2 * de)

    xs = _dispatch(dest_flat, xp, n_tiles * EXPERT_TILE)
    act = _expert_up(tile_expert, tile_valid, xs, w_gu, b_gu)
    ye = _expert_down(tile_expert, tile_valid, act, w_down.astype(BF16), b_down.reshape(n_exp, 1, d))
    return _combine(dest_flat, sel_w, h2, ye, batch=batch, seq=seq, tp=tp)


def kernel(x, meta_tokens, norm1_g, w_in, q_norm_g, w_uq, w_iq, kv_norm_g, q_head_norm_g, k_head_norm_g,
           idx_k_norm_g, w_uv, w_branch_attn, w_pool, pool_scale, w_branch_pool, w_out, norm2_g, router_w,
           router_b, w_gate_up, b_gate_up, w_down, b_down):
    batch, seq, d = x.shape
    n_meta = meta_tokens.shape[0]
    depth = norm1_g.shape[0]
    qr, kvr, di = q_norm_g.shape[1], kv_norm_g.shape[1], idx_k_norm_g.shape[1]
    n_heads, v = w_uv.shape[1], w_uv.shape[3]
    idx_heads = w_iq.shape[2] // di
    pw = pool_scale.shape[1]
    assert depth == 1
    assert n_meta <= CHUNK and seq % Q_BLOCK == 0 and idx_heads <= LANES
    assert qr % kvr == 0 and (qr + kvr) % di == 0 and di % LANES == 0
    pad_front = Q_BLOCK - n_meta
    tp = Q_BLOCK + seq
    tk = -(-tp // KEY_TILE) * KEY_TILE
    k_sel = min(TOPK_MAX, seq // 4)
    attn_scale = kvr ** -0.5
    idx_scale = (di ** -0.5) * (idx_heads ** -0.5)

    meta = jnp.broadcast_to(meta_tokens[None].astype(x.dtype), (batch, n_meta, d))
    h = jnp.concatenate([jnp.zeros((batch, pad_front, d), x.dtype), meta, x], axis=1).reshape(batch * tp, d)

    out = None
    for l in range(depth):
        n_small = qr + kvr + di
        o_pool = n_small + idx_heads
        o_gate = o_pool + pw
        w_small = jnp.pad(w_in[l][:, :o_pool], ((0, 0), (0, LANES - idx_heads))).astype(BF16)
        w_a = w_in[l][:, o_pool:o_gate].astype(BF16)
        w_g = w_in[l][:, o_gate:].astype(BF16)

        xn = _rmsnorm_rows(h, norm1_g[l], BF16)
        small = _matmul(xn, w_small, F32, tn=_tile(w_small.shape[1], 512, LANES))
        a_pool = _matmul(xn, w_a, BF16, tn=512)
        gates = _matmul(xn, w_g, BF16, tn=512, sigmoid=True)

        y_pool = _pool_mixer(a_pool, w_pool[l].astype(BF16), pool_scale[l], batch=batch, tp=tp,
                             pad_front=pad_front)

        cq, widx = _prep_q(small, q_norm_g[l], qr, n_small, idx_scale)
        lat, kk, ki = _prep_kv(small, kv_norm_g[l], k_head_norm_g[l], idx_k_norm_g[l],
                               batch=batch, tp=tp, tk=tk, qr=qr, kvr=kvr, di=di)
        q = _qproj(cq, w_uq[l].astype(BF16), q_head_norm_g[l], n_heads=n_heads, hd=kvr, heads_per_tile=1,
                   scale=attn_scale, norm=True)
        qi = _qproj(cq, w_iq[l].astype(BF16), jnp.ones((di,), F32), n_heads=idx_heads, hd=di,
                    heads_per_tile=min(4, idx_heads), scale=1.0, norm=False)
        y_attn = _sparse_attention(q, qi, widx, ki, kk, lat, w_uv[l].astype(BF16), batch=batch, tp=tp,
                                   k_sel=k_sel, pad_front=pad_front)

        merged = _merge(y_pool, y_attn, w_branch_pool[l].astype(BF16), w_branch_attn[l].astype(BF16), gates)
        h2 = _out_proj(merged, w_out[l].astype(BF16), h)

        out = _moe(h2, norm2_g[l], router_w[l], router_b[l], w_gate_up[l], b_gate_up[l], w_down[l],
                   b_down[l], batch=batch, seq=seq, tp=tp)
    return out
```

```python
import functools

import jax
import jax.numpy as jnp
from jax import lax
from jax.experimental import pallas as pl
from jax.experimental.pallas import tpu as pltpu

F32 = jnp.float32
BF16 = jnp.bfloat16
I32 = jnp.int32

CHUNK = 64
POOL_WINDOWS = (2, 4, 8, 16)
TOPK_MAX = 256
TOP_K_EXPERTS = 4
SWIGLU_LIMIT = 7.0
SWIGLU_ALPHA = 1.702
EPS = 1e-6
NEG = -1e30

Q_BLOCK = 128
KEY_TILE = 512
HEADS_PER_STEP = 16
INDEXER_HEADS_PER_DOT = 8
HEADS_PER_CHAIN = 4
HALO = 16
EXPERT_TILE = 256
ROUTE_TILE = 256
LANES = 128
INT_MIN = -2 ** 31
VMEM_LIMIT = 56 * 1024 * 1024


def _tile(n, target, mult):
    best = None
    for t in range(mult, min(n, target) + 1, mult):
        if n % t == 0:
            best = t
    assert best is not None, (n, target, mult)
    return best


def _params(*sem):
    return pltpu.CompilerParams(dimension_semantics=sem, vmem_limit_bytes=VMEM_LIMIT)


def _rmsnorm_kernel(x_ref, g_ref, o_ref):
    x = x_ref[...]
    ms = jnp.mean(x * x, axis=-1, keepdims=True)
    o_ref[...] = (x * lax.rsqrt(ms + EPS) * g_ref[...]).astype(o_ref.dtype)


def _rmsnorm_rows(x, g, out_dtype):
    n, d = x.shape
    tm = _tile(n, 256, 8)
    return pl.pallas_call(
        _rmsnorm_kernel,
        grid=(n // tm,),
        in_specs=[pl.BlockSpec((tm, d), lambda i: (i, 0)), pl.BlockSpec((1, d), lambda i: (0, 0))],
        out_specs=pl.BlockSpec((tm, d), lambda i: (i, 0)),
        out_shape=jax.ShapeDtypeStruct((n, d), out_dtype),
        name="rmsnorm_rows",
        compiler_params=_params("parallel"),
    )(x, g.reshape(1, d))


def _matmul_kernel(x_ref, w_ref, o_ref, *, sigmoid):
    acc = jnp.dot(x_ref[...], w_ref[...], preferred_element_type=F32)
    if sigmoid:
        acc = jax.nn.sigmoid(acc)
    o_ref[...] = acc.astype(o_ref.dtype)


def _matmul(x, w, out_dtype, *, tn, name, sigmoid=False):
    m, k = x.shape
    n = w.shape[1]
    tm = _tile(m, 768, 128)
    return pl.pallas_call(
        functools.partial(_matmul_kernel, sigmoid=sigmoid),
        grid=(m // tm, n // tn),
        in_specs=[pl.BlockSpec((tm, k), lambda i, j: (i, 0)), pl.BlockSpec((k, tn), lambda i, j: (0, j))],
        out_specs=pl.BlockSpec((tm, tn), lambda i, j: (i, j)),
        out_shape=jax.ShapeDtypeStruct((m, n), out_dtype),
        name=name,
        compiler_params=_params("parallel", "arbitrary"),
    )(x, w)


def _prep_q_kernel(cq_ref, wi_ref, g_ref, cq_o, wi_o, *, idx_scale):
    x = cq_ref[...]
    ms = jnp.mean(x * x, axis=-1, keepdims=True)
    cq_o[...] = (x * lax.rsqrt(ms + EPS) * g_ref[...]).astype(cq_o.dtype)
    wi_o[...] = wi_ref[...] * idx_scale


def _prep_q(small, q_norm_g, qr, wi_col, idx_scale):
    n = small.shape[0]
    tm = _tile(n, 768, 128)
    return pl.pallas_call(
        functools.partial(_prep_q_kernel, idx_scale=idx_scale),
        grid=(n // tm,),
        in_specs=[pl.BlockSpec((tm, qr), lambda i: (i, 0)),
                  pl.BlockSpec((tm, LANES), lambda i: (i, wi_col // LANES)),
                  pl.BlockSpec((1, qr), lambda i: (0, 0))],
        out_specs=[pl.BlockSpec((tm, qr), lambda i: (i, 0)), pl.BlockSpec((tm, LANES), lambda i: (i, 0))],
        out_shape=[jax.ShapeDtypeStruct((n, qr), BF16), jax.ShapeDtypeStruct((n, LANES), F32)],
        name="prep_query_latent",
        compiler_params=_params("parallel"),
    )(small, small, q_norm_g.reshape(1, qr))


def _prep_kv_kernel(ckv_ref, kidx_ref, gkv_ref, gkh_ref, gki_ref, lat_o, kk_o, ki_o, *, n_valid):
    i = pl.program_id(1)

    @pl.when(i < n_valid)
    def _():
        c = ckv_ref[...]
        lat = c * lax.rsqrt(jnp.mean(c * c, axis=-1, keepdims=True) + EPS) * gkv_ref[...]
        kk = lat * lax.rsqrt(jnp.mean(lat * lat, axis=-1, keepdims=True) + EPS) * gkh_ref[...]
        k = kidx_ref[...]
        ki = k * lax.rsqrt(jnp.mean(k * k, axis=-1, keepdims=True) + EPS) * gki_ref[...]
        lat_o[0] = lat.astype(lat_o.dtype)
        kk_o[0] = kk.astype(kk_o.dtype)
        ki_o[0] = ki.astype(ki_o.dtype)

    @pl.when(i >= n_valid)
    def _():
        lat_o[...] = jnp.zeros_like(lat_o)
        kk_o[...] = jnp.zeros_like(kk_o)
        ki_o[...] = jnp.zeros_like(ki_o)


def _prep_kv(small, gkv, gkh, gki, *, batch, tp, tk, qr, kvr, di):
    nq = tp // Q_BLOCK
    nk = tk // Q_BLOCK

    def row(b, i):
        return b * nq + jnp.minimum(i, nq - 1)

    return pl.pallas_call(
        functools.partial(_prep_kv_kernel, n_valid=nq),
        grid=(batch, nk),
        in_specs=[pl.BlockSpec((Q_BLOCK, kvr), lambda b, i: (row(b, i), qr // kvr)),
                  pl.BlockSpec((Q_BLOCK, di), lambda b, i: (row(b, i), (qr + kvr) // di)),
                  pl.BlockSpec((1, kvr), lambda b, i: (0, 0)),
                  pl.BlockSpec((1, kvr), lambda b, i: (0, 0)),
                  pl.BlockSpec((1, di), lambda b, i: (0, 0))],
        out_specs=[pl.BlockSpec((1, Q_BLOCK, kvr), lambda b, i: (b, i, 0)),
                   pl.BlockSpec((1, Q_BLOCK, kvr), lambda b, i: (b, i, 0)),
                   pl.BlockSpec((1, Q_BLOCK, di), lambda b, i: (b, i, 0))],
        out_shape=[jax.ShapeDtypeStruct((batch, tk, kvr), BF16),
                   jax.ShapeDtypeStruct((batch, tk, kvr), BF16),
                   jax.ShapeDtypeStruct((batch, tk, di), BF16)],
        name="prep_keys",
        compiler_params=_params("parallel", "arbitrary"),
    )(small, small, gkv.reshape(1, kvr), gkh.reshape(1, kvr), gki.reshape(1, di))


def _pool_kernel(a_ref, halo_ref, w_ref, sc_ref, o_ref, xs_ref, *, tp_tile, pad_front, pg):
    i = pl.program_id(1)
    t = i * tp_tile + lax.broadcasted_iota(I32, (tp_tile, 1), 0) - pad_front
    for g, win in enumerate(POOL_WINDOWS):
        cols = slice(g * pg, (g + 1) * pg)
        xs_ref[0:HALO, :] = halo_ref[0, :, cols].astype(F32)
        xs_ref[HALO:, :] = a_ref[0, :, cols].astype(F32)
        cur = xs_ref[pl.ds(HALO, tp_tile), :]
        acc = cur
        for k in range(1, win):
            acc = acc + xs_ref[pl.ds(HALO - k, tp_tile), :]
        cnt = jnp.clip(t + 1, 1, win).astype(F32)
        pooled = (acc / cnt - cur).astype(BF16)
        y = jnp.dot(pooled, w_ref[g], preferred_element_type=F32) * sc_ref[:, cols]
        o_ref[0, :, cols] = y.astype(o_ref.dtype)


def _pool_mixer(a, w_pool, pool_scale, *, batch, tp, pad_front):
    pw = a.shape[-1]
    n_groups, pg, _ = w_pool.shape
    tpt = _tile(tp, 1536, HALO)
    a3 = a.reshape(batch, tp, pw)
    halo_blocks = tpt // HALO
    out = pl.pallas_call(
        functools.partial(_pool_kernel, tp_tile=tpt, pad_front=pad_front, pg=pg),
        grid=(batch, tp // tpt),
        in_specs=[pl.BlockSpec((1, tpt, pw), lambda b, i: (b, i, 0)),
                  pl.BlockSpec((1, HALO, pw), lambda b, i: (b, jnp.maximum(i * halo_blocks - 1, 0), 0)),
                  pl.BlockSpec((n_groups, pg, pg), lambda b, i: (0, 0, 0)),
                  pl.BlockSpec((1, pw), lambda b, i: (0, 0))],
        out_specs=pl.BlockSpec((1, tpt, pw), lambda b, i: (b, i, 0)),
        out_shape=jax.ShapeDtypeStruct((batch, tp, pw), BF16),
        scratch_shapes=[pltpu.VMEM((tpt + HALO, pg), F32)],
        name="pool_mixer",
        compiler_params=_params("parallel", "arbitrary"),
    )(a3, a3, w_pool, pool_scale.reshape(1, pw))
    return out.reshape(batch * tp, pw)


def _qproj_kernel(x_ref, w_ref, g_ref, o_ref, *, scale, norm, heads, hd):
    acc = jnp.dot(x_ref[...], w_ref[...], preferred_element_type=F32)
    nblk = o_ref.shape[0]
    for hh in range(heads):
        a = acc[:, hh * hd:(hh + 1) * hd]
        if norm:
            a = a * lax.rsqrt(jnp.mean(a * a, axis=-1, keepdims=True) + EPS) * (g_ref[...] * scale)
        a = a.astype(o_ref.dtype)
        for r in range(nblk):
            o_ref[r, hh] = a[r * Q_BLOCK:(r + 1) * Q_BLOCK]


def _qproj(cq, w, g, *, n_heads, hd, heads_per_tile, scale, norm, name):
    n, r = cq.shape
    tm = _tile(n, 768, Q_BLOCK)
    nblk = tm // Q_BLOCK
    tn = heads_per_tile * hd
    return pl.pallas_call(
        functools.partial(_qproj_kernel, scale=scale, norm=norm, heads=heads_per_tile, hd=hd),
        grid=(n // tm, n_heads // heads_per_tile),
        in_specs=[pl.BlockSpec((tm, r), lambda i, j: (i, 0)),
                  pl.BlockSpec((r, tn), lambda i, j: (0, j)),
                  pl.BlockSpec((1, hd), lambda i, j: (0, 0))],
        out_specs=pl.BlockSpec((nblk, heads_per_tile, Q_BLOCK, hd), lambda i, j: (i, j, 0, 0)),
        out_shape=jax.ShapeDtypeStruct((n // Q_BLOCK, n_heads, Q_BLOCK, hd), BF16),
        name=name,
        compiler_params=_params("parallel", "arbitrary"),
    )(cq, w, g.reshape(1, hd))


def _attn_kernel(q_ref, qi_ref, wi_ref, ki_ref, kk_ref, lat_ref, wuv_ref, o_ref,
                 bias_ref, keyp_ref, acc_ref, m_ref, l_ref, *, k_sel, pad_front, idx_heads):
    j = pl.program_id(1)
    hg = pl.program_id(2)
    hps, qb, c = q_ref.shape[1], q_ref.shape[2], q_ref.shape[3]
    di = qi_ref.shape[3]
    v = wuv_ref.shape[2]
    n_t = (j * qb + qb + KEY_TILE - 1) // KEY_TILE
    nt_dims = (((1,), (1,)), ((), ()))

    @pl.when(hg == 0)
    def _select():
        tq = j * qb + lax.broadcasted_iota(I32, (qb, 1), 0)
        limit = ((jnp.maximum(tq, CHUNK) + CHUNK) // CHUNK) * CHUNK

        def score_tile(kt, carry):
            off = pl.multiple_of(kt * KEY_TILE, KEY_TILE)
            ki_t = ki_ref[0, pl.ds(off, KEY_TILE), :]
            part = jnp.zeros((qb, KEY_TILE), F32)
            hpi = min(INDEXER_HEADS_PER_DOT, idx_heads)
            for g in range(idx_heads // hpi):
                qg = qi_ref[0, g * hpi:(g + 1) * hpi].reshape(hpi * qb, di)
                s = lax.dot_general(qg, ki_t, nt_dims, preferred_element_type=F32)
                s = jnp.maximum(s, 0.0)
                for hh in range(hpi):
                    h = g * hpi + hh
                    part = part + s[hh * qb:(hh + 1) * qb] * wi_ref[:, h:h + 1]
            bits = pltpu.bitcast(part, I32)
            key = bits ^ ((bits >> 31) & 0x7FFFFFFF)
            s_idx = off + lax.broadcasted_iota(I32, (qb, KEY_TILE), 1)
            adm = (s_idx >= pad_front) & (s_idx < limit)
            keyp_ref[:, pl.ds(off, KEY_TILE)] = jnp.where(adm, key, INT_MIN)
            return carry

        lax.fori_loop(0, n_t, score_tile, 0)

        def count_ge(thr):
            def body(kt, cnt):
                off = pl.multiple_of(kt * KEY_TILE, KEY_TILE)
                ge = jnp.where(keyp_ref[:, pl.ds(off, KEY_TILE)] >= thr, 1.0, 0.0)
                for u in range(KEY_TILE // LANES):
                    cnt = cnt + ge[:, u * LANES:(u + 1) * LANES]
                return cnt
            cnt = lax.fori_loop(0, n_t, body, jnp.zeros((qb, LANES), F32))
            return jnp.sum(cnt, axis=1, keepdims=True)

        def bit_step(bi, thr):
            cand = thr + lax.shift_left(jnp.int32(1), 31 - bi)
            return jnp.where(count_ge(cand) >= k_sel, cand, thr)

        thr = lax.fori_loop(0, 32, bit_step, jnp.full((qb, 1), INT_MIN, I32))
        thr = jnp.maximum(thr, INT_MIN + 1)

        def bias_tile(kt, carry):
            off = pl.multiple_of(kt * KEY_TILE, KEY_TILE)
            sel = keyp_ref[:, pl.ds(off, KEY_TILE)] >= thr
            bias_ref[:, pl.ds(off, KEY_TILE)] = jnp.where(sel, 0.0, NEG)
            return carry

        lax.fori_loop(0, n_t, bias_tile, 0)

        @pl.when(jnp.max(count_ge(thr)) > k_sel)
        def _ties():
            quota = k_sel - count_ge(thr + 1)
            upper = (lax.broadcasted_iota(I32, (LANES, LANES), 0)
                     < lax.broadcasted_iota(I32, (LANES, LANES), 1)).astype(BF16)

            def chunk(ci, seen):
                off = pl.multiple_of(ci * LANES, LANES)
                kp = keyp_ref[:, pl.ds(off, LANES)]
                tie = jnp.where(kp == thr, 1.0, 0.0)
                rank = jnp.dot(tie.astype(BF16), upper, preferred_element_type=F32) + seen
                sel = jnp.where(kp > thr, 1.0, tie * jnp.where(rank < quota, 1.0, 0.0))
                bias_ref[:, pl.ds(off, LANES)] = jnp.where(sel > 0.5, 0.0, NEG)
                return seen + jnp.sum(tie, axis=1, keepdims=True)

            lax.fori_loop(0, n_t * (KEY_TILE // LANES), chunk, jnp.zeros((qb, 1), F32))

    m_ref[...] = jnp.full(m_ref.shape, -3e38, F32)
    l_ref[...] = jnp.zeros(l_ref.shape, F32)
    acc_ref[...] = jnp.zeros(acc_ref.shape, F32)
    hpc = min(HEADS_PER_CHAIN, hps)
    rc = hpc * qb
    k_slabs = KEY_TILE // LANES
    c_slabs = c // LANES

    def kv_step(kt, carry):
        off = pl.multiple_of(kt * KEY_TILE, KEY_TILE)
        kk_t = kk_ref[0, pl.ds(off, KEY_TILE), :]
        lat_t = lat_ref[0, pl.ds(off, KEY_TILE), :]
        bias_t = bias_ref[:, pl.ds(off, KEY_TILE)]
        for ci in range(hps // hpc):
            rs = slice(ci * rc, (ci + 1) * rc)
            q_c = q_ref[0, ci * hpc:(ci + 1) * hpc].reshape(rc, c)
            s = lax.dot_general(q_c, kk_t, nt_dims, preferred_element_type=F32)
            s = (s.reshape(hpc, qb, KEY_TILE) + bias_t[None]).reshape(rc, KEY_TILE)
            slabs = [s[:, u * LANES:(u + 1) * LANES] for u in range(k_slabs)]
            mx = slabs[0]
            for sl in slabs[1:]:
                mx = jnp.maximum(mx, sl)
            m_prev = m_ref[rs, :]
            m_new = jnp.maximum(m_prev, jnp.max(mx, axis=1, keepdims=True))
            alpha = jnp.exp(m_prev - m_new)
            ps = [jnp.exp(sl - m_new) for sl in slabs]
            psum = ps[0]
            for pu in ps[1:]:
                psum = psum + pu
            l_ref[rs, :] = alpha * l_ref[rs, :] + jnp.sum(psum, axis=1, keepdims=True)
            pv = jnp.dot(jnp.concatenate(ps, axis=1).astype(BF16), lat_t, preferred_element_type=F32)
            for u in range(c_slabs):
                cs = slice(u * LANES, (u + 1) * LANES)
                acc_ref[rs, cs] = alpha * acc_ref[rs, cs] + pv[:, cs]
            m_ref[rs, :] = m_new
        return carry

    lax.fori_loop(0, n_t, kv_step, 0)

    inv_l = 1.0 / l_ref[...]
    for hh in range(hps):
        rs = slice(hh * qb, (hh + 1) * qb)
        o = jnp.concatenate([acc_ref[rs, u * LANES:(u + 1) * LANES] * inv_l[rs] for u in range(c_slabs)], axis=1)
        y = jnp.dot(o.astype(BF16), wuv_ref[hh], preferred_element_type=F32)
        o_ref[:, hh * v:(hh + 1) * v] = y.astype(o_ref.dtype)


def _sparse_attention(q, qi, widx, ki, kk, lat, w_uv, *, batch, tp, k_sel, pad_front):
    nblk, n_heads, qb, c = q.shape
    idx_heads, di = qi.shape[1], qi.shape[3]
    tk = ki.shape[1]
    v = w_uv.shape[2]
    nq = tp // qb
    hps = min(HEADS_PER_STEP, n_heads)
    rows = hps * qb
    return pl.pallas_call(
        functools.partial(_attn_kernel, k_sel=k_sel, pad_front=pad_front, idx_heads=idx_heads),
        grid=(batch, nq, n_heads // hps),
        in_specs=[pl.BlockSpec((1, hps, qb, c), lambda b, j, h: (b * nq + j, h, 0, 0)),
                  pl.BlockSpec((1, idx_heads, qb, di), lambda b, j, h: (b * nq + j, 0, 0, 0)),
                  pl.BlockSpec((qb, LANES), lambda b, j, h: (b * nq + j, 0)),
                  pl.BlockSpec((1, tk, di), lambda b, j, h: (b, 0, 0)),
                  pl.BlockSpec((1, tk, c), lambda b, j, h: (b, 0, 0)),
                  pl.BlockSpec((1, tk, c), lambda b, j, h: (b, 0, 0)),
                  pl.BlockSpec((hps, c, v), lambda b, j, h: (h, 0, 0))],
        out_specs=pl.BlockSpec((qb, hps * v), lambda b, j, h: (b * nq + j, h)),
        out_shape=jax.ShapeDtypeStruct((nblk * qb, n_heads * v), BF16),
        scratch_shapes=[pltpu.VMEM((qb, tk), F32), pltpu.VMEM((qb, tk), I32),
                        pltpu.VMEM((rows, c), F32), pltpu.VMEM((rows, LANES), F32),
                        pltpu.VMEM((rows, LANES), F32)],
        name="sparse_attention",
        compiler_params=_params("parallel", "arbitrary", "arbitrary"),
    )(q, qi, widx, ki, kk, lat, w_uv)


def _merge_kernel(yp_ref, ya_ref, wp_ref, wa_ref, gp_ref, ga_ref, o_ref):
    pool = jnp.dot(yp_ref[...], wp_ref[...], preferred_element_type=F32)
    attn = jnp.dot(ya_ref[...], wa_ref[...], preferred_element_type=F32)
    o_ref[...] = (gp_ref[...].astype(F32) * pool + ga_ref[...].astype(F32) * attn).astype(o_ref.dtype)


def _merge(y_pool, y_attn, w_bp, w_ba, gates):
    n, pw = y_pool.shape
    aw = y_attn.shape[1]
    d = w_bp.shape[1]
    tm = _tile(n, 768, 128)
    tn = 512
    return pl.pallas_call(
        _merge_kernel,
        grid=(n // tm, d // tn),
        in_specs=[pl.BlockSpec((tm, pw), lambda i, j: (i, 0)),
                  pl.BlockSpec((tm, aw), lambda i, j: (i, 0)),
                  pl.BlockSpec((pw, tn), lambda i, j: (0, j)),
                  pl.BlockSpec((aw, tn), lambda i, j: (0, j)),
                  pl.BlockSpec((tm, tn), lambda i, j: (i, j)),
                  pl.BlockSpec((tm, tn), lambda i, j: (i, j + d // tn))],
        out_specs=pl.BlockSpec((tm, tn), lambda i, j: (i, j)),
        out_shape=jax.ShapeDtypeStruct((n, d), BF16),
        name="branch_merge",
        compiler_params=_params("parallel", "arbitrary"),
    )(y_pool, y_attn, w_bp, w_ba, gates, gates)


def _out_proj_kernel(x_ref, w_ref, r_ref, o_ref):
    o_ref[...] = r_ref[...] + jnp.dot(x_ref[...], w_ref[...], preferred_element_type=F32)


def _out_proj(merged, w_out, resid):
    n, k = merged.shape
    d = w_out.shape[1]
    tm = _tile(n, 768, 128)
    tn = 512
    return pl.pallas_call(
        _out_proj_kernel,
        grid=(n // tm, d // tn),
        in_specs=[pl.BlockSpec((tm, k), lambda i, j: (i, 0)),
                  pl.BlockSpec((k, tn), lambda i, j: (0, j)),
                  pl.BlockSpec((tm, tn), lambda i, j: (i, j))],
        out_specs=pl.BlockSpec((tm, tn), lambda i, j: (i, j)),
        out_shape=jax.ShapeDtypeStruct((n, d), F32),
        name="out_proj_residual",
        compiler_params=_params("parallel", "arbitrary"),
    )(merged, w_out, resid)


def _pack_halves(x):
    w = x.shape[1] // 2
    lo = pltpu.bitcast(x[:, :w].astype(BF16).astype(F32), I32)
    hi = pltpu.bitcast(x[:, w:].astype(BF16).astype(F32), I32)
    return ((lo >> 16) & 0xFFFF) | (hi & -65536)


def _unpack_halves(p):
    return pltpu.bitcast(p << 16, F32), pltpu.bitcast(p & -65536, F32)


def _router_kernel(h_ref, g_ref, rw_ref, rb_ref, xp_o, e_o, w_o, rank_o, cnt_o, carry_ref):
    i = pl.program_id(0)

    @pl.when(i == 0)
    def _():
        carry_ref[...] = jnp.zeros_like(carry_ref)

    x = h_ref[...]
    xn = x * lax.rsqrt(jnp.mean(x * x, axis=-1, keepdims=True) + EPS) * g_ref[...]
    xp_o[...] = _pack_halves(xn)
    logits = jnp.dot(xn.astype(BF16), rw_ref[...], preferred_element_type=F32) + rb_ref[...]
    tm = logits.shape[0]
    lane = lax.broadcasted_iota(I32, (tm, LANES), 1).astype(F32)
    vals, hots = [], []
    cur = logits
    for k in range(TOP_K_EXPERTS):
        m = jnp.max(cur, axis=1, keepdims=True)
        idx = jnp.min(jnp.where(cur == m, lane, float(LANES)), axis=1, keepdims=True)
        hot = lane == idx
        vals.append(m)
        hots.append(hot)
        e_o[:, k:k + 1] = idx.astype(I32)
        cur = jnp.where(hot, -jnp.inf, cur)
    exps = [jnp.exp(vk - vals[0]) for vk in vals]
    denom = exps[0]
    for ek in exps[1:]:
        denom = denom + ek
    for k in range(TOP_K_EXPERTS):
        w_o[:, k:k + 1] = exps[k] / denom
    onehot = jnp.zeros((tm, LANES), F32)
    for hot in hots:
        onehot = onehot + jnp.where(hot, 1.0, 0.0)
    lower = (lax.broadcasted_iota(I32, (tm, tm), 0) > lax.broadcasted_iota(I32, (tm, tm), 1)).astype(BF16)
    before = jnp.dot(lower, onehot.astype(BF16), preferred_element_type=F32) + carry_ref[...]
    for k in range(TOP_K_EXPERTS):
        rank_o[:, k:k + 1] = jnp.sum(jnp.where(hots[k], before, 0.0), axis=1, keepdims=True).astype(I32)
    carry_ref[...] = carry_ref[...] + jnp.sum(onehot, axis=0, keepdims=True)
    cnt_o[...] = carry_ref[...]


def _router(h2, norm2_g, rw, rb):
    n, d = h2.shape
    tm = ROUTE_TILE
    kk = TOP_K_EXPERTS
    return pl.pallas_call(
        _router_kernel,
        grid=(n // tm,),
        in_specs=[pl.BlockSpec((tm, d), lambda i: (i, 0)),
                  pl.BlockSpec((1, d), lambda i: (0, 0)),
                  pl.BlockSpec((d, LANES), lambda i: (0, 0)),
                  pl.BlockSpec((1, LANES), lambda i: (0, 0))],
        out_specs=[pl.BlockSpec((tm, d // 2), lambda i: (i, 0)),
                   pl.BlockSpec((tm, kk), lambda i: (i, 0)),
                   pl.BlockSpec((tm, kk), lambda i: (i, 0)),
                   pl.BlockSpec((tm, kk), lambda i: (i, 0)),
                   pl.BlockSpec((1, LANES), lambda i: (0, 0))],
        out_shape=[jax.ShapeDtypeStruct((n, d // 2), I32),
                   jax.ShapeDtypeStruct((n, kk), I32),
                   jax.ShapeDtypeStruct((n, kk), F32),
                   jax.ShapeDtypeStruct((n, kk), I32),
                   jax.ShapeDtypeStruct((1, LANES), F32)],
        scratch_shapes=[pltpu.VMEM((1, LANES), F32)],
        name="moe_router",
        compiler_params=_params("arbitrary"),
    )(h2, norm2_g.reshape(1, d), rw, rb)


def _row_copy(src_ref, src_row, dst_ref, dst_row, sem):
    return pltpu.make_async_copy(src_ref.at[pl.ds(src_row, 1), :], dst_ref.at[pl.ds(dst_row, 1), :], sem)


def _dispatch_kernel(dest_ref, xp_ref, init_ref, xs_ref, sem):
    del init_ref
    tm = xp_ref.shape[0]
    base = pl.program_id(0) * tm * TOP_K_EXPERTS

    def issue(r, carry):
        for k in range(TOP_K_EXPERTS):
            _row_copy(xp_ref, r, xs_ref, dest_ref[base + r * TOP_K_EXPERTS + k], sem).start()
        return carry

    def drain(r, carry):
        for k in range(TOP_K_EXPERTS):
            _row_copy(xp_ref, 0, xs_ref, 0, sem).wait()
        return carry

    lax.fori_loop(0, tm, issue, 0)
    lax.fori_loop(0, tm, drain, 0)


def _dispatch(dest_flat, xp, n_slots):
    n, w = xp.shape
    tm = ROUTE_TILE
    return pl.pallas_call(
        _dispatch_kernel,
        grid_spec=pltpu.PrefetchScalarGridSpec(
            num_scalar_prefetch=1,
            grid=(n // tm,),
            in_specs=[pl.BlockSpec((tm, w), lambda i, dest: (i, 0)),
                      pl.BlockSpec(memory_space=pl.ANY)],
            out_specs=pl.BlockSpec(memory_space=pl.ANY),
            scratch_shapes=[pltpu.SemaphoreType.DMA(())]),
        out_shape=jax.ShapeDtypeStruct((n_slots, w), I32),
        input_output_aliases={2: 0},
        name="moe_dispatch",
        compiler_params=_params("arbitrary"),
    )(dest_flat, xp, jnp.zeros((n_slots, w), I32))


def _deinterleave_kernel(w_ref, o_ref):
    blk = 2 * LANES
    dst = lax.broadcasted_iota(I32, (blk, blk), 1)
    src = jnp.where(dst < LANES, 2 * dst, 2 * (dst - LANES) + 1)
    perm = (lax.broadcasted_iota(I32, (blk, blk), 0) == src).astype(BF16)
    for b in range(w_ref.shape[2] // blk):
        cols = slice(b * blk, (b + 1) * blk)
        o_ref[0, :, cols] = jnp.dot(w_ref[0, :, cols].astype(BF16), perm,
                                    preferred_element_type=F32).astype(o_ref.dtype)


def _deinterleave_gate_up(w_gate_up):
    n_exp, d, de2 = w_gate_up.shape
    assert de2 % (2 * LANES) == 0
    tk = _tile(d, 512, 16)
    return pl.pallas_call(
        _deinterleave_kernel,
        grid=(n_exp, d // tk),
        in_specs=[pl.BlockSpec((1, tk, de2), lambda e, i: (e, i, 0))],
        out_specs=pl.BlockSpec((1, tk, de2), lambda e, i: (e, i, 0)),
        out_shape=jax.ShapeDtypeStruct((n_exp, d, de2), BF16),
        name="moe_weight_regroup",
        compiler_params=_params("parallel", "arbitrary"),
    )(w_gate_up)


def _expert_up_kernel(te_ref, tv_ref, xs_ref, w_ref, b_ref, o_ref):
    t = pl.program_id(0)

    @pl.when(tv_ref[t] > 0)
    def _():
        lo, hi = _unpack_halves(xs_ref[...])
        x = jnp.concatenate([lo.astype(BF16), hi.astype(BF16)], axis=1)
        gu = jnp.dot(x, w_ref[0], preferred_element_type=F32) + b_ref[0]
        for blk in range(gu.shape[1] // (2 * LANES)):
            g0 = blk * 2 * LANES
            gate = jnp.minimum(gu[:, g0:g0 + LANES], SWIGLU_LIMIT)
            up = jnp.clip(gu[:, g0 + LANES:g0 + 2 * LANES], -SWIGLU_LIMIT, SWIGLU_LIMIT)
            act = (up + 1.0) * gate * jax.nn.sigmoid(SWIGLU_ALPHA * gate)
            o_ref[:, blk * LANES:(blk + 1) * LANES] = act.astype(o_ref.dtype)

    @pl.when(tv_ref[t] == 0)
    def _():
        o_ref[...] = jnp.zeros_like(o_ref)


def _expert_up(tile_expert, tile_valid, xs, w_gu, b_gu):
    n_slots, w = xs.shape
    n_exp, d, de2 = w_gu.shape
    tm = EXPERT_TILE
    return pl.pallas_call(
        _expert_up_kernel,
        grid_spec=pltpu.PrefetchScalarGridSpec(
            num_scalar_prefetch=2,
            grid=(n_slots // tm,),
            in_specs=[pl.BlockSpec((tm, w), lambda t, te, tv: (t, 0)),
                      pl.BlockSpec((1, d, de2), lambda t, te, tv: (te[t], 0, 0)),
                      pl.BlockSpec((1, 1, de2), lambda t, te, tv: (te[t], 0, 0))],
            out_specs=pl.BlockSpec((tm, de2 // 2), lambda t, te, tv: (t, 0))),
        out_shape=jax.ShapeDtypeStruct((n_slots, de2 // 2), BF16),
        name="moe_expert_up",
        compiler_params=_params("arbitrary"),
    )(tile_expert, tile_valid, xs, w_gu, b_gu)


def _expert_down_kernel(te_ref, tv_ref, a_ref, w_ref, b_ref, o_ref):
    t = pl.program_id(0)

    @pl.when(tv_ref[t] > 0)
    def _():
        y = jnp.dot(a_ref[...], w_ref[0], preferred_element_type=F32) + b_ref[0]
        o_ref[...] = _pack_halves(y)

    @pl.when(tv_ref[t] == 0)
    def _():
        o_ref[...] = jnp.zeros_like(o_ref)


def _expert_down(tile_expert, tile_valid, act, w_down, b_down):
    n_slots, de = act.shape
    n_exp, _, d = w_down.shape
    tm = EXPERT_TILE
    return pl.pallas_call(
        _expert_down_kernel,
        grid_spec=pltpu.PrefetchScalarGridSpec(
            num_scalar_prefetch=2,
            grid=(n_slots // tm,),
            in_specs=[pl.BlockSpec((tm, de), lambda t, te, tv: (t, 0)),
                      pl.BlockSpec((1, de, d), lambda t, te, tv: (te[t], 0, 0)),
                      pl.BlockSpec((1, 1, d), lambda t, te, tv: (te[t], 0, 0))],
            out_specs=pl.BlockSpec((tm, d // 2), lambda t, te, tv: (t, 0))),
        out_shape=jax.ShapeDtypeStruct((n_slots, d // 2), I32),
        name="moe_expert_down",
        compiler_params=_params("arbitrary"),
    )(tile_expert, tile_valid, act, w_down, b_down)


def _combine_kernel(dest_ref, w_ref, h_ref, ye_ref, o_ref, buf_ref, sem, *, nq):
    b = pl.program_id(0)
    i = pl.program_id(1)
    tm = h_ref.shape[0]
    half = buf_ref.shape[2]
    base = ((b * nq + 1 + i) * tm) * TOP_K_EXPERTS

    def issue(r, carry):
        for k in range(TOP_K_EXPERTS):
            _row_copy(ye_ref, dest_ref[base + r * TOP_K_EXPERTS + k], buf_ref.at[k], r, sem).start()
        return carry

    def drain(r, carry):
        for k in range(TOP_K_EXPERTS):
            _row_copy(ye_ref, 0, buf_ref.at[k], 0, sem).wait()
        return carry

    lax.fori_loop(0, tm, issue, 0)
    lax.fori_loop(0, tm, drain, 0)

    lo_sum = h_ref[:, :half]
    hi_sum = h_ref[:, half:]
    for k in range(TOP_K_EXPERTS):
        lo, hi = _unpack_halves(buf_ref[k])
        wk = w_ref[:, k:k + 1]
        lo_sum = lo_sum + wk * lo
        hi_sum = hi_sum + wk * hi
    o_ref[0, :, :half] = lo_sum
    o_ref[0, :, half:] = hi_sum


def _combine(dest_flat, sel_w, h2, ye, *, batch, seq, tp):
    d = h2.shape[1]
    tm = Q_BLOCK
    nq = tp // tm
    return pl.pallas_call(
        functools.partial(_combine_kernel, nq=nq),
        grid_spec=pltpu.PrefetchScalarGridSpec(
            num_scalar_prefetch=1,
            grid=(batch, seq // tm),
            in_specs=[pl.BlockSpec((tm, TOP_K_EXPERTS), lambda b, i, dest: (b * nq + 1 + i, 0)),
                      pl.BlockSpec((tm, d), lambda b, i, dest: (b * nq + 1 + i, 0)),
                      pl.BlockSpec(memory_space=pl.ANY)],
            out_specs=pl.BlockSpec((1, tm, d), lambda b, i, dest: (b, i, 0)),
            scratch_shapes=[pltpu.VMEM((TOP_K_EXPERTS, tm, d // 2), I32), pltpu.SemaphoreType.DMA(())]),
        out_shape=jax.ShapeDtypeStruct((batch, seq, d), F32),
        name="moe_combine",
        compiler_params=_params("arbitrary", "arbitrary"),
    )(dest_flat, sel_w, h2, ye)


def _moe(h2, norm2_g, router_w, router_b, w_gate_up, b_gate_up, w_down, b_down, *, batch, seq, tp):
    n, d = h2.shape
    n_exp = router_w.shape[1]
    de = w_down.shape[1]
    assert n_exp <= LANES and n % ROUTE_TILE == 0
    rw = jnp.pad(router_w, ((0, 0), (0, LANES - n_exp))).astype(BF16)
    rb = jnp.pad(router_b, (0, LANES - n_exp), constant_values=NEG).reshape(1, LANES)
    xp, sel_e, sel_w, rank, counts = _router(h2, norm2_g, rw, rb)

    cnt = counts[0, :n_exp].astype(I32)
    padded = ((cnt + EXPERT_TILE - 1) // EXPERT_TILE) * EXPERT_TILE
    e_ids = jnp.arange(n_exp, dtype=I32)
    ends = jnp.sum(jnp.where(e_ids[:, None] <= e_ids[None, :], padded[:, None], 0), axis=0)
    starts = ends - padded
    total = ends[n_exp - 1]
    dest_flat = (starts[sel_e] + rank).reshape(-1)
    n_tiles = (n * TOP_K_EXPERTS) // EXPERT_TILE + n_exp
    tile_start = jnp.arange(n_tiles, dtype=I32) * EXPERT_TILE
    tile_valid = (tile_start < total).astype(I32)
    last_expert = jnp.sum((ends <= total - 1).astype(I32))
    tile_expert = jnp.minimum(jnp.sum((ends[None, :] <= tile_start[:, None]).astype(I32), axis=1), last_expert)

    w_gu = _deinterleave_gate_up(w_gate_up)
    nb = (2 * de) // (2 * LANES)
    b_gu = b_gate_up.reshape(n_exp, nb, LANES, 2).transpose(0, 1, 3, 2).reshape(n_exp, 1, 2 * de)

    xs = _dispatch(dest_flat, xp, n_tiles * EXPERT_TILE)
    act = _expert_up(tile_expert, tile_valid, xs, w_gu, b_gu)
    ye = _expert_down(tile_expert, tile_valid, act, w_down.astype(BF16), b_down.reshape(n_exp, 1, d))
    return _combine(dest_flat, sel_w, h2, ye, batch=batch, seq=seq, tp=tp)


def kernel(x, meta_tokens, norm1_g, w_in, q_norm_g, w_uq, w_iq, kv_norm_g, q_head_norm_g, k_head_norm_g,
           idx_k_norm_g, w_uv, w_branch_attn, w_pool, pool_scale, w_branch_pool, w_out, norm2_g, router_w,
           router_b, w_gate_up, b_gate_up, w_down, b_down):
    batch, seq, d = x.shape
    n_meta = meta_tokens.shape[0]
    depth = norm1_g.shape[0]
    qr, kvr, di = q_norm_g.shape[1], kv_norm_g.shape[1], idx_k_norm_g.shape[1]
    n_heads, v = w_uv.shape[1], w_uv.shape[3]
    idx_heads = w_iq.shape[2] // di
    pw = pool_scale.shape[1]
    assert depth == 1
    assert n_meta <= CHUNK and seq % Q_BLOCK == 0 and idx_heads <= LANES
    assert qr % kvr == 0 and (qr + kvr) % di == 0 and di % LANES == 0
    pad_front = Q_BLOCK - n_meta
    tp = Q_BLOCK + seq
    tk = -(-tp // KEY_TILE) * KEY_TILE
    k_sel = min(TOPK_MAX, seq // 4)
    attn_scale = kvr ** -0.5
    idx_scale = (di ** -0.5) * (idx_heads ** -0.5)

    meta = jnp.broadcast_to(meta_tokens[None].astype(x.dtype), (batch, n_meta, d))
    h = jnp.concatenate([jnp.zeros((batch, pad_front, d), x.dtype), meta, x], axis=1).reshape(batch * tp, d)

    out = None
    for l in range(depth):
        n_small = qr + kvr + di
        o_pool = n_small + idx_heads
        o_gate = o_pool + pw
        w_small = jnp.pad(w_in[l][:, :o_pool], ((0, 0), (0, LANES - idx_heads))).astype(BF16)
        w_a = w_in[l][:, o_pool:o_gate].astype(BF16)
        w_g = w_in[l][:, o_gate:].astype(BF16)

        xn = _rmsnorm_rows(h, norm1_g[l], BF16)
        small = _matmul(xn, w_small, F32, tn=_tile(w_small.shape[1], 512, LANES), name="in_proj_latents")
        a_pool = _matmul(xn, w_a, BF16, tn=512, name="in_proj_pool")
        gates = _matmul(xn, w_g, BF16, tn=512, sigmoid=True, name="in_proj_gates")

        y_pool = _pool_mixer(a_pool, w_pool[l].astype(BF16), pool_scale[l], batch=batch, tp=tp,
                             pad_front=pad_front)

        cq, widx = _prep_q(small, q_norm_g[l], qr, n_small, idx_scale)
        lat, kk, ki = _prep_kv(small, kv_norm_g[l], k_head_norm_g[l], idx_k_norm_g[l],
                               batch=batch, tp=tp, tk=tk, qr=qr, kvr=kvr, di=di)
        q = _qproj(cq, w_uq[l].astype(BF16), q_head_norm_g[l], n_heads=n_heads, hd=kvr, heads_per_tile=1,
                   scale=attn_scale, norm=True, name="q_proj_headnorm")
        qi = _qproj(cq, w_iq[l].astype(BF16), jnp.ones((di,), F32), n_heads=idx_heads, hd=di,
                    heads_per_tile=min(4, idx_heads), scale=1.0, norm=False, name="indexer_q_proj")
        y_attn = _sparse_attention(q, qi, widx, ki, kk, lat, w_uv[l].astype(BF16), batch=batch, tp=tp,
                                   k_sel=k_sel, pad_front=pad_front)

        merged = _merge(y_pool, y_attn, w_branch_pool[l].astype(BF16), w_branch_attn[l].astype(BF16), gates)
        h2 = _out_proj(merged, w_out[l].astype(BF16), h)

        out = _moe(h2, norm2_g[l], router_w[l], router_b[l], w_gate_up[l], b_gate_up[l], w_down[l],
                   b_down[l], batch=batch, seq=seq, tp=tp)
    return out
```

```python
import functools

import jax
import jax.numpy as jnp
from jax import lax
from jax.experimental import pallas as pl
from jax.experimental.pallas import tpu as pltpu

F32 = jnp.float32
BF16 = jnp.bfloat16
I32 = jnp.int32

CHUNK = 64
POOL_WINDOWS = (2, 4, 8, 16)
TOPK_MAX = 256
TOP_K_EXPERTS = 4
SWIGLU_LIMIT = 7.0
SWIGLU_ALPHA = 1.702
EPS = 1e-6
NEG = -1e30

Q_BLOCK = 128
KEY_TILE = 512
HEADS_PER_STEP = 16
INDEXER_HEADS_PER_DOT = 8
HEADS_PER_CHAIN = 4
HALO = 16
EXPERT_TILE = 256
ROUTE_TILE = 256
LANES = 128
INT_MIN = -2 ** 31
MAX_EXP_SPAN = 80.0
BOUND_MARGIN = 1.02
VMEM_LIMIT = 56 * 1024 * 1024


def _tile(n, target, mult):
    best = None
    for t in range(mult, min(n, target) + 1, mult):
        if n % t == 0:
            best = t
    assert best is not None, (n, target, mult)
    return best


def _params(*sem):
    return pltpu.CompilerParams(dimension_semantics=sem, vmem_limit_bytes=VMEM_LIMIT)


def _rmsnorm_kernel(x_ref, g_ref, o_ref):
    x = x_ref[...]
    ms = jnp.mean(x * x, axis=-1, keepdims=True)
    o_ref[...] = (x * lax.rsqrt(ms + EPS) * g_ref[...]).astype(o_ref.dtype)


def _rmsnorm_rows(x, g, out_dtype):
    n, d = x.shape
    tm = _tile(n, 256, 8)
    return pl.pallas_call(
        _rmsnorm_kernel,
        grid=(n // tm,),
        in_specs=[pl.BlockSpec((tm, d), lambda i: (i, 0)), pl.BlockSpec((1, d), lambda i: (0, 0))],
        out_specs=pl.BlockSpec((tm, d), lambda i: (i, 0)),
        out_shape=jax.ShapeDtypeStruct((n, d), out_dtype),
        name="rmsnorm_rows",
        compiler_params=_params("parallel"),
    )(x, g.reshape(1, d))


def _matmul_kernel(x_ref, w_ref, o_ref, *, sigmoid):
    acc = jnp.dot(x_ref[...], w_ref[...], preferred_element_type=F32)
    if sigmoid:
        acc = jax.nn.sigmoid(acc)
    o_ref[...] = acc.astype(o_ref.dtype)


def _matmul(x, w, out_dtype, *, tn, name, sigmoid=False):
    m, k = x.shape
    n = w.shape[1]
    tm = _tile(m, 768, 128)
    return pl.pallas_call(
        functools.partial(_matmul_kernel, sigmoid=sigmoid),
        grid=(m // tm, n // tn),
        in_specs=[pl.BlockSpec((tm, k), lambda i, j: (i, 0)), pl.BlockSpec((k, tn), lambda i, j: (0, j))],
        out_specs=pl.BlockSpec((tm, tn), lambda i, j: (i, j)),
        out_shape=jax.ShapeDtypeStruct((m, n), out_dtype),
        name=name,
        compiler_params=_params("parallel", "arbitrary"),
    )(x, w)


def _prep_q_kernel(cq_ref, wi_ref, g_ref, cq_o, wi_o, *, idx_scale):
    x = cq_ref[...]
    ms = jnp.mean(x * x, axis=-1, keepdims=True)
    cq_o[...] = (x * lax.rsqrt(ms + EPS) * g_ref[...]).astype(cq_o.dtype)
    wi_o[...] = wi_ref[...] * idx_scale


def _prep_q(small, q_norm_g, qr, wi_col, idx_scale):
    n = small.shape[0]
    tm = _tile(n, 768, 128)
    return pl.pallas_call(
        functools.partial(_prep_q_kernel, idx_scale=idx_scale),
        grid=(n // tm,),
        in_specs=[pl.BlockSpec((tm, qr), lambda i: (i, 0)),
                  pl.BlockSpec((tm, LANES), lambda i: (i, wi_col // LANES)),
                  pl.BlockSpec((1, qr), lambda i: (0, 0))],
        out_specs=[pl.BlockSpec((tm, qr), lambda i: (i, 0)), pl.BlockSpec((tm, LANES), lambda i: (i, 0))],
        out_shape=[jax.ShapeDtypeStruct((n, qr), BF16), jax.ShapeDtypeStruct((n, LANES), F32)],
        name="prep_query_latent",
        compiler_params=_params("parallel"),
    )(small, small, q_norm_g.reshape(1, qr))


def _prep_kv_kernel(ckv_ref, kidx_ref, gkv_ref, gkh_ref, gki_ref, lat_o, kk_o, ki_o, *, n_valid):
    i = pl.program_id(1)

    @pl.when(i < n_valid)
    def _():
        c = ckv_ref[...]
        lat = c * lax.rsqrt(jnp.mean(c * c, axis=-1, keepdims=True) + EPS) * gkv_ref[...]
        kk = lat * lax.rsqrt(jnp.mean(lat * lat, axis=-1, keepdims=True) + EPS) * gkh_ref[...]
        k = kidx_ref[...]
        ki = k * lax.rsqrt(jnp.mean(k * k, axis=-1, keepdims=True) + EPS) * gki_ref[...]
        lat_o[0] = lat.astype(lat_o.dtype)
        kk_o[0] = kk.astype(kk_o.dtype)
        ki_o[0] = ki.astype(ki_o.dtype)

    @pl.when(i >= n_valid)
    def _():
        lat_o[...] = jnp.zeros_like(lat_o)
        kk_o[...] = jnp.zeros_like(kk_o)
        ki_o[...] = jnp.zeros_like(ki_o)


def _prep_kv(small, gkv, gkh, gki, *, batch, tp, tk, qr, kvr, di):
    nq = tp // Q_BLOCK
    nk = tk // Q_BLOCK

    def row(b, i):
        return b * nq + jnp.minimum(i, nq - 1)

    return pl.pallas_call(
        functools.partial(_prep_kv_kernel, n_valid=nq),
        grid=(batch, nk),
        in_specs=[pl.BlockSpec((Q_BLOCK, kvr), lambda b, i: (row(b, i), qr // kvr)),
                  pl.BlockSpec((Q_BLOCK, di), lambda b, i: (row(b, i), (qr + kvr) // di)),
                  pl.BlockSpec((1, kvr), lambda b, i: (0, 0)),
                  pl.BlockSpec((1, kvr), lambda b, i: (0, 0)),
                  pl.BlockSpec((1, di), lambda b, i: (0, 0))],
        out_specs=[pl.BlockSpec((1, Q_BLOCK, kvr), lambda b, i: (b, i, 0)),
                   pl.BlockSpec((1, Q_BLOCK, kvr), lambda b, i: (b, i, 0)),
                   pl.BlockSpec((1, Q_BLOCK, di), lambda b, i: (b, i, 0))],
        out_shape=[jax.ShapeDtypeStruct((batch, tk, kvr), BF16),
                   jax.ShapeDtypeStruct((batch, tk, kvr), BF16),
                   jax.ShapeDtypeStruct((batch, tk, di), BF16)],
        name="prep_keys",
        compiler_params=_params("parallel", "arbitrary"),
    )(small, small, gkv.reshape(1, kvr), gkh.reshape(1, kvr), gki.reshape(1, di))


def _pool_kernel(a_ref, halo_ref, w_ref, sc_ref, o_ref, xs_ref, *, tp_tile, pad_front, pg):
    i = pl.program_id(1)
    t = i * tp_tile + lax.broadcasted_iota(I32, (tp_tile, 1), 0) - pad_front
    for g, win in enumerate(POOL_WINDOWS):
        cols = slice(g * pg, (g + 1) * pg)
        xs_ref[0:HALO, :] = halo_ref[0, :, cols].astype(F32)
        xs_ref[HALO:, :] = a_ref[0, :, cols].astype(F32)
        cur = xs_ref[pl.ds(HALO, tp_tile), :]
        acc = cur
        for k in range(1, win):
            acc = acc + xs_ref[pl.ds(HALO - k, tp_tile), :]
        cnt = jnp.clip(t + 1, 1, win).astype(F32)
        pooled = (acc / cnt - cur).astype(BF16)
        y = jnp.dot(pooled, w_ref[g], preferred_element_type=F32) * sc_ref[:, cols]
        o_ref[0, :, cols] = y.astype(o_ref.dtype)


def _pool_mixer(a, w_pool, pool_scale, *, batch, tp, pad_front):
    pw = a.shape[-1]
    n_groups, pg, _ = w_pool.shape
    tpt = _tile(tp, 1536, HALO)
    a3 = a.reshape(batch, tp, pw)
    halo_blocks = tpt // HALO
    out = pl.pallas_call(
        functools.partial(_pool_kernel, tp_tile=tpt, pad_front=pad_front, pg=pg),
        grid=(batch, tp // tpt),
        in_specs=[pl.BlockSpec((1, tpt, pw), lambda b, i: (b, i, 0)),
                  pl.BlockSpec((1, HALO, pw), lambda b, i: (b, jnp.maximum(i * halo_blocks - 1, 0), 0)),
                  pl.BlockSpec((n_groups, pg, pg), lambda b, i: (0, 0, 0)),
                  pl.BlockSpec((1, pw), lambda b, i: (0, 0))],
        out_specs=pl.BlockSpec((1, tpt, pw), lambda b, i: (b, i, 0)),
        out_shape=jax.ShapeDtypeStruct((batch, tp, pw), BF16),
        scratch_shapes=[pltpu.VMEM((tpt + HALO, pg), F32)],
        name="pool_mixer",
        compiler_params=_params("parallel", "arbitrary"),
    )(a3, a3, w_pool, pool_scale.reshape(1, pw))
    return out.reshape(batch * tp, pw)


def _qproj_kernel(x_ref, w_ref, g_ref, o_ref, *, scale, norm, heads, hd):
    acc = jnp.dot(x_ref[...], w_ref[...], preferred_element_type=F32)
    nblk = o_ref.shape[0]
    for hh in range(heads):
        a = acc[:, hh * hd:(hh + 1) * hd]
        if norm:
            a = a * lax.rsqrt(jnp.mean(a * a, axis=-1, keepdims=True) + EPS) * (g_ref[...] * scale)
        a = a.astype(o_ref.dtype)
        for r in range(nblk):
            o_ref[r, hh] = a[r * Q_BLOCK:(r + 1) * Q_BLOCK]


def _qproj(cq, w, g, *, n_heads, hd, heads_per_tile, scale, norm, name):
    n, r = cq.shape
    tm = _tile(n, 1536, Q_BLOCK)
    nblk = tm // Q_BLOCK
    tn = heads_per_tile * hd
    return pl.pallas_call(
        functools.partial(_qproj_kernel, scale=scale, norm=norm, heads=heads_per_tile, hd=hd),
        grid=(n // tm, n_heads // heads_per_tile),
        in_specs=[pl.BlockSpec((tm, r), lambda i, j: (i, 0)),
                  pl.BlockSpec((r, tn), lambda i, j: (0, j)),
                  pl.BlockSpec((1, hd), lambda i, j: (0, 0))],
        out_specs=pl.BlockSpec((nblk, heads_per_tile, Q_BLOCK, hd), lambda i, j: (i, j, 0, 0)),
        out_shape=jax.ShapeDtypeStruct((n // Q_BLOCK, n_heads, Q_BLOCK, hd), BF16),
        name=name,
        compiler_params=_params("parallel", "arbitrary"),
    )(cq, w, g.reshape(1, hd))


def _attn_kernel(shift_ref, q_ref, qi_ref, wi_ref, ki_ref, kk_ref, lat_ref, wuv_ref, o_ref,
                 bias_ref, keyp_ref, acc_ref, m_ref, l_ref, *, k_sel, pad_front, idx_heads):
    j = pl.program_id(1)
    hg = pl.program_id(2)
    shift = shift_ref[0]
    use_shift = shift_ref[1] > 0.5
    hps, qb, c = q_ref.shape[1], q_ref.shape[2], q_ref.shape[3]
    di = qi_ref.shape[3]
    v = wuv_ref.shape[2]
    n_t = (j * qb + qb + KEY_TILE - 1) // KEY_TILE
    nt_dims = (((1,), (1,)), ((), ()))

    @pl.when(hg == 0)
    def _select():
        tq = j * qb + lax.broadcasted_iota(I32, (qb, 1), 0)
        limit = ((jnp.maximum(tq, CHUNK) + CHUNK) // CHUNK) * CHUNK

        def score_tile(kt, carry):
            off = pl.multiple_of(kt * KEY_TILE, KEY_TILE)
            ki_t = ki_ref[0, pl.ds(off, KEY_TILE), :]
            part = jnp.zeros((qb, KEY_TILE), F32)
            hpi = min(INDEXER_HEADS_PER_DOT, idx_heads)
            for g in range(idx_heads // hpi):
                qg = qi_ref[0, g * hpi:(g + 1) * hpi].reshape(hpi * qb, di)
                s = lax.dot_general(qg, ki_t, nt_dims, preferred_element_type=F32)
                s = jnp.maximum(s, 0.0)
                for hh in range(hpi):
                    h = g * hpi + hh
                    part = part + s[hh * qb:(hh + 1) * qb] * wi_ref[:, h:h + 1]
            bits = pltpu.bitcast(part, I32)
            key = bits ^ ((bits >> 31) & 0x7FFFFFFF)
            s_idx = off + lax.broadcasted_iota(I32, (qb, KEY_TILE), 1)
            adm = (s_idx >= pad_front) & (s_idx < limit)
            keyp_ref[:, pl.ds(off, KEY_TILE)] = jnp.where(adm, key, INT_MIN)
            return carry

        lax.fori_loop(0, n_t, score_tile, 0)

        def count_ge(thr):
            def body(kt, cnt):
                off = pl.multiple_of(kt * KEY_TILE, KEY_TILE)
                ge = jnp.where(keyp_ref[:, pl.ds(off, KEY_TILE)] >= thr, 1.0, 0.0)
                for u in range(KEY_TILE // LANES):
                    cnt = cnt + ge[:, u * LANES:(u + 1) * LANES]
                return cnt
            cnt = lax.fori_loop(0, n_t, body, jnp.zeros((qb, LANES), F32))
            return jnp.sum(cnt, axis=1, keepdims=True)

        def bit_step(bi, thr):
            cand = thr + lax.shift_left(jnp.int32(1), 31 - bi)
            return jnp.where(count_ge(cand) >= k_sel, cand, thr)

        thr = lax.fori_loop(0, 32, bit_step, jnp.full((qb, 1), INT_MIN, I32))
        thr = jnp.maximum(thr, INT_MIN + 1)

        def bias_tile(kt, carry):
            off = pl.multiple_of(kt * KEY_TILE, KEY_TILE)
            sel = keyp_ref[:, pl.ds(off, KEY_TILE)] >= thr
            bias_ref[:, pl.ds(off, KEY_TILE)] = jnp.where(sel, -shift, NEG)
            return carry

        lax.fori_loop(0, n_t, bias_tile, 0)

        @pl.when(jnp.max(count_ge(thr)) > k_sel)
        def _ties():
            quota = k_sel - count_ge(thr + 1)
            upper = (lax.broadcasted_iota(I32, (LANES, LANES), 0)
                     < lax.broadcasted_iota(I32, (LANES, LANES), 1)).astype(BF16)

            def chunk(ci, seen):
                off = pl.multiple_of(ci * LANES, LANES)
                kp = keyp_ref[:, pl.ds(off, LANES)]
                tie = jnp.where(kp == thr, 1.0, 0.0)
                rank = jnp.dot(tie.astype(BF16), upper, preferred_element_type=F32) + seen
                sel = jnp.where(kp > thr, 1.0, tie * jnp.where(rank < quota, 1.0, 0.0))
                bias_ref[:, pl.ds(off, LANES)] = jnp.where(sel > 0.5, -shift, NEG)
                return seen + jnp.sum(tie, axis=1, keepdims=True)

            lax.fori_loop(0, n_t * (KEY_TILE // LANES), chunk, jnp.zeros((qb, 1), F32))

    l_ref[...] = jnp.zeros(l_ref.shape, F32)
    acc_ref[...] = jnp.zeros(acc_ref.shape, F32)
    hpc = min(HEADS_PER_CHAIN, hps)
    rc = hpc * qb
    k_slabs = KEY_TILE // LANES
    c_slabs = c // LANES

    def masked_logits(ci, off):
        q_c = q_ref[0, ci * hpc:(ci + 1) * hpc].reshape(rc, c)
        s = lax.dot_general(q_c, kk_ref[0, pl.ds(off, KEY_TILE), :], nt_dims, preferred_element_type=F32)
        s = (s.reshape(hpc, qb, KEY_TILE) + bias_ref[:, pl.ds(off, KEY_TILE)][None]).reshape(rc, KEY_TILE)
        return [s[:, u * LANES:(u + 1) * LANES] for u in range(k_slabs)]

    def project_out(inv_l):
        for hh in range(hps):
            rs = slice(hh * qb, (hh + 1) * qb)
            o = jnp.concatenate([acc_ref[rs, u * LANES:(u + 1) * LANES] * inv_l[rs] for u in range(c_slabs)],
                                axis=1)
            y = jnp.dot(o.astype(BF16), wuv_ref[hh], preferred_element_type=F32)
            o_ref[:, hh * v:(hh + 1) * v] = y.astype(o_ref.dtype)

    @pl.when(use_shift)
    def _static_shift():
        def kv_step(kt, carry):
            off = pl.multiple_of(kt * KEY_TILE, KEY_TILE)
            lat_t = lat_ref[0, pl.ds(off, KEY_TILE), :]
            for ci in range(hps // hpc):
                rs = slice(ci * rc, (ci + 1) * rc)
                ps = [jnp.exp(sl) for sl in masked_logits(ci, off)]
                psum = ps[0]
                for pu in ps[1:]:
                    psum = psum + pu
                l_ref[rs, :] = l_ref[rs, :] + psum
                pv = jnp.dot(jnp.concatenate(ps, axis=1).astype(BF16), lat_t, preferred_element_type=F32)
                for u in range(c_slabs):
                    cs = slice(u * LANES, (u + 1) * LANES)
                    acc_ref[rs, cs] = acc_ref[rs, cs] + pv[:, cs]
            return carry

        lax.fori_loop(0, n_t, kv_step, 0)
        project_out(jnp.broadcast_to(1.0 / jnp.sum(l_ref[...], axis=1, keepdims=True), l_ref.shape))

    @pl.when(jnp.logical_not(use_shift))
    def _online():
        m_ref[...] = jnp.full(m_ref.shape, -3e38, F32)

        def kv_step(kt, carry):
            off = pl.multiple_of(kt * KEY_TILE, KEY_TILE)
            lat_t = lat_ref[0, pl.ds(off, KEY_TILE), :]
            for ci in range(hps // hpc):
                rs = slice(ci * rc, (ci + 1) * rc)
                slabs = masked_logits(ci, off)
                mx = slabs[0]
                for sl in slabs[1:]:
                    mx = jnp.maximum(mx, sl)
                m_prev = m_ref[rs, :]
                m_new = jnp.maximum(m_prev, jnp.max(mx, axis=1, keepdims=True))
                alpha = jnp.exp(m_prev - m_new)
                ps = [jnp.exp(sl - m_new) for sl in slabs]
                psum = ps[0]
                for pu in ps[1:]:
                    psum = psum + pu
                l_ref[rs, :] = alpha * l_ref[rs, :] + jnp.sum(psum, axis=1, keepdims=True)
                pv = jnp.dot(jnp.concatenate(ps, axis=1).astype(BF16), lat_t, preferred_element_type=F32)
                for u in range(c_slabs):
                    cs = slice(u * LANES, (u + 1) * LANES)
                    acc_ref[rs, cs] = alpha * acc_ref[rs, cs] + pv[:, cs]
                m_ref[rs, :] = m_new
            return carry

        lax.fori_loop(0, n_t, kv_step, 0)
        project_out(1.0 / l_ref[...])


def _sparse_attention(logit_bound, q, qi, widx, ki, kk, lat, w_uv, *, batch, tp, k_sel, pad_front):
    nblk, n_heads, qb, c = q.shape
    usable = 2.0 * logit_bound <= MAX_EXP_SPAN
    shift_info = jnp.stack([jnp.where(usable, logit_bound, 0.0), usable.astype(F32)]).astype(F32)
    idx_heads, di = qi.shape[1], qi.shape[3]
    tk = ki.shape[1]
    v = w_uv.shape[2]
    nq = tp // qb
    hps = min(HEADS_PER_STEP, n_heads)
    rows = hps * qb
    return pl.pallas_call(
        functools.partial(_attn_kernel, k_sel=k_sel, pad_front=pad_front, idx_heads=idx_heads),
        grid=(batch, nq, n_heads // hps),
        in_specs=[pl.BlockSpec(memory_space=pltpu.SMEM),
                  pl.BlockSpec((1, hps, qb, c), lambda b, j, h: (b * nq + j, h, 0, 0)),
                  pl.BlockSpec((1, idx_heads, qb, di), lambda b, j, h: (b * nq + j, 0, 0, 0)),
                  pl.BlockSpec((qb, LANES), lambda b, j, h: (b * nq + j, 0)),
                  pl.BlockSpec((1, tk, di), lambda b, j, h: (b, 0, 0)),
                  pl.BlockSpec((1, tk, c), lambda b, j, h: (b, 0, 0)),
                  pl.BlockSpec((1, tk, c), lambda b, j, h: (b, 0, 0)),
                  pl.BlockSpec((hps, c, v), lambda b, j, h: (h, 0, 0))],
        out_specs=pl.BlockSpec((qb, hps * v), lambda b, j, h: (b * nq + j, h)),
        out_shape=jax.ShapeDtypeStruct((nblk * qb, n_heads * v), BF16),
        scratch_shapes=[pltpu.VMEM((qb, tk), F32), pltpu.VMEM((qb, tk), I32),
                        pltpu.VMEM((rows, c), F32), pltpu.VMEM((rows, LANES), F32),
                        pltpu.VMEM((rows, LANES), F32)],
        name="sparse_attention",
        compiler_params=_params("parallel", "arbitrary", "arbitrary"),
    )(shift_info, q, qi, widx, ki, kk, lat, w_uv)


def _merge_kernel(yp_ref, ya_ref, wp_ref, wa_ref, gp_ref, ga_ref, o_ref):
    pool = jnp.dot(yp_ref[...], wp_ref[...], preferred_element_type=F32)
    attn = jnp.dot(ya_ref[...], wa_ref[...], preferred_element_type=F32)
    o_ref[...] = (gp_ref[...].astype(F32) * pool + ga_ref[...].astype(F32) * attn).astype(o_ref.dtype)


def _merge(y_pool, y_attn, w_bp, w_ba, gates):
    n, pw = y_pool.shape
    aw = y_attn.shape[1]
    d = w_bp.shape[1]
    tm = _tile(n, 768, 128)
    tn = 512
    return pl.pallas_call(
        _merge_kernel,
        grid=(n // tm, d // tn),
        in_specs=[pl.BlockSpec((tm, pw), lambda i, j: (i, 0)),
                  pl.BlockSpec((tm, aw), lambda i, j: (i, 0)),
                  pl.BlockSpec((pw, tn), lambda i, j: (0, j)),
                  pl.BlockSpec((aw, tn), lambda i, j: (0, j)),
                  pl.BlockSpec((tm, tn), lambda i, j: (i, j)),
                  pl.BlockSpec((tm, tn), lambda i, j: (i, j + d // tn))],
        out_specs=pl.BlockSpec((tm, tn), lambda i, j: (i, j)),
        out_shape=jax.ShapeDtypeStruct((n, d), BF16),
        name="branch_merge",
        compiler_params=_params("parallel", "arbitrary"),
    )(y_pool, y_attn, w_bp, w_ba, gates, gates)


def _out_proj_kernel(x_ref, w_ref, r_ref, o_ref):
    o_ref[...] = r_ref[...] + jnp.dot(x_ref[...], w_ref[...], preferred_element_type=F32)


def _out_proj(merged, w_out, resid):
    n, k = merged.shape
    d = w_out.shape[1]
    tm = _tile(n, 768, 128)
    tn = _tile(d, 1024, LANES)
    return pl.pallas_call(
        _out_proj_kernel,
        grid=(n // tm, d // tn),
        in_specs=[pl.BlockSpec((tm, k), lambda i, j: (i, 0)),
                  pl.BlockSpec((k, tn), lambda i, j: (0, j)),
                  pl.BlockSpec((tm, tn), lambda i, j: (i, j))],
        out_specs=pl.BlockSpec((tm, tn), lambda i, j: (i, j)),
        out_shape=jax.ShapeDtypeStruct((n, d), F32),
        name="out_proj_residual",
        compiler_params=_params("parallel", "arbitrary"),
    )(merged, w_out, resid)


def _pack_halves(x):
    w = x.shape[1] // 2
    lo = pltpu.bitcast(x[:, :w].astype(BF16).astype(F32), I32)
    hi = pltpu.bitcast(x[:, w:].astype(BF16).astype(F32), I32)
    return ((lo >> 16) & 0xFFFF) | (hi & -65536)


def _unpack_halves(p):
    return pltpu.bitcast(p << 16, F32), pltpu.bitcast(p & -65536, F32)


def _router_kernel(h_ref, g_ref, rw_ref, rb_ref, xp_o, e_o, w_o, rank_o, cnt_o, carry_ref):
    i = pl.program_id(0)

    @pl.when(i == 0)
    def _():
        carry_ref[...] = jnp.zeros_like(carry_ref)

    x = h_ref[...]
    xn = x * lax.rsqrt(jnp.mean(x * x, axis=-1, keepdims=True) + EPS) * g_ref[...]
    xp_o[...] = _pack_halves(xn)
    logits = jnp.dot(xn.astype(BF16), rw_ref[...], preferred_element_type=F32) + rb_ref[...]
    tm = logits.shape[0]
    lane = lax.broadcasted_iota(I32, (tm, LANES), 1).astype(F32)
    vals, hots = [], []
    cur = logits
    for k in range(TOP_K_EXPERTS):
        m = jnp.max(cur, axis=1, keepdims=True)
        idx = jnp.min(jnp.where(cur == m, lane, float(LANES)), axis=1, keepdims=True)
        hot = lane == idx
        vals.append(m)
        hots.append(hot)
        e_o[:, k:k + 1] = idx.astype(I32)
        cur = jnp.where(hot, -jnp.inf, cur)
    exps = [jnp.exp(vk - vals[0]) for vk in vals]
    denom = exps[0]
    for ek in exps[1:]:
        denom = denom + ek
    for k in range(TOP_K_EXPERTS):
        w_o[:, k:k + 1] = exps[k] / denom
    onehot = jnp.zeros((tm, LANES), F32)
    for hot in hots:
        onehot = onehot + jnp.where(hot, 1.0, 0.0)
    lower = (lax.broadcasted_iota(I32, (tm, tm), 0) > lax.broadcasted_iota(I32, (tm, tm), 1)).astype(BF16)
    before = jnp.dot(lower, onehot.astype(BF16), preferred_element_type=F32) + carry_ref[...]
    for k in range(TOP_K_EXPERTS):
        rank_o[:, k:k + 1] = jnp.sum(jnp.where(hots[k], before, 0.0), axis=1, keepdims=True).astype(I32)
    carry_ref[...] = carry_ref[...] + jnp.sum(onehot, axis=0, keepdims=True)
    cnt_o[...] = carry_ref[...]


def _router(h2, norm2_g, rw, rb):
    n, d = h2.shape
    tm = ROUTE_TILE
    kk = TOP_K_EXPERTS
    return pl.pallas_call(
        _router_kernel,
        grid=(n // tm,),
        in_specs=[pl.BlockSpec((tm, d), lambda i: (i, 0)),
                  pl.BlockSpec((1, d), lambda i: (0, 0)),
                  pl.BlockSpec((d, LANES), lambda i: (0, 0)),
                  pl.BlockSpec((1, LANES), lambda i: (0, 0))],
        out_specs=[pl.BlockSpec((tm, d // 2), lambda i: (i, 0)),
                   pl.BlockSpec((tm, kk), lambda i: (i, 0)),
                   pl.BlockSpec((tm, kk), lambda i: (i, 0)),
                   pl.BlockSpec((tm, kk), lambda i: (i, 0)),
                   pl.BlockSpec((1, LANES), lambda i: (0, 0))],
        out_shape=[jax.ShapeDtypeStruct((n, d // 2), I32),
                   jax.ShapeDtypeStruct((n, kk), I32),
                   jax.ShapeDtypeStruct((n, kk), F32),
                   jax.ShapeDtypeStruct((n, kk), I32),
                   jax.ShapeDtypeStruct((1, LANES), F32)],
        scratch_shapes=[pltpu.VMEM((1, LANES), F32)],
        name="moe_router",
        compiler_params=_params("arbitrary"),
    )(h2, norm2_g.reshape(1, d), rw, rb)


def _row_copy(src_ref, src_row, dst_ref, dst_row, sem):
    return pltpu.make_async_copy(src_ref.at[pl.ds(src_row, 1), :], dst_ref.at[pl.ds(dst_row, 1), :], sem)


def _dispatch_kernel(dest_ref, xp_ref, init_ref, xs_ref, sem):
    del init_ref
    tm = xp_ref.shape[0]
    base = pl.program_id(0) * tm * TOP_K_EXPERTS

    def issue(r, carry):
        for k in range(TOP_K_EXPERTS):
            _row_copy(xp_ref, r, xs_ref, dest_ref[base + r * TOP_K_EXPERTS + k], sem).start()
        return carry

    def drain(r, carry):
        for k in range(TOP_K_EXPERTS):
            _row_copy(xp_ref, 0, xs_ref, 0, sem).wait()
        return carry

    lax.fori_loop(0, tm, issue, 0)
    lax.fori_loop(0, tm, drain, 0)


def _dispatch(dest_flat, xp, n_slots):
    n, w = xp.shape
    tm = ROUTE_TILE
    return pl.pallas_call(
        _dispatch_kernel,
        grid_spec=pltpu.PrefetchScalarGridSpec(
            num_scalar_prefetch=1,
            grid=(n // tm,),
            in_specs=[pl.BlockSpec((tm, w), lambda i, dest: (i, 0)),
                      pl.BlockSpec(memory_space=pl.ANY)],
            out_specs=pl.BlockSpec(memory_space=pl.ANY),
            scratch_shapes=[pltpu.SemaphoreType.DMA(())]),
        out_shape=jax.ShapeDtypeStruct((n_slots, w), I32),
        input_output_aliases={2: 0},
        name="moe_dispatch",
        compiler_params=_params("arbitrary"),
    )(dest_flat, xp, jnp.zeros((n_slots, w), I32))


def _deinterleave_kernel(w_ref, o_ref):
    blk = 2 * LANES
    dst = lax.broadcasted_iota(I32, (blk, blk), 1)
    src = jnp.where(dst < LANES, 2 * dst, 2 * (dst - LANES) + 1)
    perm = (lax.broadcasted_iota(I32, (blk, blk), 0) == src).astype(BF16)
    for b in range(w_ref.shape[2] // blk):
        cols = slice(b * blk, (b + 1) * blk)
        o_ref[0, :, cols] = jnp.dot(w_ref[0, :, cols].astype(BF16), perm,
                                    preferred_element_type=F32).astype(o_ref.dtype)


def _deinterleave_gate_up(w_gate_up):
    n_exp, d, de2 = w_gate_up.shape
    assert de2 % (2 * LANES) == 0
    tk = _tile(d, 512, 16)
    return pl.pallas_call(
        _deinterleave_kernel,
        grid=(n_exp, d // tk),
        in_specs=[pl.BlockSpec((1, tk, de2), lambda e, i: (e, i, 0))],
        out_specs=pl.BlockSpec((1, tk, de2), lambda e, i: (e, i, 0)),
        out_shape=jax.ShapeDtypeStruct((n_exp, d, de2), BF16),
        name="moe_weight_regroup",
        compiler_params=_params("parallel", "arbitrary"),
    )(w_gate_up)


def _expert_up_kernel(te_ref, tv_ref, xs_ref, w_ref, b_ref, o_ref):
    t = pl.program_id(0)

    @pl.when(tv_ref[t] > 0)
    def _():
        lo, hi = _unpack_halves(xs_ref[...])
        x = jnp.concatenate([lo.astype(BF16), hi.astype(BF16)], axis=1)
        gu = jnp.dot(x, w_ref[0], preferred_element_type=F32) + b_ref[0]
        for blk in range(gu.shape[1] // (2 * LANES)):
            g0 = blk * 2 * LANES
            gate = jnp.minimum(gu[:, g0:g0 + LANES], SWIGLU_LIMIT)
            up = jnp.clip(gu[:, g0 + LANES:g0 + 2 * LANES], -SWIGLU_LIMIT, SWIGLU_LIMIT)
            act = (up + 1.0) * gate * jax.nn.sigmoid(SWIGLU_ALPHA * gate)
            o_ref[:, blk * LANES:(blk + 1) * LANES] = act.astype(o_ref.dtype)

    @pl.when(tv_ref[t] == 0)
    def _():
        o_ref[...] = jnp.zeros_like(o_ref)


def _expert_up(tile_expert, tile_valid, xs, w_gu, b_gu):
    n_slots, w = xs.shape
    n_exp, d, de2 = w_gu.shape
    tm = EXPERT_TILE
    return pl.pallas_call(
        _expert_up_kernel,
        grid_spec=pltpu.PrefetchScalarGridSpec(
            num_scalar_prefetch=2,
            grid=(n_slots // tm,),
            in_specs=[pl.BlockSpec((tm, w), lambda t, te, tv: (t, 0)),
                      pl.BlockSpec((1, d, de2), lambda t, te, tv: (te[t], 0, 0)),
                      pl.BlockSpec((1, 1, de2), lambda t, te, tv: (te[t], 0, 0))],
            out_specs=pl.BlockSpec((tm, de2 // 2), lambda t, te, tv: (t, 0))),
        out_shape=jax.ShapeDtypeStruct((n_slots, de2 // 2), BF16),
        name="moe_expert_up",
        compiler_params=_params("arbitrary"),
    )(tile_expert, tile_valid, xs, w_gu, b_gu)


def _expert_down_kernel(te_ref, tv_ref, a_ref, w_ref, b_ref, o_ref):
    t = pl.program_id(0)

    @pl.when(tv_ref[t] > 0)
    def _():
        y = jnp.dot(a_ref[...], w_ref[0], preferred_element_type=F32) + b_ref[0]
        o_ref[...] = _pack_halves(y)

    @pl.when(tv_ref[t] == 0)
    def _():
        o_ref[...] = jnp.zeros_like(o_ref)


def _expert_down(tile_expert, tile_valid, act, w_down, b_down):
    n_slots, de = act.shape
    n_exp, _, d = w_down.shape
    tm = EXPERT_TILE
    return pl.pallas_call(
        _expert_down_kernel,
        grid_spec=pltpu.PrefetchScalarGridSpec(
            num_scalar_prefetch=2,
            grid=(n_slots // tm,),
            in_specs=[pl.BlockSpec((tm, de), lambda t, te, tv: (t, 0)),
                      pl.BlockSpec((1, de, d), lambda t, te, tv: (te[t], 0, 0)),
                      pl.BlockSpec((1, 1, d), lambda t, te, tv: (te[t], 0, 0))],
            out_specs=pl.BlockSpec((tm, d // 2), lambda t, te, tv: (t, 0))),
        out_shape=jax.ShapeDtypeStruct((n_slots, d // 2), I32),
        name="moe_expert_down",
        compiler_params=_params("arbitrary"),
    )(tile_expert, tile_valid, act, w_down, b_down)


def _combine_kernel(dest_ref, w_ref, h_ref, ye_ref, o_ref, buf_ref, sems, *, nq):
    b = pl.program_id(0)
    i = pl.program_id(1)
    n_i = pl.num_programs(1)
    n_steps = pl.num_programs(0) * n_i
    tm = h_ref.shape[0]
    half = buf_ref.shape[3]
    step = b * n_i + i
    slot = step % 2

    def start_rows(bb, ii, sl):
        base = ((bb * nq + 1 + ii) * tm) * TOP_K_EXPERTS

        def issue(r, carry):
            for k in range(TOP_K_EXPERTS):
                _row_copy(ye_ref, dest_ref[base + r * TOP_K_EXPERTS + k], buf_ref.at[sl, k], r,
                          sems.at[sl]).start()
            return carry

        lax.fori_loop(0, tm, issue, 0)

    @pl.when(step == 0)
    def _():
        start_rows(b, i, slot)

    @pl.when(step + 1 < n_steps)
    def _():
        wrap = i + 1 == n_i
        start_rows(jnp.where(wrap, b + 1, b), jnp.where(wrap, 0, i + 1), 1 - slot)

    def drain(r, carry):
        for k in range(TOP_K_EXPERTS):
            _row_copy(ye_ref, 0, buf_ref.at[slot, k], 0, sems.at[slot]).wait()
        return carry

    lax.fori_loop(0, tm, drain, 0)

    lo_sum = h_ref[:, :half]
    hi_sum = h_ref[:, half:]
    for k in range(TOP_K_EXPERTS):
        lo, hi = _unpack_halves(buf_ref[slot, k])
        wk = w_ref[:, k:k + 1]
        lo_sum = lo_sum + wk * lo
        hi_sum = hi_sum + wk * hi
    o_ref[0, :, :half] = lo_sum
    o_ref[0, :, half:] = hi_sum


def _combine(dest_flat, sel_w, h2, ye, *, batch, seq, tp):
    d = h2.shape[1]
    tm = Q_BLOCK
    nq = tp // tm
    return pl.pallas_call(
        functools.partial(_combine_kernel, nq=nq),
        grid_spec=pltpu.PrefetchScalarGridSpec(
            num_scalar_prefetch=1,
            grid=(batch, seq // tm),
            in_specs=[pl.BlockSpec((tm, TOP_K_EXPERTS), lambda b, i, dest: (b * nq + 1 + i, 0)),
                      pl.BlockSpec((tm, d), lambda b, i, dest: (b * nq + 1 + i, 0)),
                      pl.BlockSpec(memory_space=pl.ANY)],
            out_specs=pl.BlockSpec((1, tm, d), lambda b, i, dest: (b, i, 0)),
            scratch_shapes=[pltpu.VMEM((2, TOP_K_EXPERTS, tm, d // 2), I32), pltpu.SemaphoreType.DMA((2,))]),
        out_shape=jax.ShapeDtypeStruct((batch, seq, d), F32),
        name="moe_combine",
        compiler_params=_params("arbitrary", "arbitrary"),
    )(dest_flat, sel_w, h2, ye)


def _moe(h2, norm2_g, router_w, router_b, w_gate_up, b_gate_up, w_down, b_down, *, batch, seq, tp):
    n, d = h2.shape
    n_exp = router_w.shape[1]
    de = w_down.shape[1]
    assert n_exp <= LANES and n % ROUTE_TILE == 0
    rw = jnp.pad(router_w, ((0, 0), (0, LANES - n_exp))).astype(BF16)
    rb = jnp.pad(router_b, (0, LANES - n_exp), constant_values=NEG).reshape(1, LANES)
    xp, sel_e, sel_w, rank, counts = _router(h2, norm2_g, rw, rb)

    cnt = counts[0, :n_exp].astype(I32)
    padded = ((cnt + EXPERT_TILE - 1) // EXPERT_TILE) * EXPERT_TILE
    e_ids = jnp.arange(n_exp, dtype=I32)
    ends = jnp.sum(jnp.where(e_ids[:, None] <= e_ids[None, :], padded[:, None], 0), axis=0)
    starts = ends - padded
    total = ends[n_exp - 1]
    dest_flat = (starts[sel_e] + rank).reshape(-1)
    n_tiles = (n * TOP_K_EXPERTS) // EXPERT_TILE + n_exp
    tile_start = jnp.arange(n_tiles, dtype=I32) * EXPERT_TILE
    tile_valid = (tile_start < total).astype(I32)
    last_expert = jnp.sum((ends <= total - 1).astype(I32))
    tile_expert = jnp.minimum(jnp.sum((ends[None, :] <= tile_start[:, None]).astype(I32), axis=1), last_expert)

    w_gu = _deinterleave_gate_up(w_gate_up)
    nb = (2 * de) // (2 * LANES)
    b_gu = b_gate_up.reshape(n_exp, nb, LANES, 2).transpose(0, 1, 3, 2).reshape(n_exp, 1, 2 * de)

    xs = _dispatch(dest_flat, xp, n_tiles * EXPERT_TILE)
    act = _expert_up(tile_expert, tile_valid, xs, w_gu, b_gu)
    ye = _expert_down(tile_expert, tile_valid, act, w_down.astype(BF16), b_down.reshape(n_exp, 1, d))
    return _combine(dest_flat, sel_w, h2, ye, batch=batch, seq=seq, tp=tp)


def kernel(x, meta_tokens, norm1_g, w_in, q_norm_g, w_uq, w_iq, kv_norm_g, q_head_norm_g, k_head_norm_g,
           idx_k_norm_g, w_uv, w_branch_attn, w_pool, pool_scale, w_branch_pool, w_out, norm2_g, router_w,
           router_b, w_gate_up, b_gate_up, w_down, b_down):
    batch, seq, d = x.shape
    n_meta = meta_tokens.shape[0]
    depth = norm1_g.shape[0]
    qr, kvr, di = q_norm_g.shape[1], kv_norm_g.shape[1], idx_k_norm_g.shape[1]
    n_heads, v = w_uv.shape[1], w_uv.shape[3]
    idx_heads = w_iq.shape[2] // di
    pw = pool_scale.shape[1]
    assert depth == 1
    assert n_meta <= CHUNK and seq % Q_BLOCK == 0 and idx_heads <= LANES
    assert qr % kvr == 0 and (qr + kvr) % di == 0 and di % LANES == 0
    pad_front = Q_BLOCK - n_meta
    tp = Q_BLOCK + seq
    tk = -(-tp // KEY_TILE) * KEY_TILE
    k_sel = min(TOPK_MAX, seq // 4)
    attn_scale = kvr ** -0.5
    idx_scale = (di ** -0.5) * (idx_heads ** -0.5)

    meta = jnp.broadcast_to(meta_tokens[None].astype(x.dtype), (batch, n_meta, d))
    h = jnp.concatenate([jnp.zeros((batch, pad_front, d), x.dtype), meta, x], axis=1).reshape(batch * tp, d)

    out = None
    for l in range(depth):
        n_small = qr + kvr + di
        o_pool = n_small + idx_heads
        o_gate = o_pool + pw
        w_small = jnp.pad(w_in[l][:, :o_pool], ((0, 0), (0, LANES - idx_heads))).astype(BF16)
        w_a = w_in[l][:, o_pool:o_gate].astype(BF16)
        w_g = w_in[l][:, o_gate:].astype(BF16)

        xn = _rmsnorm_rows(h, norm1_g[l], BF16)
        small = _matmul(xn, w_small, F32, tn=_tile(w_small.shape[1], 512, LANES), name="in_proj_latents")
        a_pool = _matmul(xn, w_a, BF16, tn=_tile(pw, 1024, LANES), name="in_proj_pool")
        gates = _matmul(xn, w_g, BF16, tn=_tile(2 * d, 1024, LANES), sigmoid=True, name="in_proj_gates")

        y_pool = _pool_mixer(a_pool, w_pool[l].astype(BF16), pool_scale[l], batch=batch, tp=tp,
                             pad_front=pad_front)

        cq, widx = _prep_q(small, q_norm_g[l], qr, n_small, idx_scale)
        lat, kk, ki = _prep_kv(small, kv_norm_g[l], k_head_norm_g[l], idx_k_norm_g[l],
                               batch=batch, tp=tp, tk=tk, qr=qr, kvr=kvr, di=di)
        q = _qproj(cq, w_uq[l].astype(BF16), q_head_norm_g[l], n_heads=n_heads, hd=kvr,
                   heads_per_tile=min(2, n_heads),
                   scale=attn_scale, norm=True, name="q_proj_headnorm")
        qi = _qproj(cq, w_iq[l].astype(BF16), jnp.ones((di,), F32), n_heads=idx_heads, hd=di,
                    heads_per_tile=min(8, idx_heads), scale=1.0, norm=False, name="indexer_q_proj")
        logit_bound = (BOUND_MARGIN * kvr * attn_scale * jnp.max(jnp.abs(q_head_norm_g[l]))
                       * jnp.max(jnp.abs(k_head_norm_g[l])))
        y_attn = _sparse_attention(logit_bound, q, qi, widx, ki, kk, lat, w_uv[l].astype(BF16), batch=batch,
                                   tp=tp, k_sel=k_sel, pad_front=pad_front)

        merged = _merge(y_pool, y_attn, w_branch_pool[l].astype(BF16), w_branch_attn[l].astype(BF16), gates)
        h2 = _out_proj(merged, w_out[l].astype(BF16), h)

        out = _moe(h2, norm2_g[l], router_w[l], router_b[l], w_gate_up[l], b_gate_up[l], w_down[l],
                   b_down[l], batch=batch, seq=seq, tp=tp)
    return out
```

```python
import functools

import jax
import jax.numpy as jnp
from jax import lax
from jax.experimental import pallas as pl
from jax.experimental.pallas import tpu as pltpu

F32 = jnp.float32
BF16 = jnp.bfloat16
I32 = jnp.int32

CHUNK = 64
POOL_WINDOWS = (2, 4, 8, 16)
TOPK_MAX = 256
TOP_K_EXPERTS = 4
SWIGLU_LIMIT = 7.0
SWIGLU_ALPHA = 1.702
EPS = 1e-6
NEG = -1e30

Q_BLOCK = 128
KEY_TILE = 512
HEADS_PER_STEP = 16
INDEXER_HEADS_PER_DOT = 8
HEADS_PER_CHAIN = 4
HALO = 16
EXPERT_TILE = 256
ROUTE_TILE = 256
LANES = 128
INT_MIN = -2 ** 31
MAX_EXP_SPAN = 80.0
BOUND_MARGIN = 1.02
VMEM_LIMIT = 56 * 1024 * 1024


def _tile(n, target, mult):
    best = None
    for t in range(mult, min(n, target) + 1, mult):
        if n % t == 0:
            best = t
    assert best is not None, (n, target, mult)
    return best


def _params(*sem):
    return pltpu.CompilerParams(dimension_semantics=sem, vmem_limit_bytes=VMEM_LIMIT)


def _embed_norm_kernel(x_ref, head_ref, g_ref, h_ref, xn_ref):
    rows = jnp.where(pl.program_id(1) == 0, head_ref[...], x_ref[0])
    h_ref[...] = rows
    ms = jnp.mean(rows * rows, axis=-1, keepdims=True)
    xn_ref[...] = (rows * lax.rsqrt(ms + EPS) * g_ref[...]).astype(xn_ref.dtype)


def _embed_norm(x, head_rows, g):
    batch, seq, d = x.shape
    nq = seq // Q_BLOCK + 1
    n = batch * nq * Q_BLOCK
    return pl.pallas_call(
        _embed_norm_kernel,
        grid=(batch, nq),
        in_specs=[pl.BlockSpec((1, Q_BLOCK, d), lambda b, i: (b, jnp.maximum(i - 1, 0), 0)),
                  pl.BlockSpec((Q_BLOCK, d), lambda b, i: (0, 0)),
                  pl.BlockSpec((1, d), lambda b, i: (0, 0))],
        out_specs=[pl.BlockSpec((Q_BLOCK, d), lambda b, i: (b * nq + i, 0)),
                   pl.BlockSpec((Q_BLOCK, d), lambda b, i: (b * nq + i, 0))],
        out_shape=[jax.ShapeDtypeStruct((n, d), x.dtype), jax.ShapeDtypeStruct((n, d), BF16)],
        name="embed_rmsnorm",
        compiler_params=_params("parallel", "arbitrary"),
    )(x, head_rows, g.reshape(1, d))


def _matmul_kernel(x_ref, w_ref, o_ref, *, sigmoid):
    acc = jnp.dot(x_ref[...], w_ref[...], preferred_element_type=F32)
    if sigmoid:
        acc = jax.nn.sigmoid(acc)
    o_ref[...] = acc.astype(o_ref.dtype)


def _matmul(x, w, out_dtype, *, tn, name, sigmoid=False):
    m, k = x.shape
    n = w.shape[1]
    tm = _tile(m, 768, 128)
    return pl.pallas_call(
        functools.partial(_matmul_kernel, sigmoid=sigmoid),
        grid=(m // tm, n // tn),
        in_specs=[pl.BlockSpec((tm, k), lambda i, j: (i, 0)), pl.BlockSpec((k, tn), lambda i, j: (0, j))],
        out_specs=pl.BlockSpec((tm, tn), lambda i, j: (i, j)),
        out_shape=jax.ShapeDtypeStruct((m, n), out_dtype),
        name=name,
        compiler_params=_params("parallel", "arbitrary"),
    )(x, w)


def _prep_q_kernel(cq_ref, wi_ref, g_ref, cq_o, wi_o, *, idx_scale):
    x = cq_ref[...]
    ms = jnp.mean(x * x, axis=-1, keepdims=True)
    cq_o[...] = (x * lax.rsqrt(ms + EPS) * g_ref[...]).astype(cq_o.dtype)
    wi_o[...] = wi_ref[...] * idx_scale


def _prep_q(small, q_norm_g, qr, wi_col, idx_scale):
    n = small.shape[0]
    tm = _tile(n, 768, 128)
    return pl.pallas_call(
        functools.partial(_prep_q_kernel, idx_scale=idx_scale),
        grid=(n // tm,),
        in_specs=[pl.BlockSpec((tm, qr), lambda i: (i, 0)),
                  pl.BlockSpec((tm, LANES), lambda i: (i, wi_col // LANES)),
                  pl.BlockSpec((1, qr), lambda i: (0, 0))],
        out_specs=[pl.BlockSpec((tm, qr), lambda i: (i, 0)), pl.BlockSpec((tm, LANES), lambda i: (i, 0))],
        out_shape=[jax.ShapeDtypeStruct((n, qr), BF16), jax.ShapeDtypeStruct((n, LANES), F32)],
        name="prep_query_latent",
        compiler_params=_params("parallel"),
    )(small, small, q_norm_g.reshape(1, qr))


def _prep_kv_kernel(ckv_ref, kidx_ref, gkv_ref, gkh_ref, gki_ref, lat_o, kk_o, ki_o, *, n_valid):
    i = pl.program_id(1)

    @pl.when(i < n_valid)
    def _():
        c = ckv_ref[...]
        lat = c * lax.rsqrt(jnp.mean(c * c, axis=-1, keepdims=True) + EPS) * gkv_ref[...]
        kk = lat * lax.rsqrt(jnp.mean(lat * lat, axis=-1, keepdims=True) + EPS) * gkh_ref[...]
        k = kidx_ref[...]
        ki = k * lax.rsqrt(jnp.mean(k * k, axis=-1, keepdims=True) + EPS) * gki_ref[...]
        lat_o[0] = lat.astype(lat_o.dtype)
        kk_o[0] = kk.astype(kk_o.dtype)
        ki_o[0] = ki.astype(ki_o.dtype)

    @pl.when(i >= n_valid)
    def _():
        lat_o[...] = jnp.zeros_like(lat_o)
        kk_o[...] = jnp.zeros_like(kk_o)
        ki_o[...] = jnp.zeros_like(ki_o)


def _prep_kv(small, gkv, gkh, gki, *, batch, tp, tk, qr, kvr, di):
    nq = tp // Q_BLOCK
    nk = tk // Q_BLOCK

    def row(b, i):
        return b * nq + jnp.minimum(i, nq - 1)

    return pl.pallas_call(
        functools.partial(_prep_kv_kernel, n_valid=nq),
        grid=(batch, nk),
        in_specs=[pl.BlockSpec((Q_BLOCK, kvr), lambda b, i: (row(b, i), qr // kvr)),
                  pl.BlockSpec((Q_BLOCK, di), lambda b, i: (row(b, i), (qr + kvr) // di)),
                  pl.BlockSpec((1, kvr), lambda b, i: (0, 0)),
                  pl.BlockSpec((1, kvr), lambda b, i: (0, 0)),
                  pl.BlockSpec((1, di), lambda b, i: (0, 0))],
        out_specs=[pl.BlockSpec((1, Q_BLOCK, kvr), lambda b, i: (b, i, 0)),
                   pl.BlockSpec((1, Q_BLOCK, kvr), lambda b, i: (b, i, 0)),
                   pl.BlockSpec((1, Q_BLOCK, di), lambda b, i: (b, i, 0))],
        out_shape=[jax.ShapeDtypeStruct((batch, tk, kvr), BF16),
                   jax.ShapeDtypeStruct((batch, tk, kvr), BF16),
                   jax.ShapeDtypeStruct((batch, tk, di), BF16)],
        name="prep_keys",
        compiler_params=_params("parallel", "arbitrary"),
    )(small, small, gkv.reshape(1, kvr), gkh.reshape(1, kvr), gki.reshape(1, di))


def _pool_kernel(a_ref, halo_ref, w_ref, sc_ref, o_ref, xs_ref, *, tp_tile, pad_front, pg):
    i = pl.program_id(1)
    t = i * tp_tile + lax.broadcasted_iota(I32, (tp_tile, 1), 0) - pad_front
    for g, win in enumerate(POOL_WINDOWS):
        cols = slice(g * pg, (g + 1) * pg)
        xs_ref[0:HALO, :] = halo_ref[0, :, cols].astype(F32)
        xs_ref[HALO:, :] = a_ref[0, :, cols].astype(F32)
        cur = xs_ref[pl.ds(HALO, tp_tile), :]
        acc = cur
        for k in range(1, win):
            acc = acc + xs_ref[pl.ds(HALO - k, tp_tile), :]
        cnt = jnp.clip(t + 1, 1, win).astype(F32)
        pooled = (acc / cnt - cur).astype(BF16)
        y = jnp.dot(pooled, w_ref[g], preferred_element_type=F32) * sc_ref[:, cols]
        o_ref[0, :, cols] = y.astype(o_ref.dtype)


def _pool_mixer(a, w_pool, pool_scale, *, batch, tp, pad_front):
    pw = a.shape[-1]
    n_groups, pg, _ = w_pool.shape
    tpt = _tile(tp, 1536, HALO)
    a3 = a.reshape(batch, tp, pw)
    halo_blocks = tpt // HALO
    out = pl.pallas_call(
        functools.partial(_pool_kernel, tp_tile=tpt, pad_front=pad_front, pg=pg),
        grid=(batch, tp // tpt),
        in_specs=[pl.BlockSpec((1, tpt, pw), lambda b, i: (b, i, 0)),
                  pl.BlockSpec((1, HALO, pw), lambda b, i: (b, jnp.maximum(i * halo_blocks - 1, 0), 0)),
                  pl.BlockSpec((n_groups, pg, pg), lambda b, i: (0, 0, 0)),
                  pl.BlockSpec((1, pw), lambda b, i: (0, 0))],
        out_specs=pl.BlockSpec((1, tpt, pw), lambda b, i: (b, i, 0)),
        out_shape=jax.ShapeDtypeStruct((batch, tp, pw), BF16),
        scratch_shapes=[pltpu.VMEM((tpt + HALO, pg), F32)],
        name="pool_mixer",
        compiler_params=_params("parallel", "arbitrary"),
    )(a3, a3, w_pool, pool_scale.reshape(1, pw))
    return out.reshape(batch * tp, pw)


def _qproj_kernel(x_ref, w_ref, g_ref, o_ref, *, scale, norm, heads, hd):
    acc = jnp.dot(x_ref[...], w_ref[...], preferred_element_type=F32)
    nblk = o_ref.shape[0]
    for hh in range(heads):
        a = acc[:, hh * hd:(hh + 1) * hd]
        if norm:
            a = a * lax.rsqrt(jnp.mean(a * a, axis=-1, keepdims=True) + EPS) * (g_ref[...] * scale)
        a = a.astype(o_ref.dtype)
        for r in range(nblk):
            o_ref[r, hh] = a[r * Q_BLOCK:(r + 1) * Q_BLOCK]


def _qproj(cq, w, g, *, n_heads, hd, heads_per_tile, scale, norm, name):
    n, r = cq.shape
    tm = _tile(n, 1536, Q_BLOCK)
    nblk = tm // Q_BLOCK
    tn = heads_per_tile * hd
    return pl.pallas_call(
        functools.partial(_qproj_kernel, scale=scale, norm=norm, heads=heads_per_tile, hd=hd),
        grid=(n // tm, n_heads // heads_per_tile),
        in_specs=[pl.BlockSpec((tm, r), lambda i, j: (i, 0)),
                  pl.BlockSpec((r, tn), lambda i, j: (0, j)),
                  pl.BlockSpec((1, hd), lambda i, j: (0, 0))],
        out_specs=pl.BlockSpec((nblk, heads_per_tile, Q_BLOCK, hd), lambda i, j: (i, j, 0, 0)),
        out_shape=jax.ShapeDtypeStruct((n // Q_BLOCK, n_heads, Q_BLOCK, hd), BF16),
        name=name,
        compiler_params=_params("parallel", "arbitrary"),
    )(cq, w, g.reshape(1, hd))


def _attn_kernel(shift_ref, q_ref, qi_ref, wi_ref, ki_ref, kk_ref, lat_ref, wuv_ref, o_ref,
                 bias_ref, keyp_ref, acc_ref, m_ref, l_ref, *, k_sel, pad_front, idx_heads):
    j = pl.program_id(1)
    hg = pl.program_id(2)
    shift = shift_ref[0]
    use_shift = shift_ref[1] > 0.5
    hps, qb, c = q_ref.shape[1], q_ref.shape[2], q_ref.shape[3]
    di = qi_ref.shape[3]
    v = wuv_ref.shape[2]
    n_t = (j * qb + qb + KEY_TILE - 1) // KEY_TILE
    nt_dims = (((1,), (1,)), ((), ()))

    @pl.when(hg == 0)
    def _select():
        tq = j * qb + lax.broadcasted_iota(I32, (qb, 1), 0)
        limit = ((jnp.maximum(tq, CHUNK) + CHUNK) // CHUNK) * CHUNK

        def score_tile(kt, carry):
            off = pl.multiple_of(kt * KEY_TILE, KEY_TILE)
            ki_t = ki_ref[0, pl.ds(off, KEY_TILE), :]
            part = jnp.zeros((qb, KEY_TILE), F32)
            hpi = min(INDEXER_HEADS_PER_DOT, idx_heads)
            for g in range(idx_heads // hpi):
                qg = qi_ref[0, g * hpi:(g + 1) * hpi].reshape(hpi * qb, di)
                s = lax.dot_general(qg, ki_t, nt_dims, preferred_element_type=F32)
                s = jnp.maximum(s, 0.0)
                for hh in range(hpi):
                    h = g * hpi + hh
                    part = part + s[hh * qb:(hh + 1) * qb] * wi_ref[:, h:h + 1]
            bits = pltpu.bitcast(part, I32)
            key = bits ^ ((bits >> 31) & 0x7FFFFFFF)
            s_idx = off + lax.broadcasted_iota(I32, (qb, KEY_TILE), 1)
            adm = (s_idx >= pad_front) & (s_idx < limit)
            keyp_ref[:, pl.ds(off, KEY_TILE)] = jnp.where(adm, key, INT_MIN)
            return carry

        lax.fori_loop(0, n_t, score_tile, 0)

        def count_ge(thr):
            def body(kt, cnt):
                off = pl.multiple_of(kt * KEY_TILE, KEY_TILE)
                ge = jnp.where(keyp_ref[:, pl.ds(off, KEY_TILE)] >= thr, 1.0, 0.0)
                for u in range(KEY_TILE // LANES):
                    cnt = cnt + ge[:, u * LANES:(u + 1) * LANES]
                return cnt
            cnt = lax.fori_loop(0, n_t, body, jnp.zeros((qb, LANES), F32))
            return jnp.sum(cnt, axis=1, keepdims=True)

        def bit_step(bi, thr):
            cand = thr + lax.shift_left(jnp.int32(1), 31 - bi)
            return jnp.where(count_ge(cand) >= k_sel, cand, thr)

        thr = lax.fori_loop(0, 32, bit_step, jnp.full((qb, 1), INT_MIN, I32))
        thr = jnp.maximum(thr, INT_MIN + 1)

        def bias_tile(kt, carry):
            off = pl.multiple_of(kt * KEY_TILE, KEY_TILE)
            sel = keyp_ref[:, pl.ds(off, KEY_TILE)] >= thr
            bias_ref[:, pl.ds(off, KEY_TILE)] = jnp.where(sel, -shift, NEG)
            return carry

        lax.fori_loop(0, n_t, bias_tile, 0)

        @pl.when(jnp.max(count_ge(thr)) > k_sel)
        def _ties():
            quota = k_sel - count_ge(thr + 1)
            upper = (lax.broadcasted_iota(I32, (LANES, LANES), 0)
                     < lax.broadcasted_iota(I32, (LANES, LANES), 1)).astype(BF16)

            def chunk(ci, seen):
                off = pl.multiple_of(ci * LANES, LANES)
                kp = keyp_ref[:, pl.ds(off, LANES)]
                tie = jnp.where(kp == thr, 1.0, 0.0)
                rank = jnp.dot(tie.astype(BF16), upper, preferred_element_type=F32) + seen
                sel = jnp.where(kp > thr, 1.0, tie * jnp.where(rank < quota, 1.0, 0.0))
                bias_ref[:, pl.ds(off, LANES)] = jnp.where(sel > 0.5, -shift, NEG)
                return seen + jnp.sum(tie, axis=1, keepdims=True)

            lax.fori_loop(0, n_t * (KEY_TILE // LANES), chunk, jnp.zeros((qb, 1), F32))

    l_ref[...] = jnp.zeros(l_ref.shape, F32)
    acc_ref[...] = jnp.zeros(acc_ref.shape, F32)
    hpc = min(HEADS_PER_CHAIN, hps)
    rc = hpc * qb
    k_slabs = KEY_TILE // LANES
    c_slabs = c // LANES

    def masked_logits(ci, off):
        q_c = q_ref[0, ci * hpc:(ci + 1) * hpc].reshape(rc, c)
        s = lax.dot_general(q_c, kk_ref[0, pl.ds(off, KEY_TILE), :], nt_dims, preferred_element_type=F32)
        s = (s.reshape(hpc, qb, KEY_TILE) + bias_ref[:, pl.ds(off, KEY_TILE)][None]).reshape(rc, KEY_TILE)
        return [s[:, u * LANES:(u + 1) * LANES] for u in range(k_slabs)]

    def project_out(inv_l):
        for hh in range(hps):
            rs = slice(hh * qb, (hh + 1) * qb)
            o = jnp.concatenate([acc_ref[rs, u * LANES:(u + 1) * LANES] * inv_l[rs] for u in range(c_slabs)],
                                axis=1)
            y = jnp.dot(o.astype(BF16), wuv_ref[hh], preferred_element_type=F32)
            o_ref[:, hh * v:(hh + 1) * v] = y.astype(o_ref.dtype)

    @pl.when(use_shift)
    def _static_shift():
        def kv_step(kt, carry):
            off = pl.multiple_of(kt * KEY_TILE, KEY_TILE)
            lat_t = lat_ref[0, pl.ds(off, KEY_TILE), :]
            for ci in range(hps // hpc):
                rs = slice(ci * rc, (ci + 1) * rc)
                ps = [jnp.exp(sl) for sl in masked_logits(ci, off)]
                psum = ps[0]
                for pu in ps[1:]:
                    psum = psum + pu
                l_ref[rs, :] = l_ref[rs, :] + psum
                pv = jnp.dot(jnp.concatenate(ps, axis=1).astype(BF16), lat_t, preferred_element_type=F32)
                for u in range(c_slabs):
                    cs = slice(u * LANES, (u + 1) * LANES)
                    acc_ref[rs, cs] = acc_ref[rs, cs] + pv[:, cs]
            return carry

        lax.fori_loop(0, n_t, kv_step, 0)
        project_out(jnp.broadcast_to(1.0 / jnp.sum(l_ref[...], axis=1, keepdims=True), l_ref.shape))

    @pl.when(jnp.logical_not(use_shift))
    def _online():
        m_ref[...] = jnp.full(m_ref.shape, -3e38, F32)

        def kv_step(kt, carry):
            off = pl.multiple_of(kt * KEY_TILE, KEY_TILE)
            lat_t = lat_ref[0, pl.ds(off, KEY_TILE), :]
            for ci in range(hps // hpc):
                rs = slice(ci * rc, (ci + 1) * rc)
                slabs = masked_logits(ci, off)
                mx = slabs[0]
                for sl in slabs[1:]:
                    mx = jnp.maximum(mx, sl)
                m_prev = m_ref[rs, :]
                m_new = jnp.maximum(m_prev, jnp.max(mx, axis=1, keepdims=True))
                alpha = jnp.exp(m_prev - m_new)
                ps = [jnp.exp(sl - m_new) for sl in slabs]
                psum = ps[0]
                for pu in ps[1:]:
                    psum = psum + pu
                l_ref[rs, :] = alpha * l_ref[rs, :] + jnp.sum(psum, axis=1, keepdims=True)
                pv = jnp.dot(jnp.concatenate(ps, axis=1).astype(BF16), lat_t, preferred_element_type=F32)
                for u in range(c_slabs):
                    cs = slice(u * LANES, (u + 1) * LANES)
                    acc_ref[rs, cs] = alpha * acc_ref[rs, cs] + pv[:, cs]
                m_ref[rs, :] = m_new
            return carry

        lax.fori_loop(0, n_t, kv_step, 0)
        project_out(1.0 / l_ref[...])


def _sparse_attention(logit_bound, q, qi, widx, ki, kk, lat, w_uv, *, batch, tp, k_sel, pad_front):
    nblk, n_heads, qb, c = q.shape
    usable = 2.0 * logit_bound <= MAX_EXP_SPAN
    shift_info = jnp.stack([jnp.where(usable, logit_bound, 0.0), usable.astype(F32)]).astype(F32)
    idx_heads, di = qi.shape[1], qi.shape[3]
    tk = ki.shape[1]
    v = w_uv.shape[2]
    nq = tp // qb
    hps = min(HEADS_PER_STEP, n_heads)
    rows = hps * qb
    return pl.pallas_call(
        functools.partial(_attn_kernel, k_sel=k_sel, pad_front=pad_front, idx_heads=idx_heads),
        grid=(batch, nq, n_heads // hps),
        in_specs=[pl.BlockSpec(memory_space=pltpu.SMEM),
                  pl.BlockSpec((1, hps, qb, c), lambda b, j, h: (b * nq + j, h, 0, 0)),
                  pl.BlockSpec((1, idx_heads, qb, di), lambda b, j, h: (b * nq + j, 0, 0, 0)),
                  pl.BlockSpec((qb, LANES), lambda b, j, h: (b * nq + j, 0)),
                  pl.BlockSpec((1, tk, di), lambda b, j, h: (b, 0, 0)),
                  pl.BlockSpec((1, tk, c), lambda b, j, h: (b, 0, 0)),
                  pl.BlockSpec((1, tk, c), lambda b, j, h: (b, 0, 0)),
                  pl.BlockSpec((hps, c, v), lambda b, j, h: (h, 0, 0))],
        out_specs=pl.BlockSpec((qb, hps * v), lambda b, j, h: (b * nq + j, h)),
        out_shape=jax.ShapeDtypeStruct((nblk * qb, n_heads * v), BF16),
        scratch_shapes=[pltpu.VMEM((qb, tk), F32), pltpu.VMEM((qb, tk), I32),
                        pltpu.VMEM((rows, c), F32), pltpu.VMEM((rows, LANES), F32),
                        pltpu.VMEM((rows, LANES), F32)],
        name="sparse_attention",
        compiler_params=_params("parallel", "arbitrary", "arbitrary"),
    )(shift_info, q, qi, widx, ki, kk, lat, w_uv)


def _merge_kernel(yp_ref, ya_ref, wp_ref, wa_ref, gp_ref, ga_ref, o_ref):
    pool = jnp.dot(yp_ref[...], wp_ref[...], preferred_element_type=F32)
    attn = jnp.dot(ya_ref[...], wa_ref[...], preferred_element_type=F32)
    o_ref[...] = (gp_ref[...].astype(F32) * pool + ga_ref[...].astype(F32) * attn).astype(o_ref.dtype)


def _merge(y_pool, y_attn, w_bp, w_ba, gates):
    n, pw = y_pool.shape
    aw = y_attn.shape[1]
    d = w_bp.shape[1]
    tm = _tile(n, 768, 128)
    tn = 512
    return pl.pallas_call(
        _merge_kernel,
        grid=(n // tm, d // tn),
        in_specs=[pl.BlockSpec((tm, pw), lambda i, j: (i, 0)),
                  pl.BlockSpec((tm, aw), lambda i, j: (i, 0)),
                  pl.BlockSpec((pw, tn), lambda i, j: (0, j)),
                  pl.BlockSpec((aw, tn), lambda i, j: (0, j)),
                  pl.BlockSpec((tm, tn), lambda i, j: (i, j)),
                  pl.BlockSpec((tm, tn), lambda i, j: (i, j + d // tn))],
        out_specs=pl.BlockSpec((tm, tn), lambda i, j: (i, j)),
        out_shape=jax.ShapeDtypeStruct((n, d), BF16),
        name="branch_merge",
        compiler_params=_params("parallel", "arbitrary"),
    )(y_pool, y_attn, w_bp, w_ba, gates, gates)


def _out_proj_kernel(x_ref, w_ref, r_ref, o_ref):
    o_ref[...] = r_ref[...] + jnp.dot(x_ref[...], w_ref[...], preferred_element_type=F32)


def _out_proj(merged, w_out, resid):
    n, k = merged.shape
    d = w_out.shape[1]
    tm = _tile(n, 768, 128)
    tn = _tile(d, 1024, LANES)
    return pl.pallas_call(
        _out_proj_kernel,
        grid=(n // tm, d // tn),
        in_specs=[pl.BlockSpec((tm, k), lambda i, j: (i, 0)),
                  pl.BlockSpec((k, tn), lambda i, j: (0, j)),
                  pl.BlockSpec((tm, tn), lambda i, j: (i, j))],
        out_specs=pl.BlockSpec((tm, tn), lambda i, j: (i, j)),
        out_shape=jax.ShapeDtypeStruct((n, d), F32),
        name="out_proj_residual",
        compiler_params=_params("parallel", "arbitrary"),
    )(merged, w_out, resid)


def _pack_halves(x):
    w = x.shape[1] // 2
    lo = pltpu.bitcast(x[:, :w].astype(BF16).astype(F32), I32)
    hi = pltpu.bitcast(x[:, w:].astype(BF16).astype(F32), I32)
    return ((lo >> 16) & 0xFFFF) | (hi & -65536)


def _unpack_halves(p):
    return pltpu.bitcast(p << 16, F32), pltpu.bitcast(p & -65536, F32)


def _router_kernel(h_ref, g_ref, rw_ref, rb_ref, xp_o, e_o, w_o, rank_o, cnt_o, carry_ref):
    i = pl.program_id(0)

    @pl.when(i == 0)
    def _():
        carry_ref[...] = jnp.zeros_like(carry_ref)

    x = h_ref[...]
    xn = x * lax.rsqrt(jnp.mean(x * x, axis=-1, keepdims=True) + EPS) * g_ref[...]
    xp_o[...] = _pack_halves(xn)
    logits = jnp.dot(xn.astype(BF16), rw_ref[...], preferred_element_type=F32) + rb_ref[...]
    tm = logits.shape[0]
    lane = lax.broadcasted_iota(I32, (tm, LANES), 1).astype(F32)
    vals, hots = [], []
    cur = logits
    for k in range(TOP_K_EXPERTS):
        m = jnp.max(cur, axis=1, keepdims=True)
        idx = jnp.min(jnp.where(cur == m, lane, float(LANES)), axis=1, keepdims=True)
        hot = lane == idx
        vals.append(m)
        hots.append(hot)
        e_o[:, k:k + 1] = idx.astype(I32)
        cur = jnp.where(hot, -jnp.inf, cur)
    exps = [jnp.exp(vk - vals[0]) for vk in vals]
    denom = exps[0]
    for ek in exps[1:]:
        denom = denom + ek
    for k in range(TOP_K_EXPERTS):
        w_o[:, k:k + 1] = exps[k] / denom
    onehot = jnp.zeros((tm, LANES), F32)
    for hot in hots:
        onehot = onehot + jnp.where(hot, 1.0, 0.0)
    lower = (lax.broadcasted_iota(I32, (tm, tm), 0) > lax.broadcasted_iota(I32, (tm, tm), 1)).astype(BF16)
    before = jnp.dot(lower, onehot.astype(BF16), preferred_element_type=F32) + carry_ref[...]
    for k in range(TOP_K_EXPERTS):
        rank_o[:, k:k + 1] = jnp.sum(jnp.where(hots[k], before, 0.0), axis=1, keepdims=True).astype(I32)
    carry_ref[...] = carry_ref[...] + jnp.sum(onehot, axis=0, keepdims=True)
    cnt_o[...] = carry_ref[...]


def _router(h2, norm2_g, rw, rb):
    n, d = h2.shape
    tm = ROUTE_TILE
    kk = TOP_K_EXPERTS
    return pl.pallas_call(
        _router_kernel,
        grid=(n // tm,),
        in_specs=[pl.BlockSpec((tm, d), lambda i: (i, 0)),
                  pl.BlockSpec((1, d), lambda i: (0, 0)),
                  pl.BlockSpec((d, LANES), lambda i: (0, 0)),
                  pl.BlockSpec((1, LANES), lambda i: (0, 0))],
        out_specs=[pl.BlockSpec((tm, d // 2), lambda i: (i, 0)),
                   pl.BlockSpec((tm, kk), lambda i: (i, 0)),
                   pl.BlockSpec((tm, kk), lambda i: (i, 0)),
                   pl.BlockSpec((tm, kk), lambda i: (i, 0)),
                   pl.BlockSpec((1, LANES), lambda i: (0, 0))],
        out_shape=[jax.ShapeDtypeStruct((n, d // 2), I32),
                   jax.ShapeDtypeStruct((n, kk), I32),
                   jax.ShapeDtypeStruct((n, kk), F32),
                   jax.ShapeDtypeStruct((n, kk), I32),
                   jax.ShapeDtypeStruct((1, LANES), F32)],
        scratch_shapes=[pltpu.VMEM((1, LANES), F32)],
        name="moe_router",
        compiler_params=_params("arbitrary"),
    )(h2, norm2_g.reshape(1, d), rw, rb)


def _row_copy(src_ref, src_row, dst_ref, dst_row, sem):
    return pltpu.make_async_copy(src_ref.at[pl.ds(src_row, 1), :], dst_ref.at[pl.ds(dst_row, 1), :], sem)


def _dispatch_kernel(dest_ref, xp_ref, init_ref, xs_ref, sem):
    del init_ref
    tm = xp_ref.shape[0]
    base = pl.program_id(0) * tm * TOP_K_EXPERTS

    def issue(r, carry):
        for k in range(TOP_K_EXPERTS):
            _row_copy(xp_ref, r, xs_ref, dest_ref[base + r * TOP_K_EXPERTS + k], sem).start()
        return carry

    def drain(r, carry):
        for k in range(TOP_K_EXPERTS):
            _row_copy(xp_ref, 0, xs_ref, 0, sem).wait()
        return carry

    lax.fori_loop(0, tm, issue, 0)
    lax.fori_loop(0, tm, drain, 0)


def _dispatch(dest_flat, xp, n_slots):
    n, w = xp.shape
    tm = ROUTE_TILE
    return pl.pallas_call(
        _dispatch_kernel,
        grid_spec=pltpu.PrefetchScalarGridSpec(
            num_scalar_prefetch=1,
            grid=(n // tm,),
            in_specs=[pl.BlockSpec((tm, w), lambda i, dest: (i, 0)),
                      pl.BlockSpec(memory_space=pl.ANY)],
            out_specs=pl.BlockSpec(memory_space=pl.ANY),
            scratch_shapes=[pltpu.SemaphoreType.DMA(())]),
        out_shape=jax.ShapeDtypeStruct((n_slots, w), I32),
        input_output_aliases={2: 0},
        name="moe_dispatch",
        compiler_params=_params("arbitrary"),
    )(dest_flat, xp, jnp.zeros((n_slots, w), I32))


def _deinterleave_kernel(w_ref, o_ref):
    blk = 2 * LANES
    dst = lax.broadcasted_iota(I32, (blk, blk), 1)
    src = jnp.where(dst < LANES, 2 * dst, 2 * (dst - LANES) + 1)
    perm = (lax.broadcasted_iota(I32, (blk, blk), 0) == src).astype(BF16)
    for b in range(w_ref.shape[2] // blk):
        cols = slice(b * blk, (b + 1) * blk)
        o_ref[0, :, cols] = jnp.dot(w_ref[0, :, cols].astype(BF16), perm,
                                    preferred_element_type=F32).astype(o_ref.dtype)


def _deinterleave_gate_up(w_gate_up):
    n_exp, d, de2 = w_gate_up.shape
    assert de2 % (2 * LANES) == 0
    tk = _tile(d, 512, 16)
    return pl.pallas_call(
        _deinterleave_kernel,
        grid=(n_exp, d // tk),
        in_specs=[pl.BlockSpec((1, tk, de2), lambda e, i: (e, i, 0))],
        out_specs=pl.BlockSpec((1, tk, de2), lambda e, i: (e, i, 0)),
        out_shape=jax.ShapeDtypeStruct((n_exp, d, de2), BF16),
        name="moe_weight_regroup",
        compiler_params=_params("parallel", "arbitrary"),
    )(w_gate_up)


def _expert_up_kernel(te_ref, tv_ref, xs_ref, w_ref, b_ref, o_ref):
    t = pl.program_id(0)

    @pl.when(tv_ref[t] > 0)
    def _():
        lo, hi = _unpack_halves(xs_ref[...])
        x = jnp.concatenate([lo.astype(BF16), hi.astype(BF16)], axis=1)
        gu = jnp.dot(x, w_ref[0], preferred_element_type=F32) + b_ref[0]
        for blk in range(gu.shape[1] // (2 * LANES)):
            g0 = blk * 2 * LANES
            gate = jnp.minimum(gu[:, g0:g0 + LANES], SWIGLU_LIMIT)
            up = jnp.clip(gu[:, g0 + LANES:g0 + 2 * LANES], -SWIGLU_LIMIT, SWIGLU_LIMIT)
            act = (up + 1.0) * gate * jax.nn.sigmoid(SWIGLU_ALPHA * gate)
            o_ref[:, blk * LANES:(blk + 1) * LANES] = act.astype(o_ref.dtype)

    @pl.when(tv_ref[t] == 0)
    def _():
        o_ref[...] = jnp.zeros_like(o_ref)


def _expert_up(tile_expert, tile_valid, xs, w_gu, b_gu):
    n_slots, w = xs.shape
    n_exp, d, de2 = w_gu.shape
    tm = EXPERT_TILE
    return pl.pallas_call(
        _expert_up_kernel,
        grid_spec=pltpu.PrefetchScalarGridSpec(
            num_scalar_prefetch=2,
            grid=(n_slots // tm,),
            in_specs=[pl.BlockSpec((tm, w), lambda t, te, tv: (t, 0)),
                      pl.BlockSpec((1, d, de2), lambda t, te, tv: (te[t], 0, 0)),
                      pl.BlockSpec((1, 1, de2), lambda t, te, tv: (te[t], 0, 0))],
            out_specs=pl.BlockSpec((tm, de2 // 2), lambda t, te, tv: (t, 0))),
        out_shape=jax.ShapeDtypeStruct((n_slots, de2 // 2), BF16),
        name="moe_expert_up",
        compiler_params=_params("arbitrary"),
    )(tile_expert, tile_valid, xs, w_gu, b_gu)


def _expert_down_kernel(te_ref, tv_ref, a_ref, w_ref, b_ref, o_ref, wbf_ref):
    t = pl.program_id(0)

    @pl.when(jnp.logical_or(t == 0, te_ref[t] != te_ref[jnp.maximum(t - 1, 0)]))
    def _():
        wbf_ref[...] = w_ref[0].astype(BF16)

    @pl.when(tv_ref[t] > 0)
    def _():
        y = jnp.dot(a_ref[...], wbf_ref[...], preferred_element_type=F32) + b_ref[0]
        o_ref[...] = _pack_halves(y)

    @pl.when(tv_ref[t] == 0)
    def _():
        o_ref[...] = jnp.zeros_like(o_ref)


def _expert_down(tile_expert, tile_valid, act, w_down, b_down):
    n_slots, de = act.shape
    n_exp, _, d = w_down.shape
    tm = EXPERT_TILE
    return pl.pallas_call(
        _expert_down_kernel,
        grid_spec=pltpu.PrefetchScalarGridSpec(
            num_scalar_prefetch=2,
            grid=(n_slots // tm,),
            in_specs=[pl.BlockSpec((tm, de), lambda t, te, tv: (t, 0)),
                      pl.BlockSpec((1, de, d), lambda t, te, tv: (te[t], 0, 0)),
                      pl.BlockSpec((1, 1, d), lambda t, te, tv: (te[t], 0, 0))],
            out_specs=pl.BlockSpec((tm, d // 2), lambda t, te, tv: (t, 0)),
            scratch_shapes=[pltpu.VMEM((de, d), BF16)]),
        out_shape=jax.ShapeDtypeStruct((n_slots, d // 2), I32),
        name="moe_expert_down",
        compiler_params=_params("arbitrary"),
    )(tile_expert, tile_valid, act, w_down, b_down)


def _combine_kernel(dest_ref, w_ref, h_ref, ye_ref, o_ref, buf_ref, sems, *, nq):
    b = pl.program_id(0)
    i = pl.program_id(1)
    n_i = pl.num_programs(1)
    n_steps = pl.num_programs(0) * n_i
    tm = h_ref.shape[0]
    half = buf_ref.shape[3]
    step = b * n_i + i
    slot = step % 2

    def start_rows(bb, ii, sl):
        base = ((bb * nq + 1 + ii) * tm) * TOP_K_EXPERTS

        def issue(r, carry):
            for k in range(TOP_K_EXPERTS):
                _row_copy(ye_ref, dest_ref[base + r * TOP_K_EXPERTS + k], buf_ref.at[sl, k], r,
                          sems.at[sl]).start()
            return carry

        lax.fori_loop(0, tm, issue, 0)

    @pl.when(step == 0)
    def _():
        start_rows(b, i, slot)

    @pl.when(step + 1 < n_steps)
    def _():
        wrap = i + 1 == n_i
        start_rows(jnp.where(wrap, b + 1, b), jnp.where(wrap, 0, i + 1), 1 - slot)

    def drain(r, carry):
        for k in range(TOP_K_EXPERTS):
            _row_copy(ye_ref, 0, buf_ref.at[slot, k], 0, sems.at[slot]).wait()
        return carry

    lax.fori_loop(0, tm, drain, 0)

    lo_sum = h_ref[:, :half]
    hi_sum = h_ref[:, half:]
    for k in range(TOP_K_EXPERTS):
        lo, hi = _unpack_halves(buf_ref[slot, k])
        wk = w_ref[:, k:k + 1]
        lo_sum = lo_sum + wk * lo
        hi_sum = hi_sum + wk * hi
    o_ref[0, :, :half] = lo_sum
    o_ref[0, :, half:] = hi_sum


def _combine(dest_flat, sel_w, h2, ye, *, batch, seq, tp):
    d = h2.shape[1]
    tm = Q_BLOCK
    nq = tp // tm
    return pl.pallas_call(
        functools.partial(_combine_kernel, nq=nq),
        grid_spec=pltpu.PrefetchScalarGridSpec(
            num_scalar_prefetch=1,
            grid=(batch, seq // tm),
            in_specs=[pl.BlockSpec((tm, TOP_K_EXPERTS), lambda b, i, dest: (b * nq + 1 + i, 0)),
                      pl.BlockSpec((tm, d), lambda b, i, dest: (b * nq + 1 + i, 0)),
                      pl.BlockSpec(memory_space=pl.ANY)],
            out_specs=pl.BlockSpec((1, tm, d), lambda b, i, dest: (b, i, 0)),
            scratch_shapes=[pltpu.VMEM((2, TOP_K_EXPERTS, tm, d // 2), I32), pltpu.SemaphoreType.DMA((2,))]),
        out_shape=jax.ShapeDtypeStruct((batch, seq, d), F32),
        name="moe_combine",
        compiler_params=_params("arbitrary", "arbitrary"),
    )(dest_flat, sel_w, h2, ye)


def _moe(h2, norm2_g, router_w, router_b, w_gate_up, b_gate_up, w_down, b_down, *, batch, seq, tp):
    n, d = h2.shape
    n_exp = router_w.shape[1]
    de = w_down.shape[1]
    assert n_exp <= LANES and n % ROUTE_TILE == 0
    rw = jnp.pad(router_w, ((0, 0), (0, LANES - n_exp))).astype(BF16)
    rb = jnp.pad(router_b, (0, LANES - n_exp), constant_values=NEG).reshape(1, LANES)
    xp, sel_e, sel_w, rank, counts = _router(h2, norm2_g, rw, rb)

    cnt = counts[0, :n_exp].astype(I32)
    padded = ((cnt + EXPERT_TILE - 1) // EXPERT_TILE) * EXPERT_TILE
    e_ids = jnp.arange(n_exp, dtype=I32)
    ends = jnp.sum(jnp.where(e_ids[:, None] <= e_ids[None, :], padded[:, None], 0), axis=0)
    starts = ends - padded
    total = ends[n_exp - 1]
    dest_flat = (starts[sel_e] + rank).reshape(-1)
    n_tiles = (n * TOP_K_EXPERTS) // EXPERT_TILE + n_exp
    tile_start = jnp.arange(n_tiles, dtype=I32) * EXPERT_TILE
    tile_valid = (tile_start < total).astype(I32)
    last_expert = jnp.sum((ends <= total - 1).astype(I32))
    tile_expert = jnp.minimum(jnp.sum((ends[None, :] <= tile_start[:, None]).astype(I32), axis=1), last_expert)

    w_gu = _deinterleave_gate_up(w_gate_up)
    nb = (2 * de) // (2 * LANES)
    b_gu = b_gate_up.reshape(n_exp, nb, LANES, 2).transpose(0, 1, 3, 2).reshape(n_exp, 1, 2 * de)

    xs = _dispatch(dest_flat, xp, n_tiles * EXPERT_TILE)
    act = _expert_up(tile_expert, tile_valid, xs, w_gu, b_gu)
    ye = _expert_down(tile_expert, tile_valid, act, w_down, b_down.reshape(n_exp, 1, d))
    return _combine(dest_flat, sel_w, h2, ye, batch=batch, seq=seq, tp=tp)


def kernel(x, meta_tokens, norm1_g, w_in, q_norm_g, w_uq, w_iq, kv_norm_g, q_head_norm_g, k_head_norm_g,
           idx_k_norm_g, w_uv, w_branch_attn, w_pool, pool_scale, w_branch_pool, w_out, norm2_g, router_w,
           router_b, w_gate_up, b_gate_up, w_down, b_down):
    batch, seq, d = x.shape
    n_meta = meta_tokens.shape[0]
    depth = norm1_g.shape[0]
    qr, kvr, di = q_norm_g.shape[1], kv_norm_g.shape[1], idx_k_norm_g.shape[1]
    n_heads, v = w_uv.shape[1], w_uv.shape[3]
    idx_heads = w_iq.shape[2] // di
    pw = pool_scale.shape[1]
    assert depth == 1
    assert n_meta <= CHUNK and seq % Q_BLOCK == 0 and idx_heads <= LANES
    assert qr % kvr == 0 and (qr + kvr) % di == 0 and di % LANES == 0
    pad_front = Q_BLOCK - n_meta
    tp = Q_BLOCK + seq
    tk = -(-tp // KEY_TILE) * KEY_TILE
    k_sel = min(TOPK_MAX, seq // 4)
    attn_scale = kvr ** -0.5
    idx_scale = (di ** -0.5) * (idx_heads ** -0.5)

    head_rows = jnp.concatenate([jnp.zeros((pad_front, d), x.dtype), meta_tokens.astype(x.dtype)], axis=0)

    out = None
    for l in range(depth):
        n_small = qr + kvr + di
        o_pool = n_small + idx_heads
        o_gate = o_pool + pw
        w_small = jnp.pad(w_in[l][:, :o_pool], ((0, 0), (0, LANES - idx_heads))).astype(BF16)
        w_a = w_in[l][:, o_pool:o_gate].astype(BF16)
        w_g = w_in[l][:, o_gate:].astype(BF16)

        h, xn = _embed_norm(x, head_rows, norm1_g[l])
        small = _matmul(xn, w_small, F32, tn=_tile(w_small.shape[1], 1024, LANES), name="in_proj_latents")
        a_pool = _matmul(xn, w_a, BF16, tn=_tile(pw, 1024, LANES), name="in_proj_pool")
        gates = _matmul(xn, w_g, BF16, tn=_tile(2 * d, 1024, LANES), sigmoid=True, name="in_proj_gates")

        y_pool = _pool_mixer(a_pool, w_pool[l].astype(BF16), pool_scale[l], batch=batch, tp=tp,
                             pad_front=pad_front)

        cq, widx = _prep_q(small, q_norm_g[l], qr, n_small, idx_scale)
        lat, kk, ki = _prep_kv(small, kv_norm_g[l], k_head_norm_g[l], idx_k_norm_g[l],
                               batch=batch, tp=tp, tk=tk, qr=qr, kvr=kvr, di=di)
        q = _qproj(cq, w_uq[l].astype(BF16), q_head_norm_g[l], n_heads=n_heads, hd=kvr,
                   heads_per_tile=min(2, n_heads),
                   scale=attn_scale, norm=True, name="q_proj_headnorm")
        qi = _qproj(cq, w_iq[l].astype(BF16), jnp.ones((di,), F32), n_heads=idx_heads, hd=di,
                    heads_per_tile=min(8, idx_heads), scale=1.0, norm=False, name="indexer_q_proj")
        logit_bound = (BOUND_MARGIN * kvr * attn_scale * jnp.max(jnp.abs(q_head_norm_g[l]))
                       * jnp.max(jnp.abs(k_head_norm_g[l])))
        y_attn = _sparse_attention(logit_bound, q, qi, widx, ki, kk, lat, w_uv[l].astype(BF16), batch=batch,
                                   tp=tp, k_sel=k_sel, pad_front=pad_front)

        merged = _merge(y_pool, y_attn, w_branch_pool[l].astype(BF16), w_branch_attn[l].astype(BF16), gates)
        h2 = _out_proj(merged, w_out[l].astype(BF16), h)

        out = _moe(h2, norm2_g[l], router_w[l], router_b[l], w_gate_up[l], b_gate_up[l], w_down[l],
                   b_down[l], batch=batch, seq=seq, tp=tp)
    return out
```

```python
import functools

import jax
import jax.numpy as jnp
from jax import lax
from jax.experimental import pallas as pl
from jax.experimental.pallas import tpu as pltpu

F32 = jnp.float32
BF16 = jnp.bfloat16
I32 = jnp.int32

CHUNK = 64
POOL_WINDOWS = (2, 4, 8, 16)
TOPK_MAX = 256
TOP_K_EXPERTS = 4
SWIGLU_LIMIT = 7.0
SWIGLU_ALPHA = 1.702
EPS = 1e-6
NEG = -1e30

Q_BLOCK = 128
KEY_TILE = 512
HEADS_PER_STEP = 32
INDEXER_HEADS_PER_DOT = 8
HEADS_PER_CHAIN = 4
HALO = 16
EXPERT_TILE = 256
ROUTE_TILE = 256
LANES = 128
INT_MIN = -2 ** 31
MAX_EXP_SPAN = 80.0
BOUND_MARGIN = 1.02
VMEM_LIMIT = 56 * 1024 * 1024


def _tile(n, target, mult):
    best = None
    for t in range(mult, min(n, target) + 1, mult):
        if n % t == 0:
            best = t
    assert best is not None, (n, target, mult)
    return best


def _params(*sem):
    return pltpu.CompilerParams(dimension_semantics=sem, vmem_limit_bytes=VMEM_LIMIT)


def _embed_norm_kernel(x_ref, head_ref, g_ref, h_ref, xn_ref):
    rows = jnp.where(pl.program_id(1) == 0, head_ref[...], x_ref[0])
    h_ref[...] = rows
    ms = jnp.mean(rows * rows, axis=-1, keepdims=True)
    xn_ref[...] = (rows * lax.rsqrt(ms + EPS) * g_ref[...]).astype(xn_ref.dtype)


def _embed_norm(x, head_rows, g):
    batch, seq, d = x.shape
    nq = seq // Q_BLOCK + 1
    n = batch * nq * Q_BLOCK
    return pl.pallas_call(
        _embed_norm_kernel,
        grid=(batch, nq),
        in_specs=[pl.BlockSpec((1, Q_BLOCK, d), lambda b, i: (b, jnp.maximum(i - 1, 0), 0)),
                  pl.BlockSpec((Q_BLOCK, d), lambda b, i: (0, 0)),
                  pl.BlockSpec((1, d), lambda b, i: (0, 0))],
        out_specs=[pl.BlockSpec((Q_BLOCK, d), lambda b, i: (b * nq + i, 0)),
                   pl.BlockSpec((Q_BLOCK, d), lambda b, i: (b * nq + i, 0))],
        out_shape=[jax.ShapeDtypeStruct((n, d), x.dtype), jax.ShapeDtypeStruct((n, d), BF16)],
        name="embed_rmsnorm",
        compiler_params=_params("parallel", "arbitrary"),
    )(x, head_rows, g.reshape(1, d))


def _matmul_kernel(x_ref, w_ref, o_ref, *, sigmoid):
    acc = jnp.dot(x_ref[...], w_ref[...], preferred_element_type=F32)
    if sigmoid:
        acc = jax.nn.sigmoid(acc)
    o_ref[...] = acc.astype(o_ref.dtype)


def _matmul(x, w, out_dtype, *, tn, name, sigmoid=False):
    m, k = x.shape
    n = w.shape[1]
    tm = _tile(m, 768, 128)
    return pl.pallas_call(
        functools.partial(_matmul_kernel, sigmoid=sigmoid),
        grid=(m // tm, n // tn),
        in_specs=[pl.BlockSpec((tm, k), lambda i, j: (i, 0)), pl.BlockSpec((k, tn), lambda i, j: (0, j))],
        out_specs=pl.BlockSpec((tm, tn), lambda i, j: (i, j)),
        out_shape=jax.ShapeDtypeStruct((m, n), out_dtype),
        name=name,
        compiler_params=_params("parallel", "arbitrary"),
    )(x, w)


def _prep_q_kernel(cq_ref, wi_ref, g_ref, cq_o, wi_o, *, idx_scale):
    x = cq_ref[...]
    ms = jnp.mean(x * x, axis=-1, keepdims=True)
    cq_o[...] = (x * lax.rsqrt(ms + EPS) * g_ref[...]).astype(cq_o.dtype)
    wi_o[...] = wi_ref[...] * idx_scale


def _prep_q(small, q_norm_g, qr, wi_col, idx_scale):
    n = small.shape[0]
    tm = _tile(n, 768, 128)
    return pl.pallas_call(
        functools.partial(_prep_q_kernel, idx_scale=idx_scale),
        grid=(n // tm,),
        in_specs=[pl.BlockSpec((tm, qr), lambda i: (i, 0)),
                  pl.BlockSpec((tm, LANES), lambda i: (i, wi_col // LANES)),
                  pl.BlockSpec((1, qr), lambda i: (0, 0))],
        out_specs=[pl.BlockSpec((tm, qr), lambda i: (i, 0)), pl.BlockSpec((tm, LANES), lambda i: (i, 0))],
        out_shape=[jax.ShapeDtypeStruct((n, qr), BF16), jax.ShapeDtypeStruct((n, LANES), F32)],
        name="prep_query_latent",
        compiler_params=_params("parallel"),
    )(small, small, q_norm_g.reshape(1, qr))


def _prep_kv_kernel(ckv_ref, kidx_ref, gkv_ref, gkh_ref, gki_ref, lat_o, kk_o, ki_o, *, n_valid):
    i = pl.program_id(1)

    @pl.when(i < n_valid)
    def _():
        c = ckv_ref[...]
        lat = c * lax.rsqrt(jnp.mean(c * c, axis=-1, keepdims=True) + EPS) * gkv_ref[...]
        kk = lat * lax.rsqrt(jnp.mean(lat * lat, axis=-1, keepdims=True) + EPS) * gkh_ref[...]
        k = kidx_ref[...]
        ki = k * lax.rsqrt(jnp.mean(k * k, axis=-1, keepdims=True) + EPS) * gki_ref[...]
        lat_o[0] = lat.astype(lat_o.dtype)
        kk_o[0] = kk.astype(kk_o.dtype)
        ki_o[0] = ki.astype(ki_o.dtype)

    @pl.when(i >= n_valid)
    def _():
        lat_o[...] = jnp.zeros_like(lat_o)
        kk_o[...] = jnp.zeros_like(kk_o)
        ki_o[...] = jnp.zeros_like(ki_o)


def _prep_kv(small, gkv, gkh, gki, *, batch, tp, tk, qr, kvr, di):
    nq = tp // Q_BLOCK
    nk = tk // Q_BLOCK

    def row(b, i):
        return b * nq + jnp.minimum(i, nq - 1)

    return pl.pallas_call(
        functools.partial(_prep_kv_kernel, n_valid=nq),
        grid=(batch, nk),
        in_specs=[pl.BlockSpec((Q_BLOCK, kvr), lambda b, i: (row(b, i), qr // kvr)),
                  pl.BlockSpec((Q_BLOCK, di), lambda b, i: (row(b, i), (qr + kvr) // di)),
                  pl.BlockSpec((1, kvr), lambda b, i: (0, 0)),
                  pl.BlockSpec((1, kvr), lambda b, i: (0, 0)),
                  pl.BlockSpec((1, di), lambda b, i: (0, 0))],
        out_specs=[pl.BlockSpec((1, Q_BLOCK, kvr), lambda b, i: (b, i, 0)),
                   pl.BlockSpec((1, Q_BLOCK, kvr), lambda b, i: (b, i, 0)),
                   pl.BlockSpec((1, Q_BLOCK, di), lambda b, i: (b, i, 0))],
        out_shape=[jax.ShapeDtypeStruct((batch, tk, kvr), BF16),
                   jax.ShapeDtypeStruct((batch, tk, kvr), BF16),
                   jax.ShapeDtypeStruct((batch, tk, di), BF16)],
        name="prep_keys",
        compiler_params=_params("parallel", "arbitrary"),
    )(small, small, gkv.reshape(1, kvr), gkh.reshape(1, kvr), gki.reshape(1, di))


def _pool_kernel(a_ref, halo_ref, w_ref, sc_ref, o_ref, xs_ref, *, tp_tile, pad_front, pg):
    i = pl.program_id(1)
    t = i * tp_tile + lax.broadcasted_iota(I32, (tp_tile, 1), 0) - pad_front
    for g, win in enumerate(POOL_WINDOWS):
        cols = slice(g * pg, (g + 1) * pg)
        xs_ref[0:HALO, :] = halo_ref[0, :, cols].astype(F32)
        xs_ref[HALO:, :] = a_ref[0, :, cols].astype(F32)
        cur = xs_ref[pl.ds(HALO, tp_tile), :]
        acc = cur
        for k in range(1, win):
            acc = acc + xs_ref[pl.ds(HALO - k, tp_tile), :]
        cnt = jnp.clip(t + 1, 1, win).astype(F32)
        pooled = (acc / cnt - cur).astype(BF16)
        y = jnp.dot(pooled, w_ref[g], preferred_element_type=F32) * sc_ref[:, cols]
        o_ref[0, :, cols] = y.astype(o_ref.dtype)


def _pool_mixer(a, w_pool, pool_scale, *, batch, tp, pad_front):
    pw = a.shape[-1]
    n_groups, pg, _ = w_pool.shape
    tpt = _tile(tp, 1536, HALO)
    a3 = a.reshape(batch, tp, pw)
    halo_blocks = tpt // HALO
    out = pl.pallas_call(
        functools.partial(_pool_kernel, tp_tile=tpt, pad_front=pad_front, pg=pg),
        grid=(batch, tp // tpt),
        in_specs=[pl.BlockSpec((1, tpt, pw), lambda b, i: (b, i, 0)),
                  pl.BlockSpec((1, HALO, pw), lambda b, i: (b, jnp.maximum(i * halo_blocks - 1, 0), 0)),
                  pl.BlockSpec((n_groups, pg, pg), lambda b, i: (0, 0, 0)),
                  pl.BlockSpec((1, pw), lambda b, i: (0, 0))],
        out_specs=pl.BlockSpec((1, tpt, pw), lambda b, i: (b, i, 0)),
        out_shape=jax.ShapeDtypeStruct((batch, tp, pw), BF16),
        scratch_shapes=[pltpu.VMEM((tpt + HALO, pg), F32)],
        name="pool_mixer",
        compiler_params=_params("parallel", "arbitrary"),
    )(a3, a3, w_pool, pool_scale.reshape(1, pw))
    return out.reshape(batch * tp, pw)


def _qproj_kernel(x_ref, w_ref, g_ref, o_ref, *, scale, norm, heads, hd):
    acc = jnp.dot(x_ref[...], w_ref[...], preferred_element_type=F32)
    nblk = o_ref.shape[0]
    for hh in range(heads):
        a = acc[:, hh * hd:(hh + 1) * hd]
        if norm:
            a = a * lax.rsqrt(jnp.mean(a * a, axis=-1, keepdims=True) + EPS) * (g_ref[...] * scale)
        a = a.astype(o_ref.dtype)
        for r in range(nblk):
            o_ref[r, hh] = a[r * Q_BLOCK:(r + 1) * Q_BLOCK]


def _qproj(cq, w, g, *, n_heads, hd, heads_per_tile, scale, norm, name):
    n, r = cq.shape
    tm = _tile(n, 1536, Q_BLOCK)
    nblk = tm // Q_BLOCK
    tn = heads_per_tile * hd
    return pl.pallas_call(
        functools.partial(_qproj_kernel, scale=scale, norm=norm, heads=heads_per_tile, hd=hd),
        grid=(n // tm, n_heads // heads_per_tile),
        in_specs=[pl.BlockSpec((tm, r), lambda i, j: (i, 0)),
                  pl.BlockSpec((r, tn), lambda i, j: (0, j)),
                  pl.BlockSpec((1, hd), lambda i, j: (0, 0))],
        out_specs=pl.BlockSpec((nblk, heads_per_tile, Q_BLOCK, hd), lambda i, j: (i, j, 0, 0)),
        out_shape=jax.ShapeDtypeStruct((n // Q_BLOCK, n_heads, Q_BLOCK, hd), BF16),
        name=name,
        compiler_params=_params("parallel", "arbitrary"),
    )(cq, w, g.reshape(1, hd))


def _attn_kernel(shift_ref, q_ref, qi_ref, wi_ref, ki_ref, kk_ref, lat_ref, wuv_ref, o_ref,
                 bias_ref, keyp_ref, acc_ref, m_ref, l_ref, *, k_sel, pad_front, idx_heads):
    j = pl.program_id(1)
    hg = pl.program_id(2)
    shift = shift_ref[0]
    use_shift = shift_ref[1] > 0.5
    hps, qb, c = q_ref.shape[1], q_ref.shape[2], q_ref.shape[3]
    di = qi_ref.shape[3]
    v = wuv_ref.shape[2]
    n_t = (j * qb + qb + KEY_TILE - 1) // KEY_TILE
    nt_dims = (((1,), (1,)), ((), ()))

    @pl.when(hg == 0)
    def _select():
        tq = j * qb + lax.broadcasted_iota(I32, (qb, 1), 0)
        limit = ((jnp.maximum(tq, CHUNK) + CHUNK) // CHUNK) * CHUNK

        def score_tile(kt, carry):
            off = pl.multiple_of(kt * KEY_TILE, KEY_TILE)
            ki_t = ki_ref[0, pl.ds(off, KEY_TILE), :]
            part = jnp.zeros((qb, KEY_TILE), F32)
            hpi = min(INDEXER_HEADS_PER_DOT, idx_heads)
            for g in range(idx_heads // hpi):
                qg = qi_ref[0, g * hpi:(g + 1) * hpi].reshape(hpi * qb, di)
                s = lax.dot_general(qg, ki_t, nt_dims, preferred_element_type=F32)
                s = jnp.maximum(s, 0.0)
                for hh in range(hpi):
                    h = g * hpi + hh
                    part = part + s[hh * qb:(hh + 1) * qb] * wi_ref[:, h:h + 1]
            bits = pltpu.bitcast(part, I32)
            key = bits ^ ((bits >> 31) & 0x7FFFFFFF)
            s_idx = off + lax.broadcasted_iota(I32, (qb, KEY_TILE), 1)
            adm = (s_idx >= pad_front) & (s_idx < limit)
            keyp_ref[:, pl.ds(off, KEY_TILE)] = jnp.where(adm, key, INT_MIN)
            return carry

        lax.fori_loop(0, n_t, score_tile, 0)

        def count_ge(thr):
            def body(kt, cnt):
                off = pl.multiple_of(kt * KEY_TILE, KEY_TILE)
                ge = jnp.where(keyp_ref[:, pl.ds(off, KEY_TILE)] >= thr, 1.0, 0.0)
                for u in range(KEY_TILE // LANES):
                    cnt = cnt + ge[:, u * LANES:(u + 1) * LANES]
                return cnt
            cnt = lax.fori_loop(0, n_t, body, jnp.zeros((qb, LANES), F32))
            return jnp.sum(cnt, axis=1, keepdims=True)

        def bit_step(bi, thr):
            cand = thr + lax.shift_left(jnp.int32(1), 31 - bi)
            return jnp.where(count_ge(cand) >= k_sel, cand, thr)

        thr = lax.fori_loop(0, 32, bit_step, jnp.full((qb, 1), INT_MIN, I32))
        thr = jnp.maximum(thr, INT_MIN + 1)

        def bias_tile(kt, carry):
            off = pl.multiple_of(kt * KEY_TILE, KEY_TILE)
            sel = keyp_ref[:, pl.ds(off, KEY_TILE)] >= thr
            bias_ref[:, pl.ds(off, KEY_TILE)] = jnp.where(sel, -shift, NEG)
            return carry

        lax.fori_loop(0, n_t, bias_tile, 0)

        @pl.when(jnp.max(count_ge(thr)) > k_sel)
        def _ties():
            quota = k_sel - count_ge(thr + 1)
            upper = (lax.broadcasted_iota(I32, (LANES, LANES), 0)
                     < lax.broadcasted_iota(I32, (LANES, LANES), 1)).astype(BF16)

            def chunk(ci, seen):
                off = pl.multiple_of(ci * LANES, LANES)
                kp = keyp_ref[:, pl.ds(off, LANES)]
                tie = jnp.where(kp == thr, 1.0, 0.0)
                rank = jnp.dot(tie.astype(BF16), upper, preferred_element_type=F32) + seen
                sel = jnp.where(kp > thr, 1.0, tie * jnp.where(rank < quota, 1.0, 0.0))
                bias_ref[:, pl.ds(off, LANES)] = jnp.where(sel > 0.5, -shift, NEG)
                return seen + jnp.sum(tie, axis=1, keepdims=True)

            lax.fori_loop(0, n_t * (KEY_TILE // LANES), chunk, jnp.zeros((qb, 1), F32))

    l_ref[...] = jnp.zeros(l_ref.shape, F32)
    acc_ref[...] = jnp.zeros(acc_ref.shape, F32)
    hpc = min(HEADS_PER_CHAIN, hps)
    rc = hpc * qb
    k_slabs = KEY_TILE // LANES
    c_slabs = c // LANES

    def masked_logits(ci, off):
        q_c = q_ref[0, ci * hpc:(ci + 1) * hpc].reshape(rc, c)
        s = lax.dot_general(q_c, kk_ref[0, pl.ds(off, KEY_TILE), :], nt_dims, preferred_element_type=F32)
        s = (s.reshape(hpc, qb, KEY_TILE) + bias_ref[:, pl.ds(off, KEY_TILE)][None]).reshape(rc, KEY_TILE)
        return [s[:, u * LANES:(u + 1) * LANES] for u in range(k_slabs)]

    def project_out(inv_l):
        for hh in range(hps):
            rs = slice(hh * qb, (hh + 1) * qb)
            o = jnp.concatenate([acc_ref[rs, u * LANES:(u + 1) * LANES] * inv_l[rs] for u in range(c_slabs)],
                                axis=1)
            y = jnp.dot(o.astype(BF16), wuv_ref[hh], preferred_element_type=F32)
            o_ref[:, hh * v:(hh + 1) * v] = y.astype(o_ref.dtype)

    @pl.when(use_shift)
    def _static_shift():
        def kv_step(kt, carry):
            off = pl.multiple_of(kt * KEY_TILE, KEY_TILE)
            lat_t = lat_ref[0, pl.ds(off, KEY_TILE), :]
            for ci in range(hps // hpc):
                rs = slice(ci * rc, (ci + 1) * rc)
                ps = [jnp.exp(sl) for sl in masked_logits(ci, off)]
                psum = ps[0]
                for pu in ps[1:]:
                    psum = psum + pu
                l_ref[rs, :] = l_ref[rs, :] + psum
                pv = jnp.dot(jnp.concatenate(ps, axis=1).astype(BF16), lat_t, preferred_element_type=F32)
                for u in range(c_slabs):
                    cs = slice(u * LANES, (u + 1) * LANES)
                    acc_ref[rs, cs] = acc_ref[rs, cs] + pv[:, cs]
            return carry

        lax.fori_loop(0, n_t, kv_step, 0)
        project_out(jnp.broadcast_to(1.0 / jnp.sum(l_ref[...], axis=1, keepdims=True), l_ref.shape))

    @pl.when(jnp.logical_not(use_shift))
    def _online():
        m_ref[...] = jnp.full(m_ref.shape, -3e38, F32)

        def kv_step(kt, carry):
            off = pl.multiple_of(kt * KEY_TILE, KEY_TILE)
            lat_t = lat_ref[0, pl.ds(off, KEY_TILE), :]
            for ci in range(hps // hpc):
                rs = slice(ci * rc, (ci + 1) * rc)
                slabs = masked_logits(ci, off)
                mx = slabs[0]
                for sl in slabs[1:]:
                    mx = jnp.maximum(mx, sl)
                m_prev = m_ref[rs, :]
                m_new = jnp.maximum(m_prev, jnp.max(mx, axis=1, keepdims=True))
                alpha = jnp.exp(m_prev - m_new)
                ps = [jnp.exp(sl - m_new) for sl in slabs]
                psum = ps[0]
                for pu in ps[1:]:
                    psum = psum + pu
                l_ref[rs, :] = alpha * l_ref[rs, :] + jnp.sum(psum, axis=1, keepdims=True)
                pv = jnp.dot(jnp.concatenate(ps, axis=1).astype(BF16), lat_t, preferred_element_type=F32)
                for u in range(c_slabs):
                    cs = slice(u * LANES, (u + 1) * LANES)
                    acc_ref[rs, cs] = alpha * acc_ref[rs, cs] + pv[:, cs]
                m_ref[rs, :] = m_new
            return carry

        lax.fori_loop(0, n_t, kv_step, 0)
        project_out(1.0 / l_ref[...])


def _sparse_attention(logit_bound, q, qi, widx, ki, kk, lat, w_uv, *, batch, tp, k_sel, pad_front):
    nblk, n_heads, qb, c = q.shape
    usable = 2.0 * logit_bound <= MAX_EXP_SPAN
    shift_info = jnp.stack([jnp.where(usable, logit_bound, 0.0), usable.astype(F32)]).astype(F32)
    idx_heads, di = qi.shape[1], qi.shape[3]
    tk = ki.shape[1]
    v = w_uv.shape[2]
    nq = tp // qb
    hps = min(HEADS_PER_STEP, n_heads)
    rows = hps * qb
    return pl.pallas_call(
        functools.partial(_attn_kernel, k_sel=k_sel, pad_front=pad_front, idx_heads=idx_heads),
        grid=(batch, nq, n_heads // hps),
        in_specs=[pl.BlockSpec(memory_space=pltpu.SMEM),
                  pl.BlockSpec((1, hps, qb, c), lambda b, j, h: (b * nq + j, h, 0, 0)),
                  pl.BlockSpec((1, idx_heads, qb, di), lambda b, j, h: (b * nq + j, 0, 0, 0)),
                  pl.BlockSpec((qb, LANES), lambda b, j, h: (b * nq + j, 0)),
                  pl.BlockSpec((1, tk, di), lambda b, j, h: (b, 0, 0), pipeline_mode=pl.Buffered(1)),
                  pl.BlockSpec((1, tk, c), lambda b, j, h: (b, 0, 0), pipeline_mode=pl.Buffered(1)),
                  pl.BlockSpec((1, tk, c), lambda b, j, h: (b, 0, 0), pipeline_mode=pl.Buffered(1)),
                  pl.BlockSpec((hps, c, v), lambda b, j, h: (h, 0, 0))],
        out_specs=pl.BlockSpec((qb, hps * v), lambda b, j, h: (b * nq + j, h)),
        out_shape=jax.ShapeDtypeStruct((nblk * qb, n_heads * v), BF16),
        scratch_shapes=[pltpu.VMEM((qb, tk), F32), pltpu.VMEM((qb, tk), I32),
                        pltpu.VMEM((rows, c), F32), pltpu.VMEM((rows, LANES), F32),
                        pltpu.VMEM((rows, LANES), F32)],
        name="sparse_attention",
        compiler_params=_params("parallel", "arbitrary", "arbitrary"),
    )(shift_info, q, qi, widx, ki, kk, lat, w_uv)


def _merge_kernel(yp_ref, ya_ref, wp_ref, wa_ref, gp_ref, ga_ref, o_ref):
    pool = jnp.dot(yp_ref[...], wp_ref[...], preferred_element_type=F32)
    attn = jnp.dot(ya_ref[...], wa_ref[...], preferred_element_type=F32)
    o_ref[...] = (gp_ref[...].astype(F32) * pool + ga_ref[...].astype(F32) * attn).astype(o_ref.dtype)


def _merge(y_pool, y_attn, w_bp, w_ba, gates):
    n, pw = y_pool.shape
    aw = y_attn.shape[1]
    d = w_bp.shape[1]
    tm = _tile(n, 768, 128)
    tn = 512
    return pl.pallas_call(
        _merge_kernel,
        grid=(n // tm, d // tn),
        in_specs=[pl.BlockSpec((tm, pw), lambda i, j: (i, 0)),
                  pl.BlockSpec((tm, aw), lambda i, j: (i, 0)),
                  pl.BlockSpec((pw, tn), lambda i, j: (0, j)),
                  pl.BlockSpec((aw, tn), lambda i, j: (0, j)),
                  pl.BlockSpec((tm, tn), lambda i, j: (i, j)),
                  pl.BlockSpec((tm, tn), lambda i, j: (i, j + d // tn))],
        out_specs=pl.BlockSpec((tm, tn), lambda i, j: (i, j)),
        out_shape=jax.ShapeDtypeStruct((n, d), BF16),
        name="branch_merge",
        compiler_params=_params("parallel", "arbitrary"),
    )(y_pool, y_attn, w_bp, w_ba, gates, gates)


def _out_proj_kernel(x_ref, w_ref, r_ref, o_ref):
    o_ref[...] = r_ref[...] + jnp.dot(x_ref[...], w_ref[...], preferred_element_type=F32)


def _out_proj(merged, w_out, resid):
    n, k = merged.shape
    d = w_out.shape[1]
    tm = _tile(n, 768, 128)
    tn = _tile(d, 1024, LANES)
    return pl.pallas_call(
        _out_proj_kernel,
        grid=(n // tm, d // tn),
        in_specs=[pl.BlockSpec((tm, k), lambda i, j: (i, 0)),
                  pl.BlockSpec((k, tn), lambda i, j: (0, j)),
                  pl.BlockSpec((tm, tn), lambda i, j: (i, j))],
        out_specs=pl.BlockSpec((tm, tn), lambda i, j: (i, j)),
        out_shape=jax.ShapeDtypeStruct((n, d), F32),
        name="out_proj_residual",
        compiler_params=_params("parallel", "arbitrary"),
    )(merged, w_out, resid)


def _pack_halves(x):
    w = x.shape[1] // 2
    lo = pltpu.bitcast(x[:, :w].astype(BF16).astype(F32), I32)
    hi = pltpu.bitcast(x[:, w:].astype(BF16).astype(F32), I32)
    return ((lo >> 16) & 0xFFFF) | (hi & -65536)


def _unpack_halves(p):
    return pltpu.bitcast(p << 16, F32), pltpu.bitcast(p & -65536, F32)


def _router_kernel(h_ref, g_ref, rw_ref, rb_ref, xp_o, e_o, w_o, rank_o, cnt_o, carry_ref):
    i = pl.program_id(0)

    @pl.when(i == 0)
    def _():
        carry_ref[...] = jnp.zeros_like(carry_ref)

    x = h_ref[...]
    xn = x * lax.rsqrt(jnp.mean(x * x, axis=-1, keepdims=True) + EPS) * g_ref[...]
    xp_o[...] = _pack_halves(xn)
    logits = jnp.dot(xn.astype(BF16), rw_ref[...], preferred_element_type=F32) + rb_ref[...]
    tm = logits.shape[0]
    lane = lax.broadcasted_iota(I32, (tm, LANES), 1).astype(F32)
    vals, hots = [], []
    cur = logits
    for k in range(TOP_K_EXPERTS):
        m = jnp.max(cur, axis=1, keepdims=True)
        idx = jnp.min(jnp.where(cur == m, lane, float(LANES)), axis=1, keepdims=True)
        hot = lane == idx
        vals.append(m)
        hots.append(hot)
        e_o[:, k:k + 1] = idx.astype(I32)
        cur = jnp.where(hot, -jnp.inf, cur)
    exps = [jnp.exp(vk - vals[0]) for vk in vals]
    denom = exps[0]
    for ek in exps[1:]:
        denom = denom + ek
    for k in range(TOP_K_EXPERTS):
        w_o[:, k:k + 1] = exps[k] / denom
    onehot = jnp.zeros((tm, LANES), F32)
    for hot in hots:
        onehot = onehot + jnp.where(hot, 1.0, 0.0)
    lower = (lax.broadcasted_iota(I32, (tm, tm), 0) > lax.broadcasted_iota(I32, (tm, tm), 1)).astype(BF16)
    before = jnp.dot(lower, onehot.astype(BF16), preferred_element_type=F32) + carry_ref[...]
    for k in range(TOP_K_EXPERTS):
        rank_o[:, k:k + 1] = jnp.sum(jnp.where(hots[k], before, 0.0), axis=1, keepdims=True).astype(I32)
    carry_ref[...] = carry_ref[...] + jnp.sum(onehot, axis=0, keepdims=True)
    cnt_o[...] = carry_ref[...]


def _router(h2, norm2_g, rw, rb):
    n, d = h2.shape
    tm = ROUTE_TILE
    kk = TOP_K_EXPERTS
    return pl.pallas_call(
        _router_kernel,
        grid=(n // tm,),
        in_specs=[pl.BlockSpec((tm, d), lambda i: (i, 0)),
                  pl.BlockSpec((1, d), lambda i: (0, 0)),
                  pl.BlockSpec((d, LANES), lambda i: (0, 0)),
                  pl.BlockSpec((1, LANES), lambda i: (0, 0))],
        out_specs=[pl.BlockSpec((tm, d // 2), lambda i: (i, 0)),
                   pl.BlockSpec((tm, kk), lambda i: (i, 0)),
                   pl.BlockSpec((tm, kk), lambda i: (i, 0)),
                   pl.BlockSpec((tm, kk), lambda i: (i, 0)),
                   pl.BlockSpec((1, LANES), lambda i: (0, 0))],
        out_shape=[jax.ShapeDtypeStruct((n, d // 2), I32),
                   jax.ShapeDtypeStruct((n, kk), I32),
                   jax.ShapeDtypeStruct((n, kk), F32),
                   jax.ShapeDtypeStruct((n, kk), I32),
                   jax.ShapeDtypeStruct((1, LANES), F32)],
        scratch_shapes=[pltpu.VMEM((1, LANES), F32)],
        name="moe_router",
        compiler_params=_params("arbitrary"),
    )(h2, norm2_g.reshape(1, d), rw, rb)


def _row_copy(src_ref, src_row, dst_ref, dst_row, sem):
    return pltpu.make_async_copy(src_ref.at[pl.ds(src_row, 1), :], dst_ref.at[pl.ds(dst_row, 1), :], sem)


def _dispatch_kernel(dest_ref, pad_start_ref, pad_count_ref, xp_ref, xs_ref, zero_ref, sem, zero_sem):
    tm = xp_ref.shape[0]
    base = pl.program_id(0) * tm * TOP_K_EXPERTS

    def issue(r, carry):
        for k in range(TOP_K_EXPERTS):
            _row_copy(xp_ref, r, xs_ref, dest_ref[base + r * TOP_K_EXPERTS + k], sem).start()
        return carry

    def drain(r, carry):
        for k in range(TOP_K_EXPERTS):
            _row_copy(xp_ref, 0, xs_ref, 0, sem).wait()
        return carry

    lax.fori_loop(0, tm, issue, 0)
    lax.fori_loop(0, tm, drain, 0)

    @pl.when(pl.program_id(0) == pl.num_programs(0) - 1)
    def _():
        zero_ref[...] = jnp.zeros_like(zero_ref)

        def per_expert(e, total):
            def zero_row(r, carry):
                _row_copy(zero_ref, 0, xs_ref, pad_start_ref[e] + r, zero_sem).start()
                return carry
            lax.fori_loop(0, pad_count_ref[e], zero_row, 0)
            return total + pad_count_ref[e]

        total = lax.fori_loop(0, pad_start_ref.shape[0], per_expert, 0)

        def zero_done(r, carry):
            _row_copy(zero_ref, 0, xs_ref, 0, zero_sem).wait()
            return carry

        lax.fori_loop(0, total, zero_done, 0)


def _dispatch(dest_flat, pad_start, pad_count, xp, n_slots):
    n, w = xp.shape
    tm = ROUTE_TILE
    return pl.pallas_call(
        _dispatch_kernel,
        grid_spec=pltpu.PrefetchScalarGridSpec(
            num_scalar_prefetch=3,
            grid=(n // tm,),
            in_specs=[pl.BlockSpec((tm, w), lambda i, *_: (i, 0))],
            out_specs=pl.BlockSpec(memory_space=pl.ANY),
            scratch_shapes=[pltpu.VMEM((8, w), I32), pltpu.SemaphoreType.DMA(()),
                            pltpu.SemaphoreType.DMA(())]),
        out_shape=jax.ShapeDtypeStruct((n_slots, w), I32),
        name="moe_dispatch",
        compiler_params=_params("arbitrary"),
    )(dest_flat, pad_start, pad_count, xp)


def _deinterleave_kernel(w_ref, o_ref):
    blk = 2 * LANES
    dst = lax.broadcasted_iota(I32, (blk, blk), 1)
    src = jnp.where(dst < LANES, 2 * dst, 2 * (dst - LANES) + 1)
    perm = (lax.broadcasted_iota(I32, (blk, blk), 0) == src).astype(BF16)
    for b in range(w_ref.shape[2] // blk):
        cols = slice(b * blk, (b + 1) * blk)
        o_ref[0, :, cols] = jnp.dot(w_ref[0, :, cols].astype(BF16), perm,
                                    preferred_element_type=F32).astype(o_ref.dtype)


def _deinterleave_gate_up(w_gate_up):
    n_exp, d, de2 = w_gate_up.shape
    assert de2 % (2 * LANES) == 0
    tk = _tile(d, 512, 16)
    return pl.pallas_call(
        _deinterleave_kernel,
        grid=(n_exp, d // tk),
        in_specs=[pl.BlockSpec((1, tk, de2), lambda e, i: (e, i, 0))],
        out_specs=pl.BlockSpec((1, tk, de2), lambda e, i: (e, i, 0)),
        out_shape=jax.ShapeDtypeStruct((n_exp, d, de2), BF16),
        name="moe_weight_regroup",
        compiler_params=_params("parallel", "arbitrary"),
    )(w_gate_up)


def _expert_up_kernel(te_ref, tv_ref, xs_ref, w_ref, b_ref, o_ref):
    t = pl.program_id(0)

    @pl.when(tv_ref[t] > 0)
    def _():
        lo, hi = _unpack_halves(xs_ref[...])
        x = jnp.concatenate([lo.astype(BF16), hi.astype(BF16)], axis=1)
        gu = jnp.dot(x, w_ref[0], preferred_element_type=F32) + b_ref[0]
        for blk in range(gu.shape[1] // (2 * LANES)):
            g0 = blk * 2 * LANES
            gate = jnp.minimum(gu[:, g0:g0 + LANES], SWIGLU_LIMIT)
            up = jnp.clip(gu[:, g0 + LANES:g0 + 2 * LANES], -SWIGLU_LIMIT, SWIGLU_LIMIT)
            act = (up + 1.0) * gate * jax.nn.sigmoid(SWIGLU_ALPHA * gate)
            o_ref[:, blk * LANES:(blk + 1) * LANES] = act.astype(o_ref.dtype)

    @pl.when(tv_ref[t] == 0)
    def _():
        o_ref[...] = jnp.zeros_like(o_ref)


def _expert_up(tile_expert, tile_valid, xs, w_gu, b_gu):
    n_slots, w = xs.shape
    n_exp, d, de2 = w_gu.shape
    tm = EXPERT_TILE
    return pl.pallas_call(
        _expert_up_kernel,
        grid_spec=pltpu.PrefetchScalarGridSpec(
            num_scalar_prefetch=2,
            grid=(n_slots // tm,),
            in_specs=[pl.BlockSpec((tm, w), lambda t, te, tv: (t, 0)),
                      pl.BlockSpec((1, d, de2), lambda t, te, tv: (te[t], 0, 0)),
                      pl.BlockSpec((1, 1, de2), lambda t, te, tv: (te[t], 0, 0))],
            out_specs=pl.BlockSpec((tm, de2 // 2), lambda t, te, tv: (t, 0))),
        out_shape=jax.ShapeDtypeStruct((n_slots, de2 // 2), BF16),
        name="moe_expert_up",
        compiler_params=_params("arbitrary"),
    )(tile_expert, tile_valid, xs, w_gu, b_gu)


def _expert_down_kernel(te_ref, tv_ref, a_ref, w_ref, b_ref, o_ref, wbf_ref):
    t = pl.program_id(0)

    @pl.when(jnp.logical_or(t == 0, te_ref[t] != te_ref[jnp.maximum(t - 1, 0)]))
    def _():
        wbf_ref[...] = w_ref[0].astype(BF16)

    @pl.when(tv_ref[t] > 0)
    def _():
        y = jnp.dot(a_ref[...], wbf_ref[...], preferred_element_type=F32) + b_ref[0]
        o_ref[...] = _pack_halves(y)

    @pl.when(tv_ref[t] == 0)
    def _():
        o_ref[...] = jnp.zeros_like(o_ref)


def _expert_down(tile_expert, tile_valid, act, w_down, b_down):
    n_slots, de = act.shape
    n_exp, _, d = w_down.shape
    tm = EXPERT_TILE
    return pl.pallas_call(
        _expert_down_kernel,
        grid_spec=pltpu.PrefetchScalarGridSpec(
            num_scalar_prefetch=2,
            grid=(n_slots // tm,),
            in_specs=[pl.BlockSpec((tm, de), lambda t, te, tv: (t, 0)),
                      pl.BlockSpec((1, de, d), lambda t, te, tv: (te[t], 0, 0)),
                      pl.BlockSpec((1, 1, d), lambda t, te, tv: (te[t], 0, 0))],
            out_specs=pl.BlockSpec((tm, d // 2), lambda t, te, tv: (t, 0)),
            scratch_shapes=[pltpu.VMEM((de, d), BF16)]),
        out_shape=jax.ShapeDtypeStruct((n_slots, d // 2), I32),
        name="moe_expert_down",
        compiler_params=_params("arbitrary"),
    )(tile_expert, tile_valid, act, w_down, b_down)


def _combine_kernel(dest_ref, w_ref, h_ref, ye_ref, o_ref, buf_ref, sems, *, nq):
    b = pl.program_id(0)
    i = pl.program_id(1)
    n_i = pl.num_programs(1)
    n_steps = pl.num_programs(0) * n_i
    tm = h_ref.shape[0]
    half = buf_ref.shape[3]
    step = b * n_i + i
    slot = step % 2

    def start_rows(bb, ii, sl):
        base = ((bb * nq + 1 + ii) * tm) * TOP_K_EXPERTS

        def issue(r, carry):
            for k in range(TOP_K_EXPERTS):
                _row_copy(ye_ref, dest_ref[base + r * TOP_K_EXPERTS + k], buf_ref.at[sl, k], r,
                          sems.at[sl]).start()
            return carry

        lax.fori_loop(0, tm, issue, 0)

    @pl.when(step == 0)
    def _():
        start_rows(b, i, slot)

    @pl.when(step + 1 < n_steps)
    def _():
        wrap = i + 1 == n_i
        start_rows(jnp.where(wrap, b + 1, b), jnp.where(wrap, 0, i + 1), 1 - slot)

    def drain(r, carry):
        for k in range(TOP_K_EXPERTS):
            _row_copy(ye_ref, 0, buf_ref.at[slot, k], 0, sems.at[slot]).wait()
        return carry

    lax.fori_loop(0, tm, drain, 0)

    lo_sum = h_ref[:, :half]
    hi_sum = h_ref[:, half:]
    for k in range(TOP_K_EXPERTS):
        lo, hi = _unpack_halves(buf_ref[slot, k])
        wk = w_ref[:, k:k + 1]
        lo_sum = lo_sum + wk * lo
        hi_sum = hi_sum + wk * hi
    o_ref[0, :, :half] = lo_sum
    o_ref[0, :, half:] = hi_sum


def _combine(dest_flat, sel_w, h2, ye, *, batch, seq, tp):
    d = h2.shape[1]
    tm = Q_BLOCK
    nq = tp // tm
    return pl.pallas_call(
        functools.partial(_combine_kernel, nq=nq),
        grid_spec=pltpu.PrefetchScalarGridSpec(
            num_scalar_prefetch=1,
            grid=(batch, seq // tm),
            in_specs=[pl.BlockSpec((tm, TOP_K_EXPERTS), lambda b, i, dest: (b * nq + 1 + i, 0)),
                      pl.BlockSpec((tm, d), lambda b, i, dest: (b * nq + 1 + i, 0)),
                      pl.BlockSpec(memory_space=pl.ANY)],
            out_specs=pl.BlockSpec((1, tm, d), lambda b, i, dest: (b, i, 0)),
            scratch_shapes=[pltpu.VMEM((2, TOP_K_EXPERTS, tm, d // 2), I32), pltpu.SemaphoreType.DMA((2,))]),
        out_shape=jax.ShapeDtypeStruct((batch, seq, d), F32),
        name="moe_combine",
        compiler_params=_params("arbitrary", "arbitrary"),
    )(dest_flat, sel_w, h2, ye)


def _moe(h2, norm2_g, router_w, router_b, w_gate_up, b_gate_up, w_down, b_down, *, batch, seq, tp):
    n, d = h2.shape
    n_exp = router_w.shape[1]
    de = w_down.shape[1]
    assert n_exp <= LANES and n % ROUTE_TILE == 0
    rw = jnp.pad(router_w, ((0, 0), (0, LANES - n_exp))).astype(BF16)
    rb = jnp.pad(router_b, (0, LANES - n_exp), constant_values=NEG).reshape(1, LANES)
    xp, sel_e, sel_w, rank, counts = _router(h2, norm2_g, rw, rb)

    cnt = counts[0, :n_exp].astype(I32)
    padded = ((cnt + EXPERT_TILE - 1) // EXPERT_TILE) * EXPERT_TILE
    e_ids = jnp.arange(n_exp, dtype=I32)
    ends = jnp.sum(jnp.where(e_ids[:, None] <= e_ids[None, :], padded[:, None], 0), axis=0)
    starts = ends - padded
    total = ends[n_exp - 1]
    dest_flat = (starts[sel_e] + rank).reshape(-1)
    n_tiles = (n * TOP_K_EXPERTS) // EXPERT_TILE + n_exp
    tile_start = jnp.arange(n_tiles, dtype=I32) * EXPERT_TILE
    tile_valid = (tile_start < total).astype(I32)
    last_expert = jnp.sum((ends <= total - 1).astype(I32))
    tile_expert = jnp.minimum(jnp.sum((ends[None, :] <= tile_start[:, None]).astype(I32), axis=1), last_expert)

    w_gu = _deinterleave_gate_up(w_gate_up)
    nb = (2 * de) // (2 * LANES)
    b_gu = b_gate_up.reshape(n_exp, nb, LANES, 2).transpose(0, 1, 3, 2).reshape(n_exp, 1, 2 * de)

    n_slots = n_tiles * EXPERT_TILE
    pad_start = jnp.concatenate([starts + cnt, total[None]])
    pad_count = jnp.concatenate([padded - cnt, (n_slots - total)[None]])
    xs = _dispatch(dest_flat, pad_start, pad_count, xp, n_slots)
    act = _expert_up(tile_expert, tile_valid, xs, w_gu, b_gu)
    ye = _expert_down(tile_expert, tile_valid, act, w_down, b_down.reshape(n_exp, 1, d))
    return _combine(dest_flat, sel_w, h2, ye, batch=batch, seq=seq, tp=tp)


def kernel(x, meta_tokens, norm1_g, w_in, q_norm_g, w_uq, w_iq, kv_norm_g, q_head_norm_g, k_head_norm_g,
           idx_k_norm_g, w_uv, w_branch_attn, w_pool, pool_scale, w_branch_pool, w_out, norm2_g, router_w,
           router_b, w_gate_up, b_gate_up, w_down, b_down):
    batch, seq, d = x.shape
    n_meta = meta_tokens.shape[0]
    depth = norm1_g.shape[0]
    qr, kvr, di = q_norm_g.shape[1], kv_norm_g.shape[1], idx_k_norm_g.shape[1]
    n_heads, v = w_uv.shape[1], w_uv.shape[3]
    idx_heads = w_iq.shape[2] // di
    pw = pool_scale.shape[1]
    assert depth == 1
    assert n_meta <= CHUNK and seq % Q_BLOCK == 0 and idx_heads <= LANES
    assert qr % kvr == 0 and (qr + kvr) % di == 0 and di % LANES == 0
    pad_front = Q_BLOCK - n_meta
    tp = Q_BLOCK + seq
    tk = -(-tp // KEY_TILE) * KEY_TILE
    k_sel = min(TOPK_MAX, seq // 4)
    attn_scale = kvr ** -0.5
    idx_scale = (di ** -0.5) * (idx_heads ** -0.5)

    head_rows = jnp.concatenate([jnp.zeros((pad_front, d), x.dtype), meta_tokens.astype(x.dtype)], axis=0)

    out = None
    for l in range(depth):
        n_small = qr + kvr + di
        o_pool = n_small + idx_heads
        o_gate = o_pool + pw
        w_small = jnp.pad(w_in[l][:, :o_pool], ((0, 0), (0, LANES - idx_heads))).astype(BF16)
        w_a = w_in[l][:, o_pool:o_gate].astype(BF16)
        w_g = w_in[l][:, o_gate:].astype(BF16)

        h, xn = _embed_norm(x, head_rows, norm1_g[l])
        small = _matmul(xn, w_small, F32, tn=_tile(w_small.shape[1], 1024, LANES), name="in_proj_latents")
        a_pool = _matmul(xn, w_a, BF16, tn=_tile(pw, 1024, LANES), name="in_proj_pool")
        gates = _matmul(xn, w_g, BF16, tn=_tile(2 * d, 1024, LANES), sigmoid=True, name="in_proj_gates")

        y_pool = _pool_mixer(a_pool, w_pool[l].astype(BF16), pool_scale[l], batch=batch, tp=tp,
                             pad_front=pad_front)

        cq, widx = _prep_q(small, q_norm_g[l], qr, n_small, idx_scale)
        lat, kk, ki = _prep_kv(small, kv_norm_g[l], k_head_norm_g[l], idx_k_norm_g[l],
                               batch=batch, tp=tp, tk=tk, qr=qr, kvr=kvr, di=di)
        q = _qproj(cq, w_uq[l].astype(BF16), q_head_norm_g[l], n_heads=n_heads, hd=kvr,
                   heads_per_tile=min(2, n_heads),
                   scale=attn_scale, norm=True, name="q_proj_headnorm")
        qi = _qproj(cq, w_iq[l].astype(BF16), jnp.ones((di,), F32), n_heads=idx_heads, hd=di,
                    heads_per_tile=min(8, idx_heads), scale=1.0, norm=False, name="indexer_q_proj")
        logit_bound = (BOUND_MARGIN * kvr * attn_scale * jnp.max(jnp.abs(q_head_norm_g[l]))
                       * jnp.max(jnp.abs(k_head_norm_g[l])))
        y_attn = _sparse_attention(logit_bound, q, qi, widx, ki, kk, lat, w_uv[l].astype(BF16), batch=batch,
                                   tp=tp, k_sel=k_sel, pad_front=pad_front)

        merged = _merge(y_pool, y_attn, w_branch_pool[l].astype(BF16), w_branch_attn[l].astype(BF16), gates)
        h2 = _out_proj(merged, w_out[l].astype(BF16), h)

        out = _moe(h2, norm2_g[l], router_w[l], router_b[l], w_gate_up[l], b_gate_up[l], w_down[l],
                   b_down[l], batch=batch, seq=seq, tp=tp)
    return out
```

```python
import functools

import jax
import jax.numpy as jnp
from jax import lax
from jax.experimental import pallas as pl
from jax.experimental.pallas import tpu as pltpu

F32 = jnp.float32
BF16 = jnp.bfloat16
I32 = jnp.int32

CHUNK = 64
POOL_WINDOWS = (2, 4, 8, 16)
TOPK_MAX = 256
TOP_K_EXPERTS = 4
SWIGLU_LIMIT = 7.0
SWIGLU_ALPHA = 1.702
EPS = 1e-6
NEG = -1e30

Q_BLOCK = 128
KEY_TILE = 512
HEADS_PER_STEP = 32
INDEXER_HEADS_PER_DOT = 8
HEADS_PER_CHAIN = 4
HALO = 16
EXPERT_TILE = 256
ROUTE_TILE = 256
LANES = 128
INT_MIN = -2 ** 31
MAX_EXP_SPAN = 80.0
BOUND_MARGIN = 1.02
VMEM_LIMIT = 56 * 1024 * 1024


def _tile(n, target, mult):
    best = None
    for t in range(mult, min(n, target) + 1, mult):
        if n % t == 0:
            best = t
    assert best is not None, (n, target, mult)
    return best


def _params(*sem):
    return pltpu.CompilerParams(dimension_semantics=sem, vmem_limit_bytes=VMEM_LIMIT)


def _embed_norm_kernel(x_ref, head_ref, g_ref, h_ref, xn_ref):
    rows = jnp.where(pl.program_id(1) == 0, head_ref[...], x_ref[0])
    h_ref[...] = rows
    ms = jnp.mean(rows * rows, axis=-1, keepdims=True)
    xn_ref[...] = (rows * lax.rsqrt(ms + EPS) * g_ref[...]).astype(xn_ref.dtype)


def _embed_norm(x, head_rows, g):
    batch, seq, d = x.shape
    nq = seq // Q_BLOCK + 1
    n = batch * nq * Q_BLOCK
    return pl.pallas_call(
        _embed_norm_kernel,
        grid=(batch, nq),
        in_specs=[pl.BlockSpec((1, Q_BLOCK, d), lambda b, i: (b, jnp.maximum(i - 1, 0), 0)),
                  pl.BlockSpec((Q_BLOCK, d), lambda b, i: (0, 0)),
                  pl.BlockSpec((1, d), lambda b, i: (0, 0))],
        out_specs=[pl.BlockSpec((Q_BLOCK, d), lambda b, i: (b * nq + i, 0)),
                   pl.BlockSpec((Q_BLOCK, d), lambda b, i: (b * nq + i, 0))],
        out_shape=[jax.ShapeDtypeStruct((n, d), x.dtype), jax.ShapeDtypeStruct((n, d), BF16)],
        name="embed_rmsnorm",
        compiler_params=_params("parallel", "arbitrary"),
    )(x, head_rows, g.reshape(1, d))


def _matmul_kernel(x_ref, w_ref, o_ref, *, sigmoid):
    acc = jnp.dot(x_ref[...], w_ref[...], preferred_element_type=F32)
    if sigmoid:
        acc = jax.nn.sigmoid(acc)
    o_ref[...] = acc.astype(o_ref.dtype)


def _matmul(x, w, out_dtype, *, tn, name, sigmoid=False):
    m, k = x.shape
    n = w.shape[1]
    tm = _tile(m, 768, 128)
    return pl.pallas_call(
        functools.partial(_matmul_kernel, sigmoid=sigmoid),
        grid=(m // tm, n // tn),
        in_specs=[pl.BlockSpec((tm, k), lambda i, j: (i, 0)), pl.BlockSpec((k, tn), lambda i, j: (0, j))],
        out_specs=pl.BlockSpec((tm, tn), lambda i, j: (i, j)),
        out_shape=jax.ShapeDtypeStruct((m, n), out_dtype),
        name=name,
        compiler_params=_params("parallel", "arbitrary"),
    )(x, w)


def _prep_q_kernel(cq_ref, wi_ref, g_ref, cq_o, wi_o, *, idx_scale):
    x = cq_ref[...]
    ms = jnp.mean(x * x, axis=-1, keepdims=True)
    cq_o[...] = (x * lax.rsqrt(ms + EPS) * g_ref[...]).astype(cq_o.dtype)
    wi_o[...] = wi_ref[...] * idx_scale


def _prep_q(small, q_norm_g, qr, wi_col, idx_scale):
    n = small.shape[0]
    tm = _tile(n, 768, 128)
    return pl.pallas_call(
        functools.partial(_prep_q_kernel, idx_scale=idx_scale),
        grid=(n // tm,),
        in_specs=[pl.BlockSpec((tm, qr), lambda i: (i, 0)),
                  pl.BlockSpec((tm, LANES), lambda i: (i, wi_col // LANES)),
                  pl.BlockSpec((1, qr), lambda i: (0, 0))],
        out_specs=[pl.BlockSpec((tm, qr), lambda i: (i, 0)), pl.BlockSpec((tm, LANES), lambda i: (i, 0))],
        out_shape=[jax.ShapeDtypeStruct((n, qr), BF16), jax.ShapeDtypeStruct((n, LANES), F32)],
        name="prep_query_latent",
        compiler_params=_params("parallel"),
    )(small, small, q_norm_g.reshape(1, qr))


def _prep_kv_kernel(ckv_ref, kidx_ref, gkv_ref, gkh_ref, gki_ref, lat_o, kk_o, ki_o, *, n_valid):
    i = pl.program_id(1)

    @pl.when(i < n_valid)
    def _():
        c = ckv_ref[...]
        lat = c * lax.rsqrt(jnp.mean(c * c, axis=-1, keepdims=True) + EPS) * gkv_ref[...]
        kk = lat * lax.rsqrt(jnp.mean(lat * lat, axis=-1, keepdims=True) + EPS) * gkh_ref[...]
        k = kidx_ref[...]
        ki = k * lax.rsqrt(jnp.mean(k * k, axis=-1, keepdims=True) + EPS) * gki_ref[...]
        lat_o[0] = lat.astype(lat_o.dtype)
        kk_o[0] = kk.astype(kk_o.dtype)
        ki_o[0] = ki.astype(ki_o.dtype)

    @pl.when(i >= n_valid)
    def _():
        lat_o[...] = jnp.zeros_like(lat_o)
        kk_o[...] = jnp.zeros_like(kk_o)
        ki_o[...] = jnp.zeros_like(ki_o)


def _prep_kv(small, gkv, gkh, gki, *, batch, tp, tk, qr, kvr, di):
    nq = tp // Q_BLOCK
    nk = tk // Q_BLOCK

    def row(b, i):
        return b * nq + jnp.minimum(i, nq - 1)

    return pl.pallas_call(
        functools.partial(_prep_kv_kernel, n_valid=nq),
        grid=(batch, nk),
        in_specs=[pl.BlockSpec((Q_BLOCK, kvr), lambda b, i: (row(b, i), qr // kvr)),
                  pl.BlockSpec((Q_BLOCK, di), lambda b, i: (row(b, i), (qr + kvr) // di)),
                  pl.BlockSpec((1, kvr), lambda b, i: (0, 0)),
                  pl.BlockSpec((1, kvr), lambda b, i: (0, 0)),
                  pl.BlockSpec((1, di), lambda b, i: (0, 0))],
        out_specs=[pl.BlockSpec((1, Q_BLOCK, kvr), lambda b, i: (b, i, 0)),
                   pl.BlockSpec((1, Q_BLOCK, kvr), lambda b, i: (b, i, 0)),
                   pl.BlockSpec((1, Q_BLOCK, di), lambda b, i: (b, i, 0))],
        out_shape=[jax.ShapeDtypeStruct((batch, tk, kvr), BF16),
                   jax.ShapeDtypeStruct((batch, tk, kvr), BF16),
                   jax.ShapeDtypeStruct((batch, tk, di), BF16)],
        name="prep_keys",
        compiler_params=_params("parallel", "arbitrary"),
    )(small, small, gkv.reshape(1, kvr), gkh.reshape(1, kvr), gki.reshape(1, di))


def _pool_kernel(a_ref, halo_ref, w_ref, sc_ref, o_ref, xs_ref, *, tp_tile, pad_front, pg):
    i = pl.program_id(1)
    t = i * tp_tile + lax.broadcasted_iota(I32, (tp_tile, 1), 0) - pad_front
    for g, win in enumerate(POOL_WINDOWS):
        cols = slice(g * pg, (g + 1) * pg)
        xs_ref[0:HALO, :] = halo_ref[0, :, cols].astype(F32)
        xs_ref[HALO:, :] = a_ref[0, :, cols].astype(F32)
        cur = xs_ref[pl.ds(HALO, tp_tile), :]
        acc = cur
        for k in range(1, win):
            acc = acc + xs_ref[pl.ds(HALO - k, tp_tile), :]
        cnt = jnp.clip(t + 1, 1, win).astype(F32)
        pooled = (acc / cnt - cur).astype(BF16)
        y = jnp.dot(pooled, w_ref[g], preferred_element_type=F32) * sc_ref[:, cols]
        o_ref[0, :, cols] = y.astype(o_ref.dtype)


def _pool_mixer(a, w_pool, pool_scale, *, batch, tp, pad_front):
    pw = a.shape[-1]
    n_groups, pg, _ = w_pool.shape
    tpt = _tile(tp, 1536, HALO)
    a3 = a.reshape(batch, tp, pw)
    halo_blocks = tpt // HALO
    out = pl.pallas_call(
        functools.partial(_pool_kernel, tp_tile=tpt, pad_front=pad_front, pg=pg),
        grid=(batch, tp // tpt),
        in_specs=[pl.BlockSpec((1, tpt, pw), lambda b, i: (b, i, 0)),
                  pl.BlockSpec((1, HALO, pw), lambda b, i: (b, jnp.maximum(i * halo_blocks - 1, 0), 0)),
                  pl.BlockSpec((n_groups, pg, pg), lambda b, i: (0, 0, 0)),
                  pl.BlockSpec((1, pw), lambda b, i: (0, 0))],
        out_specs=pl.BlockSpec((1, tpt, pw), lambda b, i: (b, i, 0)),
        out_shape=jax.ShapeDtypeStruct((batch, tp, pw), BF16),
        scratch_shapes=[pltpu.VMEM((tpt + HALO, pg), F32)],
        name="pool_mixer",
        compiler_params=_params("parallel", "arbitrary"),
    )(a3, a3, w_pool, pool_scale.reshape(1, pw))
    return out.reshape(batch * tp, pw)


def _qproj_kernel(x_ref, w_ref, g_ref, o_ref, *, scale, norm, heads, hd):
    acc = jnp.dot(x_ref[...], w_ref[...], preferred_element_type=F32)
    nblk = o_ref.shape[0]
    for hh in range(heads):
        a = acc[:, hh * hd:(hh + 1) * hd]
        if norm:
            a = a * lax.rsqrt(jnp.mean(a * a, axis=-1, keepdims=True) + EPS) * (g_ref[...] * scale)
        a = a.astype(o_ref.dtype)
        for r in range(nblk):
            o_ref[r, hh] = a[r * Q_BLOCK:(r + 1) * Q_BLOCK]


def _qproj(cq, w, g, *, n_heads, hd, heads_per_tile, scale, norm, name):
    n, r = cq.shape
    tm = _tile(n, 1536, Q_BLOCK)
    nblk = tm // Q_BLOCK
    tn = heads_per_tile * hd
    return pl.pallas_call(
        functools.partial(_qproj_kernel, scale=scale, norm=norm, heads=heads_per_tile, hd=hd),
        grid=(n // tm, n_heads // heads_per_tile),
        in_specs=[pl.BlockSpec((tm, r), lambda i, j: (i, 0)),
                  pl.BlockSpec((r, tn), lambda i, j: (0, j)),
                  pl.BlockSpec((1, hd), lambda i, j: (0, 0))],
        out_specs=pl.BlockSpec((nblk, heads_per_tile, Q_BLOCK, hd), lambda i, j: (i, j, 0, 0)),
        out_shape=jax.ShapeDtypeStruct((n // Q_BLOCK, n_heads, Q_BLOCK, hd), BF16),
        name=name,
        compiler_params=_params("parallel", "arbitrary"),
    )(cq, w, g.reshape(1, hd))


def _attn_kernel(shift_ref, q_ref, qi_ref, wi_ref, ki_ref, kk_ref, lat_ref, wuv_ref, o_ref,
                 bias_ref, keyp_ref, acc_ref, m_ref, l_ref, *, k_sel, pad_front, idx_heads):
    j = pl.program_id(1)
    hg = pl.program_id(2)
    shift = shift_ref[0]
    use_shift = shift_ref[1] > 0.5
    hps, qb, c = q_ref.shape[1], q_ref.shape[2], q_ref.shape[3]
    di = qi_ref.shape[3]
    v = wuv_ref.shape[2]
    n_t = (j * qb + qb + KEY_TILE - 1) // KEY_TILE
    nt_dims = (((1,), (1,)), ((), ()))

    @pl.when(hg == 0)
    def _select():
        tq = j * qb + lax.broadcasted_iota(I32, (qb, 1), 0)
        limit = ((jnp.maximum(tq, CHUNK) + CHUNK) // CHUNK) * CHUNK

        def score_tile(kt, carry):
            off = pl.multiple_of(kt * KEY_TILE, KEY_TILE)
            ki_t = ki_ref[0, pl.ds(off, KEY_TILE), :]
            part = jnp.zeros((qb, KEY_TILE), F32)
            hpi = min(INDEXER_HEADS_PER_DOT, idx_heads)
            for g in range(idx_heads // hpi):
                qg = qi_ref[0, g * hpi:(g + 1) * hpi].reshape(hpi * qb, di)
                s = lax.dot_general(qg, ki_t, nt_dims, preferred_element_type=F32)
                s = jnp.maximum(s, 0.0)
                for hh in range(hpi):
                    h = g * hpi + hh
                    part = part + s[hh * qb:(hh + 1) * qb] * wi_ref[:, h:h + 1]
            bits = pltpu.bitcast(part, I32)
            key = bits ^ ((bits >> 31) & 0x7FFFFFFF)
            s_idx = off + lax.broadcasted_iota(I32, (qb, KEY_TILE), 1)
            adm = (s_idx >= pad_front) & (s_idx < limit)
            keyp_ref[:, pl.ds(off, KEY_TILE)] = jnp.where(adm, key, INT_MIN)
            return carry

        lax.fori_loop(0, n_t, score_tile, 0)

        def count_ge(thr):
            def body(kt, cnt):
                off = pl.multiple_of(kt * KEY_TILE, KEY_TILE)
                ge = jnp.where(keyp_ref[:, pl.ds(off, KEY_TILE)] >= thr, 1.0, 0.0)
                for u in range(KEY_TILE // LANES):
                    cnt = cnt + ge[:, u * LANES:(u + 1) * LANES]
                return cnt
            cnt = lax.fori_loop(0, n_t, body, jnp.zeros((qb, LANES), F32))
            return jnp.sum(cnt, axis=1, keepdims=True)

        def bit_step(bi, thr):
            cand = thr + lax.shift_left(jnp.int32(1), 31 - bi)
            return jnp.where(count_ge(cand) >= k_sel, cand, thr)

        thr = lax.fori_loop(0, 32, bit_step, jnp.full((qb, 1), INT_MIN, I32))
        thr = jnp.maximum(thr, INT_MIN + 1)

        def bias_tile(kt, carry):
            off = pl.multiple_of(kt * KEY_TILE, KEY_TILE)
            sel = keyp_ref[:, pl.ds(off, KEY_TILE)] >= thr
            bias_ref[:, pl.ds(off, KEY_TILE)] = jnp.where(sel, -shift, NEG)
            return carry

        lax.fori_loop(0, n_t, bias_tile, 0)

        @pl.when(jnp.max(count_ge(thr)) > k_sel)
        def _ties():
            quota = k_sel - count_ge(thr + 1)
            upper = (lax.broadcasted_iota(I32, (LANES, LANES), 0)
                     < lax.broadcasted_iota(I32, (LANES, LANES), 1)).astype(BF16)

            def chunk(ci, seen):
                off = pl.multiple_of(ci * LANES, LANES)
                kp = keyp_ref[:, pl.ds(off, LANES)]
                tie = jnp.where(kp == thr, 1.0, 0.0)
                rank = jnp.dot(tie.astype(BF16), upper, preferred_element_type=F32) + seen
                sel = jnp.where(kp > thr, 1.0, tie * jnp.where(rank < quota, 1.0, 0.0))
                bias_ref[:, pl.ds(off, LANES)] = jnp.where(sel > 0.5, -shift, NEG)
                return seen + jnp.sum(tie, axis=1, keepdims=True)

            lax.fori_loop(0, n_t * (KEY_TILE // LANES), chunk, jnp.zeros((qb, 1), F32))

    l_ref[...] = jnp.zeros(l_ref.shape, F32)
    acc_ref[...] = jnp.zeros(acc_ref.shape, F32)
    hpc = min(HEADS_PER_CHAIN, hps)
    rc = hpc * qb
    k_slabs = KEY_TILE // LANES
    c_slabs = c // LANES

    def masked_logits(ci, off):
        q_c = q_ref[0, ci * hpc:(ci + 1) * hpc].reshape(rc, c)
        s = lax.dot_general(q_c, kk_ref[0, pl.ds(off, KEY_TILE), :], nt_dims, preferred_element_type=F32)
        s = (s.reshape(hpc, qb, KEY_TILE) + bias_ref[:, pl.ds(off, KEY_TILE)][None]).reshape(rc, KEY_TILE)
        return [s[:, u * LANES:(u + 1) * LANES] for u in range(k_slabs)]

    def project_out(inv_l):
        for hh in range(hps):
            rs = slice(hh * qb, (hh + 1) * qb)
            o = jnp.concatenate([acc_ref[rs, u * LANES:(u + 1) * LANES] * inv_l[rs] for u in range(c_slabs)],
                                axis=1)
            y = jnp.dot(o.astype(BF16), wuv_ref[hh], preferred_element_type=F32)
            o_ref[:, hh * v:(hh + 1) * v] = y.astype(o_ref.dtype)

    @pl.when(use_shift)
    def _static_shift():
        def kv_step(kt, carry):
            off = pl.multiple_of(kt * KEY_TILE, KEY_TILE)
            lat_t = lat_ref[0, pl.ds(off, KEY_TILE), :]
            for ci in range(hps // hpc):
                rs = slice(ci * rc, (ci + 1) * rc)
                ps = [jnp.exp(sl) for sl in masked_logits(ci, off)]
                psum = ps[0]
                for pu in ps[1:]:
                    psum = psum + pu
                l_ref[rs, :] = l_ref[rs, :] + psum
                pv = jnp.dot(jnp.concatenate(ps, axis=1).astype(BF16), lat_t, preferred_element_type=F32)
                for u in range(c_slabs):
                    cs = slice(u * LANES, (u + 1) * LANES)
                    acc_ref[rs, cs] = acc_ref[rs, cs] + pv[:, cs]
            return carry

        lax.fori_loop(0, n_t, kv_step, 0)
        project_out(jnp.broadcast_to(1.0 / jnp.sum(l_ref[...], axis=1, keepdims=True), l_ref.shape))

    @pl.when(jnp.logical_not(use_shift))
    def _online():
        m_ref[...] = jnp.full(m_ref.shape, -3e38, F32)

        def kv_step(kt, carry):
            off = pl.multiple_of(kt * KEY_TILE, KEY_TILE)
            lat_t = lat_ref[0, pl.ds(off, KEY_TILE), :]
            for ci in range(hps // hpc):
                rs = slice(ci * rc, (ci + 1) * rc)
                slabs = masked_logits(ci, off)
                mx = slabs[0]
                for sl in slabs[1:]:
                    mx = jnp.maximum(mx, sl)
                m_prev = m_ref[rs, :]
                m_new = jnp.maximum(m_prev, jnp.max(mx, axis=1, keepdims=True))
                alpha = jnp.exp(m_prev - m_new)
                ps = [jnp.exp(sl - m_new) for sl in slabs]
                psum = ps[0]
                for pu in ps[1:]:
                    psum = psum + pu
                l_ref[rs, :] = alpha * l_ref[rs, :] + jnp.sum(psum, axis=1, keepdims=True)
                pv = jnp.dot(jnp.concatenate(ps, axis=1).astype(BF16), lat_t, preferred_element_type=F32)
                for u in range(c_slabs):
                    cs = slice(u * LANES, (u + 1) * LANES)
                    acc_ref[rs, cs] = alpha * acc_ref[rs, cs] + pv[:, cs]
                m_ref[rs, :] = m_new
            return carry

        lax.fori_loop(0, n_t, kv_step, 0)
        project_out(1.0 / l_ref[...])


def _sparse_attention(logit_bound, q, qi, widx, ki, kk, lat, w_uv, *, batch, tp, k_sel, pad_front):
    nblk, n_heads, qb, c = q.shape
    usable = 2.0 * logit_bound <= MAX_EXP_SPAN
    shift_info = jnp.stack([jnp.where(usable, logit_bound, 0.0), usable.astype(F32)]).astype(F32)
    idx_heads, di = qi.shape[1], qi.shape[3]
    tk = ki.shape[1]
    v = w_uv.shape[2]
    nq = tp // qb
    hps = min(HEADS_PER_STEP, n_heads)
    rows = hps * qb
    return pl.pallas_call(
        functools.partial(_attn_kernel, k_sel=k_sel, pad_front=pad_front, idx_heads=idx_heads),
        grid=(batch, nq, n_heads // hps),
        in_specs=[pl.BlockSpec(memory_space=pltpu.SMEM),
                  pl.BlockSpec((1, hps, qb, c), lambda b, j, h: (b * nq + j, h, 0, 0)),
                  pl.BlockSpec((1, idx_heads, qb, di), lambda b, j, h: (b * nq + j, 0, 0, 0)),
                  pl.BlockSpec((qb, LANES), lambda b, j, h: (b * nq + j, 0)),
                  pl.BlockSpec((1, tk, di), lambda b, j, h: (b, 0, 0), pipeline_mode=pl.Buffered(1)),
                  pl.BlockSpec((1, tk, c), lambda b, j, h: (b, 0, 0), pipeline_mode=pl.Buffered(1)),
                  pl.BlockSpec((1, tk, c), lambda b, j, h: (b, 0, 0), pipeline_mode=pl.Buffered(1)),
                  pl.BlockSpec((hps, c, v), lambda b, j, h: (h, 0, 0))],
        out_specs=pl.BlockSpec((qb, hps * v), lambda b, j, h: (b * nq + j, h)),
        out_shape=jax.ShapeDtypeStruct((nblk * qb, n_heads * v), BF16),
        scratch_shapes=[pltpu.VMEM((qb, tk), F32), pltpu.VMEM((qb, tk), I32),
                        pltpu.VMEM((rows, c), F32), pltpu.VMEM((rows, LANES), F32),
                        pltpu.VMEM((rows, LANES), F32)],
        name="sparse_attention",
        compiler_params=_params("parallel", "arbitrary", "arbitrary"),
    )(shift_info, q, qi, widx, ki, kk, lat, w_uv)


def _merge_kernel(yp_ref, ya_ref, wp_ref, wa_ref, gp_ref, ga_ref, o_ref):
    pool = jnp.dot(yp_ref[...], wp_ref[...], preferred_element_type=F32)
    attn = jnp.dot(ya_ref[...], wa_ref[...], preferred_element_type=F32)
    o_ref[...] = (gp_ref[...].astype(F32) * pool + ga_ref[...].astype(F32) * attn).astype(o_ref.dtype)


def _merge(y_pool, y_attn, w_bp, w_ba, gates):
    n, pw = y_pool.shape
    aw = y_attn.shape[1]
    d = w_bp.shape[1]
    tm = _tile(n, 768, 128)
    tn = 512
    return pl.pallas_call(
        _merge_kernel,
        grid=(n // tm, d // tn),
        in_specs=[pl.BlockSpec((tm, pw), lambda i, j: (i, 0)),
                  pl.BlockSpec((tm, aw), lambda i, j: (i, 0)),
                  pl.BlockSpec((pw, tn), lambda i, j: (0, j)),
                  pl.BlockSpec((aw, tn), lambda i, j: (0, j)),
                  pl.BlockSpec((tm, tn), lambda i, j: (i, j)),
                  pl.BlockSpec((tm, tn), lambda i, j: (i, j + d // tn))],
        out_specs=pl.BlockSpec((tm, tn), lambda i, j: (i, j)),
        out_shape=jax.ShapeDtypeStruct((n, d), BF16),
        name="branch_merge",
        compiler_params=_params("parallel", "arbitrary"),
    )(y_pool, y_attn, w_bp, w_ba, gates, gates)


def _out_proj_kernel(x_ref, w_ref, r_ref, o_ref):
    o_ref[...] = r_ref[...] + jnp.dot(x_ref[...], w_ref[...], preferred_element_type=F32)


def _out_proj(merged, w_out, resid):
    n, k = merged.shape
    d = w_out.shape[1]
    tm = _tile(n, 768, 128)
    tn = _tile(d, 1024, LANES)
    return pl.pallas_call(
        _out_proj_kernel,
        grid=(n // tm, d // tn),
        in_specs=[pl.BlockSpec((tm, k), lambda i, j: (i, 0)),
                  pl.BlockSpec((k, tn), lambda i, j: (0, j)),
                  pl.BlockSpec((tm, tn), lambda i, j: (i, j))],
        out_specs=pl.BlockSpec((tm, tn), lambda i, j: (i, j)),
        out_shape=jax.ShapeDtypeStruct((n, d), F32),
        name="out_proj_residual",
        compiler_params=_params("parallel", "arbitrary"),
    )(merged, w_out, resid)


def _pack_halves(x):
    w = x.shape[1] // 2
    lo = pltpu.bitcast(x[:, :w].astype(BF16).astype(F32), I32)
    hi = pltpu.bitcast(x[:, w:].astype(BF16).astype(F32), I32)
    return ((lo >> 16) & 0xFFFF) | (hi & -65536)


def _unpack_halves(p):
    return pltpu.bitcast(p << 16, F32), pltpu.bitcast(p & -65536, F32)


def _router_kernel(h_ref, g_ref, rw_ref, rb_ref, xp_o, e_o, w_o, rank_o, cnt_o, carry_ref):
    i = pl.program_id(0)

    @pl.when(i == 0)
    def _():
        carry_ref[...] = jnp.zeros_like(carry_ref)

    x = h_ref[...]
    xn = x * lax.rsqrt(jnp.mean(x * x, axis=-1, keepdims=True) + EPS) * g_ref[...]
    xp_o[...] = _pack_halves(xn)
    logits = jnp.dot(xn.astype(BF16), rw_ref[...], preferred_element_type=F32) + rb_ref[...]
    tm = logits.shape[0]
    lane = lax.broadcasted_iota(I32, (tm, LANES), 1).astype(F32)
    vals, hots = [], []
    cur = logits
    for k in range(TOP_K_EXPERTS):
        m = jnp.max(cur, axis=1, keepdims=True)
        idx = jnp.min(jnp.where(cur == m, lane, float(LANES)), axis=1, keepdims=True)
        hot = lane == idx
        vals.append(m)
        hots.append(hot)
        e_o[:, k:k + 1] = idx.astype(I32)
        cur = jnp.where(hot, -jnp.inf, cur)
    exps = [jnp.exp(vk - vals[0]) for vk in vals]
    denom = exps[0]
    for ek in exps[1:]:
        denom = denom + ek
    for k in range(TOP_K_EXPERTS):
        w_o[:, k:k + 1] = exps[k] / denom
    onehot = jnp.zeros((tm, LANES), F32)
    for hot in hots:
        onehot = onehot + jnp.where(hot, 1.0, 0.0)
    lower = (lax.broadcasted_iota(I32, (tm, tm), 0) > lax.broadcasted_iota(I32, (tm, tm), 1)).astype(BF16)
    before = jnp.dot(lower, onehot.astype(BF16), preferred_element_type=F32) + carry_ref[...]
    for k in range(TOP_K_EXPERTS):
        rank_o[:, k:k + 1] = jnp.sum(jnp.where(hots[k], before, 0.0), axis=1, keepdims=True).astype(I32)
    carry_ref[...] = carry_ref[...] + jnp.sum(onehot, axis=0, keepdims=True)
    cnt_o[...] = carry_ref[...]


def _router(h2, norm2_g, rw, rb):
    n, d = h2.shape
    tm = ROUTE_TILE
    kk = TOP_K_EXPERTS
    return pl.pallas_call(
        _router_kernel,
        grid=(n // tm,),
        in_specs=[pl.BlockSpec((tm, d), lambda i: (i, 0)),
                  pl.BlockSpec((1, d), lambda i: (0, 0)),
                  pl.BlockSpec((d, LANES), lambda i: (0, 0)),
                  pl.BlockSpec((1, LANES), lambda i: (0, 0))],
        out_specs=[pl.BlockSpec((tm, d // 2), lambda i: (i, 0)),
                   pl.BlockSpec((tm, kk), lambda i: (i, 0)),
                   pl.BlockSpec((tm, kk), lambda i: (i, 0)),
                   pl.BlockSpec((tm, kk), lambda i: (i, 0)),
                   pl.BlockSpec((1, LANES), lambda i: (0, 0))],
        out_shape=[jax.ShapeDtypeStruct((n, d // 2), I32),
                   jax.ShapeDtypeStruct((n, kk), I32),
                   jax.ShapeDtypeStruct((n, kk), F32),
                   jax.ShapeDtypeStruct((n, kk), I32),
                   jax.ShapeDtypeStruct((1, LANES), F32)],
        scratch_shapes=[pltpu.VMEM((1, LANES), F32)],
        name="moe_router",
        compiler_params=_params("arbitrary"),
    )(h2, norm2_g.reshape(1, d), rw, rb)


def _row_copy(src_ref, src_row, dst_ref, dst_row, sem):
    return pltpu.make_async_copy(src_ref.at[pl.ds(src_row, 1), :], dst_ref.at[pl.ds(dst_row, 1), :], sem)


def _dispatch_kernel(dest_ref, pad_start_ref, pad_count_ref, xp_ref, xs_ref, zero_ref, sem, zero_sem, tail_sem):
    tm = xp_ref.shape[0]
    base = pl.program_id(0) * tm * TOP_K_EXPERTS

    def issue(r, carry):
        for k in range(TOP_K_EXPERTS):
            _row_copy(xp_ref, r, xs_ref, dest_ref[base + r * TOP_K_EXPERTS + k], sem).start(priority=k % 2)
        return carry

    def drain(r, carry):
        for k in range(TOP_K_EXPERTS):
            _row_copy(xp_ref, 0, xs_ref, 0, sem).wait()
        return carry

    lax.fori_loop(0, tm, issue, 0)
    lax.fori_loop(0, tm, drain, 0)

    @pl.when(pl.program_id(0) == pl.num_programs(0) - 1)
    def _():
        zero_ref[...] = jnp.zeros_like(zero_ref)

        def per_expert(e, total):
            def zero_row(r, carry):
                _row_copy(zero_ref, 0, xs_ref, pad_start_ref[e] + r, zero_sem).start()
                return carry
            lax.fori_loop(0, pad_count_ref[e], zero_row, 0)
            return total + pad_count_ref[e]

        n_exp = pad_start_ref.shape[0]
        total = lax.fori_loop(0, n_exp, per_expert, 0)

        def zero_done(r, carry):
            _row_copy(zero_ref, 0, xs_ref, 0, zero_sem).wait()
            return carry

        lax.fori_loop(0, total, zero_done, 0)

        tile = zero_ref.shape[0]
        first = (pad_start_ref[n_exp - 1] + pad_count_ref[n_exp - 1]) // tile
        n_tail = xs_ref.shape[0] // tile - first

        def tile_copy(i):
            row0 = pl.multiple_of((first + i) * tile, tile)
            return pltpu.make_async_copy(zero_ref, xs_ref.at[pl.ds(row0, tile), :], tail_sem)

        def zero_tile(i, carry):
            tile_copy(i).start()
            return carry

        def tile_done(i, carry):
            tile_copy(i).wait()
            return carry

        lax.fori_loop(0, n_tail, zero_tile, 0)
        lax.fori_loop(0, n_tail, tile_done, 0)


def _dispatch(dest_flat, pad_start, pad_count, xp, n_slots):
    n, w = xp.shape
    tm = ROUTE_TILE
    return pl.pallas_call(
        _dispatch_kernel,
        grid_spec=pltpu.PrefetchScalarGridSpec(
            num_scalar_prefetch=3,
            grid=(n // tm,),
            in_specs=[pl.BlockSpec((tm, w), lambda i, *_: (i, 0))],
            out_specs=pl.BlockSpec(memory_space=pl.ANY),
            scratch_shapes=[pltpu.VMEM((EXPERT_TILE, w), I32), pltpu.SemaphoreType.DMA(()),
                            pltpu.SemaphoreType.DMA(()), pltpu.SemaphoreType.DMA(())]),
        out_shape=jax.ShapeDtypeStruct((n_slots, w), I32),
        name="moe_dispatch",
        compiler_params=_params("arbitrary"),
    )(dest_flat, pad_start, pad_count, xp)


def _deinterleave_kernel(w_ref, o_ref):
    blk = 2 * LANES
    dst = lax.broadcasted_iota(I32, (blk, blk), 1)
    src = jnp.where(dst < LANES, 2 * dst, 2 * (dst - LANES) + 1)
    perm = (lax.broadcasted_iota(I32, (blk, blk), 0) == src).astype(BF16)
    for b in range(w_ref.shape[2] // blk):
        cols = slice(b * blk, (b + 1) * blk)
        o_ref[0, :, cols] = jnp.dot(w_ref[0, :, cols].astype(BF16), perm,
                                    preferred_element_type=F32).astype(o_ref.dtype)


def _deinterleave_gate_up(w_gate_up):
    n_exp, d, de2 = w_gate_up.shape
    assert de2 % (2 * LANES) == 0
    tk = _tile(d, 512, 16)
    return pl.pallas_call(
        _deinterleave_kernel,
        grid=(n_exp, d // tk),
        in_specs=[pl.BlockSpec((1, tk, de2), lambda e, i: (e, i, 0))],
        out_specs=pl.BlockSpec((1, tk, de2), lambda e, i: (e, i, 0)),
        out_shape=jax.ShapeDtypeStruct((n_exp, d, de2), BF16),
        name="moe_weight_regroup",
        compiler_params=_params("parallel", "arbitrary"),
    )(w_gate_up)


def _expert_up_kernel(te_ref, tv_ref, xs_ref, w_ref, b_ref, o_ref):
    t = pl.program_id(0)

    @pl.when(tv_ref[t] > 0)
    def _():
        lo, hi = _unpack_halves(xs_ref[...])
        x = jnp.concatenate([lo.astype(BF16), hi.astype(BF16)], axis=1)
        gu = jnp.dot(x, w_ref[0], preferred_element_type=F32) + b_ref[0]
        for blk in range(gu.shape[1] // (2 * LANES)):
            g0 = blk * 2 * LANES
            gate = jnp.minimum(gu[:, g0:g0 + LANES], SWIGLU_LIMIT)
            up = jnp.clip(gu[:, g0 + LANES:g0 + 2 * LANES], -SWIGLU_LIMIT, SWIGLU_LIMIT)
            act = (up + 1.0) * gate * jax.nn.sigmoid(SWIGLU_ALPHA * gate)
            o_ref[:, blk * LANES:(blk + 1) * LANES] = act.astype(o_ref.dtype)

    @pl.when(tv_ref[t] == 0)
    def _():
        o_ref[...] = jnp.zeros_like(o_ref)


def _expert_up(tile_expert, tile_valid, xs, w_gu, b_gu):
    n_slots, w = xs.shape
    n_exp, d, de2 = w_gu.shape
    tm = EXPERT_TILE
    return pl.pallas_call(
        _expert_up_kernel,
        grid_spec=pltpu.PrefetchScalarGridSpec(
            num_scalar_prefetch=2,
            grid=(n_slots // tm,),
            in_specs=[pl.BlockSpec((tm, w), lambda t, te, tv: (t, 0)),
                      pl.BlockSpec((1, d, de2), lambda t, te, tv: (te[t], 0, 0)),
                      pl.BlockSpec((1, 1, de2), lambda t, te, tv: (te[t], 0, 0))],
            out_specs=pl.BlockSpec((tm, de2 // 2), lambda t, te, tv: (t, 0))),
        out_shape=jax.ShapeDtypeStruct((n_slots, de2 // 2), BF16),
        name="moe_expert_up",
        compiler_params=_params("arbitrary"),
    )(tile_expert, tile_valid, xs, w_gu, b_gu)


def _expert_down_kernel(te_ref, tv_ref, a_ref, w_ref, b_ref, o_ref, wbf_ref):
    t = pl.program_id(0)

    @pl.when(jnp.logical_or(t == 0, te_ref[t] != te_ref[jnp.maximum(t - 1, 0)]))
    def _():
        wbf_ref[...] = w_ref[0].astype(BF16)

    @pl.when(tv_ref[t] > 0)
    def _():
        y = jnp.dot(a_ref[...], wbf_ref[...], preferred_element_type=F32) + b_ref[0]
        o_ref[...] = _pack_halves(y)

    @pl.when(tv_ref[t] == 0)
    def _():
        o_ref[...] = jnp.zeros_like(o_ref)


def _expert_down(tile_expert, tile_valid, act, w_down, b_down):
    n_slots, de = act.shape
    n_exp, _, d = w_down.shape
    tm = EXPERT_TILE
    return pl.pallas_call(
        _expert_down_kernel,
        grid_spec=pltpu.PrefetchScalarGridSpec(
            num_scalar_prefetch=2,
            grid=(n_slots // tm,),
            in_specs=[pl.BlockSpec((tm, de), lambda t, te, tv: (t, 0)),
                      pl.BlockSpec((1, de, d), lambda t, te, tv: (te[t], 0, 0)),
                      pl.BlockSpec((1, 1, d), lambda t, te, tv: (te[t], 0, 0))],
            out_specs=pl.BlockSpec((tm, d // 2), lambda t, te, tv: (t, 0)),
            scratch_shapes=[pltpu.VMEM((de, d), BF16)]),
        out_shape=jax.ShapeDtypeStruct((n_slots, d // 2), I32),
        name="moe_expert_down",
        compiler_params=_params("arbitrary"),
    )(tile_expert, tile_valid, act, w_down, b_down)


def _combine_kernel(dest_ref, w_ref, h_ref, ye_ref, o_ref, buf_ref, sems, *, nq):
    b = pl.program_id(0)
    i = pl.program_id(1)
    n_i = pl.num_programs(1)
    n_steps = pl.num_programs(0) * n_i
    tm = h_ref.shape[0]
    half = buf_ref.shape[3]
    step = b * n_i + i
    slot = step % 2

    def start_rows(bb, ii, sl):
        base = ((bb * nq + 1 + ii) * tm) * TOP_K_EXPERTS

        def issue(r, carry):
            for k in range(TOP_K_EXPERTS):
                _row_copy(ye_ref, dest_ref[base + r * TOP_K_EXPERTS + k], buf_ref.at[sl, k], r,
                          sems.at[sl]).start(priority=k % 2)
            return carry

        lax.fori_loop(0, tm, issue, 0)

    @pl.when(step == 0)
    def _():
        start_rows(b, i, slot)

    @pl.when(step + 1 < n_steps)
    def _():
        wrap = i + 1 == n_i
        start_rows(jnp.where(wrap, b + 1, b), jnp.where(wrap, 0, i + 1), 1 - slot)

    def drain(r, carry):
        for k in range(TOP_K_EXPERTS):
            _row_copy(ye_ref, 0, buf_ref.at[slot, k], 0, sems.at[slot]).wait()
        return carry

    lax.fori_loop(0, tm, drain, 0)

    lo_sum = h_ref[:, :half]
    hi_sum = h_ref[:, half:]
    for k in range(TOP_K_EXPERTS):
        lo, hi = _unpack_halves(buf_ref[slot, k])
        wk = w_ref[:, k:k + 1]
        lo_sum = lo_sum + wk * lo
        hi_sum = hi_sum + wk * hi
    o_ref[0, :, :half] = lo_sum
    o_ref[0, :, half:] = hi_sum


def _combine(dest_flat, sel_w, h2, ye, *, batch, seq, tp):
    d = h2.shape[1]
    tm = Q_BLOCK
    nq = tp // tm
    return pl.pallas_call(
        functools.partial(_combine_kernel, nq=nq),
        grid_spec=pltpu.PrefetchScalarGridSpec(
            num_scalar_prefetch=1,
            grid=(batch, seq // tm),
            in_specs=[pl.BlockSpec((tm, TOP_K_EXPERTS), lambda b, i, dest: (b * nq + 1 + i, 0)),
                      pl.BlockSpec((tm, d), lambda b, i, dest: (b * nq + 1 + i, 0)),
                      pl.BlockSpec(memory_space=pl.ANY)],
            out_specs=pl.BlockSpec((1, tm, d), lambda b, i, dest: (b, i, 0)),
            scratch_shapes=[pltpu.VMEM((2, TOP_K_EXPERTS, tm, d // 2), I32), pltpu.SemaphoreType.DMA((2,))]),
        out_shape=jax.ShapeDtypeStruct((batch, seq, d), F32),
        name="moe_combine",
        compiler_params=_params("arbitrary", "arbitrary"),
    )(dest_flat, sel_w, h2, ye)


def _moe(h2, norm2_g, router_w, router_b, w_gate_up, b_gate_up, w_down, b_down, *, batch, seq, tp):
    n, d = h2.shape
    n_exp = router_w.shape[1]
    de = w_down.shape[1]
    assert n_exp <= LANES and n % ROUTE_TILE == 0
    rw = jnp.pad(router_w, ((0, 0), (0, LANES - n_exp))).astype(BF16)
    rb = jnp.pad(router_b, (0, LANES - n_exp), constant_values=NEG).reshape(1, LANES)
    xp, sel_e, sel_w, rank, counts = _router(h2, norm2_g, rw, rb)

    cnt = counts[0, :n_exp].astype(I32)
    padded = ((cnt + EXPERT_TILE - 1) // EXPERT_TILE) * EXPERT_TILE
    e_ids = jnp.arange(n_exp, dtype=I32)
    ends = jnp.sum(jnp.where(e_ids[:, None] <= e_ids[None, :], padded[:, None], 0), axis=0)
    starts = ends - padded
    total = ends[n_exp - 1]
    dest_flat = (starts[sel_e] + rank).reshape(-1)
    n_tiles = (n * TOP_K_EXPERTS) // EXPERT_TILE + n_exp
    tile_start = jnp.arange(n_tiles, dtype=I32) * EXPERT_TILE
    tile_valid = (tile_start < total).astype(I32)
    last_expert = jnp.sum((ends <= total - 1).astype(I32))
    tile_expert = jnp.minimum(jnp.sum((ends[None, :] <= tile_start[:, None]).astype(I32), axis=1), last_expert)

    w_gu = _deinterleave_gate_up(w_gate_up)
    nb = (2 * de) // (2 * LANES)
    b_gu = b_gate_up.reshape(n_exp, nb, LANES, 2).transpose(0, 1, 3, 2).reshape(n_exp, 1, 2 * de)

    xs = _dispatch(dest_flat, starts + cnt, padded - cnt, xp, n_tiles * EXPERT_TILE)
    act = _expert_up(tile_expert, tile_valid, xs, w_gu, b_gu)
    ye = _expert_down(tile_expert, tile_valid, act, w_down, b_down.reshape(n_exp, 1, d))
    return _combine(dest_flat, sel_w, h2, ye, batch=batch, seq=seq, tp=tp)


def kernel(x, meta_tokens, norm1_g, w_in, q_norm_g, w_uq, w_iq, kv_norm_g, q_head_norm_g, k_head_norm_g,
           idx_k_norm_g, w_uv, w_branch_attn, w_pool, pool_scale, w_branch_pool, w_out, norm2_g, router_w,
           router_b, w_gate_up, b_gate_up, w_down, b_down):
    batch, seq, d = x.shape
    n_meta = meta_tokens.shape[0]
    depth = norm1_g.shape[0]
    qr, kvr, di = q_norm_g.shape[1], kv_norm_g.shape[1], idx_k_norm_g.shape[1]
    n_heads, v = w_uv.shape[1], w_uv.shape[3]
    idx_heads = w_iq.shape[2] // di
    pw = pool_scale.shape[1]
    assert depth == 1
    assert n_meta <= CHUNK and seq % Q_BLOCK == 0 and idx_heads <= LANES
    assert qr % kvr == 0 and (qr + kvr) % di == 0 and di % LANES == 0
    pad_front = Q_BLOCK - n_meta
    tp = Q_BLOCK + seq
    tk = -(-tp // KEY_TILE) * KEY_TILE
    k_sel = min(TOPK_MAX, seq // 4)
    attn_scale = kvr ** -0.5
    idx_scale = (di ** -0.5) * (idx_heads ** -0.5)

    head_rows = jnp.concatenate([jnp.zeros((pad_front, d), x.dtype), meta_tokens.astype(x.dtype)], axis=0)

    out = None
    for l in range(depth):
        n_small = qr + kvr + di
        o_pool = n_small + idx_heads
        o_gate = o_pool + pw
        w_small = jnp.pad(w_in[l][:, :o_pool], ((0, 0), (0, LANES - idx_heads))).astype(BF16)
        w_a = w_in[l][:, o_pool:o_gate].astype(BF16)
        w_g = w_in[l][:, o_gate:].astype(BF16)

        h, xn = _embed_norm(x, head_rows, norm1_g[l])
        small = _matmul(xn, w_small, F32, tn=_tile(w_small.shape[1], 1024, LANES), name="in_proj_latents")
        a_pool = _matmul(xn, w_a, BF16, tn=_tile(pw, 1024, LANES), name="in_proj_pool")
        gates = _matmul(xn, w_g, BF16, tn=_tile(2 * d, 1024, LANES), sigmoid=True, name="in_proj_gates")

        y_pool = _pool_mixer(a_pool, w_pool[l].astype(BF16), pool_scale[l], batch=batch, tp=tp,
                             pad_front=pad_front)

        cq, widx = _prep_q(small, q_norm_g[l], qr, n_small, idx_scale)
        lat, kk, ki = _prep_kv(small, kv_norm_g[l], k_head_norm_g[l], idx_k_norm_g[l],
                               batch=batch, tp=tp, tk=tk, qr=qr, kvr=kvr, di=di)
        q = _qproj(cq, w_uq[l].astype(BF16), q_head_norm_g[l], n_heads=n_heads, hd=kvr,
                   heads_per_tile=min(2, n_heads),
                   scale=attn_scale, norm=True, name="q_proj_headnorm")
        qi = _qproj(cq, w_iq[l].astype(BF16), jnp.ones((di,), F32), n_heads=idx_heads, hd=di,
                    heads_per_tile=min(8, idx_heads), scale=1.0, norm=False, name="indexer_q_proj")
        logit_bound = (BOUND_MARGIN * kvr * attn_scale * jnp.max(jnp.abs(q_head_norm_g[l]))
                       * jnp.max(jnp.abs(k_head_norm_g[l])))
        y_attn = _sparse_attention(logit_bound, q, qi, widx, ki, kk, lat, w_uv[l].astype(BF16), batch=batch,
                                   tp=tp, k_sel=k_sel, pad_front=pad_front)

        merged = _merge(y_pool, y_attn, w_branch_pool[l].astype(BF16), w_branch_attn[l].astype(BF16), gates)
        h2 = _out_proj(merged, w_out[l].astype(BF16), h)

        out = _moe(h2, norm2_g[l], router_w[l], router_b[l], w_gate_up[l], b_gate_up[l], w_down[l],
                   b_down[l], batch=batch, seq=seq, tp=tp)
    return out
```

```python
import functools

import jax
import jax.numpy as jnp
from jax import lax
from jax.experimental import pallas as pl
from jax.experimental.pallas import tpu as pltpu

F32 = jnp.float32
BF16 = jnp.bfloat16
I32 = jnp.int32

CHUNK = 64
POOL_WINDOWS = (2, 4, 8, 16)
TOPK_MAX = 256
TOP_K_EXPERTS = 4
SWIGLU_LIMIT = 7.0
SWIGLU_ALPHA = 1.702
EPS = 1e-6
NEG = -1e30

Q_BLOCK = 128
KEY_TILE = 512
HEADS_PER_STEP = 32
INDEXER_HEADS_PER_DOT = 8
HEADS_PER_CHAIN = 4
HALO = 16
EXPERT_TILE = 256
ROUTE_TILE = 256
LANES = 128
INT_MIN = -2 ** 31
MAX_EXP_SPAN = 80.0
BOUND_MARGIN = 1.02
VMEM_LIMIT = 56 * 1024 * 1024


def _tile(n, target, mult):
    best = None
    for t in range(mult, min(n, target) + 1, mult):
        if n % t == 0:
            best = t
    assert best is not None, (n, target, mult)
    return best


def _params(*sem):
    return pltpu.CompilerParams(dimension_semantics=sem, vmem_limit_bytes=VMEM_LIMIT)


def _embed_norm_kernel(x_ref, head_ref, g_ref, h_ref, xn_ref):
    rows = jnp.where(pl.program_id(1) == 0, head_ref[...], x_ref[0])
    h_ref[...] = rows
    ms = jnp.mean(rows * rows, axis=-1, keepdims=True)
    xn_ref[...] = (rows * lax.rsqrt(ms + EPS) * g_ref[...]).astype(xn_ref.dtype)


def _embed_norm(x, head_rows, g):
    batch, seq, d = x.shape
    nq = seq // Q_BLOCK + 1
    n = batch * nq * Q_BLOCK
    return pl.pallas_call(
        _embed_norm_kernel,
        grid=(batch, nq),
        in_specs=[pl.BlockSpec((1, Q_BLOCK, d), lambda b, i: (b, jnp.maximum(i - 1, 0), 0)),
                  pl.BlockSpec((Q_BLOCK, d), lambda b, i: (0, 0)),
                  pl.BlockSpec((1, d), lambda b, i: (0, 0))],
        out_specs=[pl.BlockSpec((Q_BLOCK, d), lambda b, i: (b * nq + i, 0)),
                   pl.BlockSpec((Q_BLOCK, d), lambda b, i: (b * nq + i, 0))],
        out_shape=[jax.ShapeDtypeStruct((n, d), x.dtype), jax.ShapeDtypeStruct((n, d), BF16)],
        name="embed_rmsnorm",
        compiler_params=_params("parallel", "arbitrary"),
    )(x, head_rows, g.reshape(1, d))


def _matmul_kernel(x_ref, w_ref, o_ref, *, sigmoid):
    acc = jnp.dot(x_ref[...], w_ref[...], preferred_element_type=F32)
    if sigmoid:
        acc = jax.nn.sigmoid(acc)
    o_ref[...] = acc.astype(o_ref.dtype)


def _matmul(x, w, out_dtype, *, tn, name, sigmoid=False):
    m, k = x.shape
    n = w.shape[1]
    tm = _tile(m, 768, 128)
    return pl.pallas_call(
        functools.partial(_matmul_kernel, sigmoid=sigmoid),
        grid=(m // tm, n // tn),
        in_specs=[pl.BlockSpec((tm, k), lambda i, j: (i, 0)), pl.BlockSpec((k, tn), lambda i, j: (0, j))],
        out_specs=pl.BlockSpec((tm, tn), lambda i, j: (i, j)),
        out_shape=jax.ShapeDtypeStruct((m, n), out_dtype),
        name=name,
        compiler_params=_params("parallel", "arbitrary"),
    )(x, w)


def _prep_q_kernel(cq_ref, wi_ref, g_ref, cq_o, wi_o, *, idx_scale):
    x = cq_ref[...]
    ms = jnp.mean(x * x, axis=-1, keepdims=True)
    cq_o[...] = (x * lax.rsqrt(ms + EPS) * g_ref[...]).astype(cq_o.dtype)
    wi_o[...] = wi_ref[...] * idx_scale


def _prep_q(small, q_norm_g, qr, wi_col, idx_scale):
    n = small.shape[0]
    tm = _tile(n, 768, 128)
    return pl.pallas_call(
        functools.partial(_prep_q_kernel, idx_scale=idx_scale),
        grid=(n // tm,),
        in_specs=[pl.BlockSpec((tm, qr), lambda i: (i, 0)),
                  pl.BlockSpec((tm, LANES), lambda i: (i, wi_col // LANES)),
                  pl.BlockSpec((1, qr), lambda i: (0, 0))],
        out_specs=[pl.BlockSpec((tm, qr), lambda i: (i, 0)), pl.BlockSpec((tm, LANES), lambda i: (i, 0))],
        out_shape=[jax.ShapeDtypeStruct((n, qr), BF16), jax.ShapeDtypeStruct((n, LANES), F32)],
        name="prep_query_latent",
        compiler_params=_params("parallel"),
    )(small, small, q_norm_g.reshape(1, qr))


def _prep_kv_kernel(ckv_ref, kidx_ref, gkv_ref, gkh_ref, gki_ref, lat_o, kk_o, ki_o, *, n_valid):
    i = pl.program_id(1)

    @pl.when(i < n_valid)
    def _():
        c = ckv_ref[...]
        lat = c * lax.rsqrt(jnp.mean(c * c, axis=-1, keepdims=True) + EPS) * gkv_ref[...]
        kk = lat * lax.rsqrt(jnp.mean(lat * lat, axis=-1, keepdims=True) + EPS) * gkh_ref[...]
        k = kidx_ref[...]
        ki = k * lax.rsqrt(jnp.mean(k * k, axis=-1, keepdims=True) + EPS) * gki_ref[...]
        lat_o[0] = lat.astype(lat_o.dtype)
        kk_o[0] = kk.astype(kk_o.dtype)
        ki_o[0] = ki.astype(ki_o.dtype)

    @pl.when(i >= n_valid)
    def _():
        lat_o[...] = jnp.zeros_like(lat_o)
        kk_o[...] = jnp.zeros_like(kk_o)
        ki_o[...] = jnp.zeros_like(ki_o)


def _prep_kv(small, gkv, gkh, gki, *, batch, tp, tk, qr, kvr, di):
    nq = tp // Q_BLOCK
    nk = tk // Q_BLOCK

    def row(b, i):
        return b * nq + jnp.minimum(i, nq - 1)

    return pl.pallas_call(
        functools.partial(_prep_kv_kernel, n_valid=nq),
        grid=(batch, nk),
        in_specs=[pl.BlockSpec((Q_BLOCK, kvr), lambda b, i: (row(b, i), qr // kvr)),
                  pl.BlockSpec((Q_BLOCK, di), lambda b, i: (row(b, i), (qr + kvr) // di)),
                  pl.BlockSpec((1, kvr), lambda b, i: (0, 0)),
                  pl.BlockSpec((1, kvr), lambda b, i: (0, 0)),
                  pl.BlockSpec((1, di), lambda b, i: (0, 0))],
        out_specs=[pl.BlockSpec((1, Q_BLOCK, kvr), lambda b, i: (b, i, 0)),
                   pl.BlockSpec((1, Q_BLOCK, kvr), lambda b, i: (b, i, 0)),
                   pl.BlockSpec((1, Q_BLOCK, di), lambda b, i: (b, i, 0))],
        out_shape=[jax.ShapeDtypeStruct((batch, tk, kvr), BF16),
                   jax.ShapeDtypeStruct((batch, tk, kvr), BF16),
                   jax.ShapeDtypeStruct((batch, tk, di), BF16)],
        name="prep_keys",
        compiler_params=_params("parallel", "arbitrary"),
    )(small, small, gkv.reshape(1, kvr), gkh.reshape(1, kvr), gki.reshape(1, di))


def _pool_kernel(a_ref, halo_ref, w_ref, sc_ref, o_ref, xs_ref, *, tp_tile, pad_front, pg):
    i = pl.program_id(1)
    t = i * tp_tile + lax.broadcasted_iota(I32, (tp_tile, 1), 0) - pad_front
    for g, win in enumerate(POOL_WINDOWS):
        cols = slice(g * pg, (g + 1) * pg)
        xs_ref[0:HALO, :] = halo_ref[0, :, cols].astype(F32)
        xs_ref[HALO:, :] = a_ref[0, :, cols].astype(F32)
        cur = xs_ref[pl.ds(HALO, tp_tile), :]
        acc = cur
        for k in range(1, win):
            acc = acc + xs_ref[pl.ds(HALO - k, tp_tile), :]
        cnt = jnp.clip(t + 1, 1, win).astype(F32)
        pooled = (acc / cnt - cur).astype(BF16)
        y = jnp.dot(pooled, w_ref[g], preferred_element_type=F32) * sc_ref[:, cols]
        o_ref[0, :, cols] = y.astype(o_ref.dtype)


def _pool_mixer(a, w_pool, pool_scale, *, batch, tp, pad_front):
    pw = a.shape[-1]
    n_groups, pg, _ = w_pool.shape
    tpt = _tile(tp, 1536, HALO)
    a3 = a.reshape(batch, tp, pw)
    halo_blocks = tpt // HALO
    out = pl.pallas_call(
        functools.partial(_pool_kernel, tp_tile=tpt, pad_front=pad_front, pg=pg),
        grid=(batch, tp // tpt),
        in_specs=[pl.BlockSpec((1, tpt, pw), lambda b, i: (b, i, 0)),
                  pl.BlockSpec((1, HALO, pw), lambda b, i: (b, jnp.maximum(i * halo_blocks - 1, 0), 0)),
                  pl.BlockSpec((n_groups, pg, pg), lambda b, i: (0, 0, 0)),
                  pl.BlockSpec((1, pw), lambda b, i: (0, 0))],
        out_specs=pl.BlockSpec((1, tpt, pw), lambda b, i: (b, i, 0)),
        out_shape=jax.ShapeDtypeStruct((batch, tp, pw), BF16),
        scratch_shapes=[pltpu.VMEM((tpt + HALO, pg), F32)],
        name="pool_mixer",
        compiler_params=_params("parallel", "arbitrary"),
    )(a3, a3, w_pool, pool_scale.reshape(1, pw))
    return out.reshape(batch * tp, pw)


def _qproj_kernel(x_ref, w_ref, g_ref, o_ref, *, scale, norm, heads, hd):
    acc = jnp.dot(x_ref[...], w_ref[...], preferred_element_type=F32)
    nblk = o_ref.shape[0]
    for hh in range(heads):
        a = acc[:, hh * hd:(hh + 1) * hd]
        if norm:
            a = a * lax.rsqrt(jnp.mean(a * a, axis=-1, keepdims=True) + EPS) * (g_ref[...] * scale)
        a = a.astype(o_ref.dtype)
        for r in range(nblk):
            o_ref[r, hh] = a[r * Q_BLOCK:(r + 1) * Q_BLOCK]


def _qproj(cq, w, g, *, n_heads, hd, heads_per_tile, scale, norm, name):
    n, r = cq.shape
    tm = _tile(n, 1536, Q_BLOCK)
    nblk = tm // Q_BLOCK
    tn = heads_per_tile * hd
    return pl.pallas_call(
        functools.partial(_qproj_kernel, scale=scale, norm=norm, heads=heads_per_tile, hd=hd),
        grid=(n // tm, n_heads // heads_per_tile),
        in_specs=[pl.BlockSpec((tm, r), lambda i, j: (i, 0)),
                  pl.BlockSpec((r, tn), lambda i, j: (0, j)),
                  pl.BlockSpec((1, hd), lambda i, j: (0, 0))],
        out_specs=pl.BlockSpec((nblk, heads_per_tile, Q_BLOCK, hd), lambda i, j: (i, j, 0, 0)),
        out_shape=jax.ShapeDtypeStruct((n // Q_BLOCK, n_heads, Q_BLOCK, hd), BF16),
        name=name,
        compiler_params=_params("parallel", "arbitrary"),
    )(cq, w, g.reshape(1, hd))


def _attn_kernel(shift_ref, q_ref, qi_ref, wi_ref, ki_ref, kk_ref, lat_ref, wuv_ref, o_ref,
                 bias_ref, keyp_ref, acc_ref, m_ref, l_ref, *, k_sel, pad_front, idx_heads):
    j = pl.program_id(1)
    hg = pl.program_id(2)
    shift = shift_ref[0]
    use_shift = shift_ref[1] > 0.5
    hps, qb, c = q_ref.shape[1], q_ref.shape[2], q_ref.shape[3]
    di = qi_ref.shape[3]
    v = wuv_ref.shape[2]
    n_t = (j * qb + qb + KEY_TILE - 1) // KEY_TILE
    nt_dims = (((1,), (1,)), ((), ()))

    @pl.when(hg == 0)
    def _select():
        tq = j * qb + lax.broadcasted_iota(I32, (qb, 1), 0)
        limit = ((jnp.maximum(tq, CHUNK) + CHUNK) // CHUNK) * CHUNK

        def score_tile(kt, carry):
            off = pl.multiple_of(kt * KEY_TILE, KEY_TILE)
            ki_t = ki_ref[0, pl.ds(off, KEY_TILE), :]
            part = jnp.zeros((qb, KEY_TILE), F32)
            hpi = min(INDEXER_HEADS_PER_DOT, idx_heads)
            for g in range(idx_heads // hpi):
                qg = qi_ref[0, g * hpi:(g + 1) * hpi].reshape(hpi * qb, di)
                s = lax.dot_general(qg, ki_t, nt_dims, preferred_element_type=F32)
                s = jnp.maximum(s, 0.0)
                for hh in range(hpi):
                    h = g * hpi + hh
                    part = part + s[hh * qb:(hh + 1) * qb] * wi_ref[:, h:h + 1]
            bits = pltpu.bitcast(part, I32)
            key = bits ^ ((bits >> 31) & 0x7FFFFFFF)
            s_idx = off + lax.broadcasted_iota(I32, (qb, KEY_TILE), 1)
            adm = (s_idx >= pad_front) & (s_idx < limit)
            keyp_ref[:, pl.ds(off, KEY_TILE)] = jnp.where(adm, key, INT_MIN)
            return carry

        lax.fori_loop(0, n_t, score_tile, 0)

        def count_ge(thr):
            def body(kt, cnt):
                off = pl.multiple_of(kt * KEY_TILE, KEY_TILE)
                ge = jnp.where(keyp_ref[:, pl.ds(off, KEY_TILE)] >= thr, 1.0, 0.0)
                for u in range(KEY_TILE // LANES):
                    cnt = cnt + ge[:, u * LANES:(u + 1) * LANES]
                return cnt
            cnt = lax.fori_loop(0, n_t, body, jnp.zeros((qb, LANES), F32))
            return jnp.sum(cnt, axis=1, keepdims=True)

        def bit_step(bi, thr):
            cand = thr + lax.shift_left(jnp.int32(1), 31 - bi)
            return jnp.where(count_ge(cand) >= k_sel, cand, thr)

        thr = lax.fori_loop(0, 32, bit_step, jnp.full((qb, 1), INT_MIN, I32))
        thr = jnp.maximum(thr, INT_MIN + 1)

        def bias_tile(kt, carry):
            off = pl.multiple_of(kt * KEY_TILE, KEY_TILE)
            sel = keyp_ref[:, pl.ds(off, KEY_TILE)] >= thr
            bias_ref[:, pl.ds(off, KEY_TILE)] = jnp.where(sel, -shift, NEG)
            return carry

        lax.fori_loop(0, n_t, bias_tile, 0)

        @pl.when(jnp.max(count_ge(thr)) > k_sel)
        def _ties():
            quota = k_sel - count_ge(thr + 1)
            upper = (lax.broadcasted_iota(I32, (LANES, LANES), 0)
                     < lax.broadcasted_iota(I32, (LANES, LANES), 1)).astype(BF16)

            def chunk(ci, seen):
                off = pl.multiple_of(ci * LANES, LANES)
                kp = keyp_ref[:, pl.ds(off, LANES)]
                tie = jnp.where(kp == thr, 1.0, 0.0)
                rank = jnp.dot(tie.astype(BF16), upper, preferred_element_type=F32) + seen
                sel = jnp.where(kp > thr, 1.0, tie * jnp.where(rank < quota, 1.0, 0.0))
                bias_ref[:, pl.ds(off, LANES)] = jnp.where(sel > 0.5, -shift, NEG)
                return seen + jnp.sum(tie, axis=1, keepdims=True)

            lax.fori_loop(0, n_t * (KEY_TILE // LANES), chunk, jnp.zeros((qb, 1), F32))

    l_ref[...] = jnp.zeros(l_ref.shape, F32)
    acc_ref[...] = jnp.zeros(acc_ref.shape, F32)
    hpc = min(HEADS_PER_CHAIN, hps)
    rc = hpc * qb
    k_slabs = KEY_TILE // LANES
    c_slabs = c // LANES

    def masked_logits(ci, off):
        q_c = q_ref[0, ci * hpc:(ci + 1) * hpc].reshape(rc, c)
        s = lax.dot_general(q_c, kk_ref[0, pl.ds(off, KEY_TILE), :], nt_dims, preferred_element_type=F32)
        s = (s.reshape(hpc, qb, KEY_TILE) + bias_ref[:, pl.ds(off, KEY_TILE)][None]).reshape(rc, KEY_TILE)
        return [s[:, u * LANES:(u + 1) * LANES] for u in range(k_slabs)]

    def project_out(inv_l):
        for hh in range(hps):
            rs = slice(hh * qb, (hh + 1) * qb)
            o = jnp.concatenate([acc_ref[rs, u * LANES:(u + 1) * LANES] * inv_l[rs] for u in range(c_slabs)],
                                axis=1)
            y = jnp.dot(o.astype(BF16), wuv_ref[hh], preferred_element_type=F32)
            o_ref[:, hh * v:(hh + 1) * v] = y.astype(o_ref.dtype)

    @pl.when(use_shift)
    def _static_shift():
        def kv_step(kt, carry):
            off = pl.multiple_of(kt * KEY_TILE, KEY_TILE)
            lat_t = lat_ref[0, pl.ds(off, KEY_TILE), :]
            for ci in range(hps // hpc):
                rs = slice(ci * rc, (ci + 1) * rc)
                ps = [jnp.exp(sl) for sl in masked_logits(ci, off)]
                psum = ps[0]
                for pu in ps[1:]:
                    psum = psum + pu
                l_ref[rs, :] = l_ref[rs, :] + psum
                pv = jnp.dot(jnp.concatenate(ps, axis=1).astype(BF16), lat_t, preferred_element_type=F32)
                for u in range(c_slabs):
                    cs = slice(u * LANES, (u + 1) * LANES)
                    acc_ref[rs, cs] = acc_ref[rs, cs] + pv[:, cs]
            return carry

        lax.fori_loop(0, n_t, kv_step, 0)
        project_out(jnp.broadcast_to(1.0 / jnp.sum(l_ref[...], axis=1, keepdims=True), l_ref.shape))

    @pl.when(jnp.logical_not(use_shift))
    def _online():
        m_ref[...] = jnp.full(m_ref.shape, -3e38, F32)

        def kv_step(kt, carry):
            off = pl.multiple_of(kt * KEY_TILE, KEY_TILE)
            lat_t = lat_ref[0, pl.ds(off, KEY_TILE), :]
            for ci in range(hps // hpc):
                rs = slice(ci * rc, (ci + 1) * rc)
                slabs = masked_logits(ci, off)
                mx = slabs[0]
                for sl in slabs[1:]:
                    mx = jnp.maximum(mx, sl)
                m_prev = m_ref[rs, :]
                m_new = jnp.maximum(m_prev, jnp.max(mx, axis=1, keepdims=True))
                alpha = jnp.exp(m_prev - m_new)
                ps = [jnp.exp(sl - m_new) for sl in slabs]
                psum = ps[0]
                for pu in ps[1:]:
                    psum = psum + pu
                l_ref[rs, :] = alpha * l_ref[rs, :] + jnp.sum(psum, axis=1, keepdims=True)
                pv = jnp.dot(jnp.concatenate(ps, axis=1).astype(BF16), lat_t, preferred_element_type=F32)
                for u in range(c_slabs):
                    cs = slice(u * LANES, (u + 1) * LANES)
                    acc_ref[rs, cs] = alpha * acc_ref[rs, cs] + pv[:, cs]
                m_ref[rs, :] = m_new
            return carry

        lax.fori_loop(0, n_t, kv_step, 0)
        project_out(1.0 / l_ref[...])


def _sparse_attention(logit_bound, q, qi, widx, ki, kk, lat, w_uv, *, batch, tp, k_sel, pad_front):
    nblk, n_heads, qb, c = q.shape
    usable = 2.0 * logit_bound <= MAX_EXP_SPAN
    shift_info = jnp.stack([jnp.where(usable, logit_bound, 0.0), usable.astype(F32)]).astype(F32)
    idx_heads, di = qi.shape[1], qi.shape[3]
    tk = ki.shape[1]
    v = w_uv.shape[2]
    nq = tp // qb
    hps = min(HEADS_PER_STEP, n_heads)
    rows = hps * qb
    return pl.pallas_call(
        functools.partial(_attn_kernel, k_sel=k_sel, pad_front=pad_front, idx_heads=idx_heads),
        grid=(batch, nq, n_heads // hps),
        in_specs=[pl.BlockSpec(memory_space=pltpu.SMEM),
                  pl.BlockSpec((1, hps, qb, c), lambda b, j, h: (b * nq + j, h, 0, 0)),
                  pl.BlockSpec((1, idx_heads, qb, di), lambda b, j, h: (b * nq + j, 0, 0, 0)),
                  pl.BlockSpec((qb, LANES), lambda b, j, h: (b * nq + j, 0)),
                  pl.BlockSpec((1, tk, di), lambda b, j, h: (b, 0, 0), pipeline_mode=pl.Buffered(1)),
                  pl.BlockSpec((1, tk, c), lambda b, j, h: (b, 0, 0), pipeline_mode=pl.Buffered(1)),
                  pl.BlockSpec((1, tk, c), lambda b, j, h: (b, 0, 0), pipeline_mode=pl.Buffered(1)),
                  pl.BlockSpec((hps, c, v), lambda b, j, h: (h, 0, 0))],
        out_specs=pl.BlockSpec((qb, hps * v), lambda b, j, h: (b * nq + j, h)),
        out_shape=jax.ShapeDtypeStruct((nblk * qb, n_heads * v), BF16),
        scratch_shapes=[pltpu.VMEM((qb, tk), F32), pltpu.VMEM((qb, tk), I32),
                        pltpu.VMEM((rows, c), F32), pltpu.VMEM((rows, LANES), F32),
                        pltpu.VMEM((rows, LANES), F32)],
        name="sparse_attention",
        compiler_params=_params("parallel", "arbitrary", "arbitrary"),
    )(shift_info, q, qi, widx, ki, kk, lat, w_uv)


def _merge_kernel(yp_ref, ya_ref, wp_ref, wa_ref, gp_ref, ga_ref, o_ref):
    pool = jnp.dot(yp_ref[...], wp_ref[...], preferred_element_type=F32)
    attn = jnp.dot(ya_ref[...], wa_ref[...], preferred_element_type=F32)
    o_ref[...] = (gp_ref[...].astype(F32) * pool + ga_ref[...].astype(F32) * attn).astype(o_ref.dtype)


def _merge(y_pool, y_attn, w_bp, w_ba, gates):
    n, pw = y_pool.shape
    aw = y_attn.shape[1]
    d = w_bp.shape[1]
    tm = _tile(n, 768, 128)
    tn = 512
    return pl.pallas_call(
        _merge_kernel,
        grid=(n // tm, d // tn),
        in_specs=[pl.BlockSpec((tm, pw), lambda i, j: (i, 0)),
                  pl.BlockSpec((tm, aw), lambda i, j: (i, 0)),
                  pl.BlockSpec((pw, tn), lambda i, j: (0, j)),
                  pl.BlockSpec((aw, tn), lambda i, j: (0, j)),
                  pl.BlockSpec((tm, tn), lambda i, j: (i, j)),
                  pl.BlockSpec((tm, tn), lambda i, j: (i, j + d // tn))],
        out_specs=pl.BlockSpec((tm, tn), lambda i, j: (i, j)),
        out_shape=jax.ShapeDtypeStruct((n, d), BF16),
        name="branch_merge",
        compiler_params=_params("parallel", "arbitrary"),
    )(y_pool, y_attn, w_bp, w_ba, gates, gates)


def _out_proj_kernel(x_ref, w_ref, r_ref, o_ref):
    o_ref[...] = r_ref[...] + jnp.dot(x_ref[...], w_ref[...], preferred_element_type=F32)


def _out_proj(merged, w_out, resid):
    n, k = merged.shape
    d = w_out.shape[1]
    tm = _tile(n, 768, 128)
    tn = _tile(d, 1024, LANES)
    return pl.pallas_call(
        _out_proj_kernel,
        grid=(n // tm, d // tn),
        in_specs=[pl.BlockSpec((tm, k), lambda i, j: (i, 0)),
                  pl.BlockSpec((k, tn), lambda i, j: (0, j)),
                  pl.BlockSpec((tm, tn), lambda i, j: (i, j))],
        out_specs=pl.BlockSpec((tm, tn), lambda i, j: (i, j)),
        out_shape=jax.ShapeDtypeStruct((n, d), F32),
        name="out_proj_residual",
        compiler_params=_params("parallel", "arbitrary"),
    )(merged, w_out, resid)


def _pack_halves(x):
    w = x.shape[1] // 2
    lo = pltpu.bitcast(x[:, :w].astype(BF16).astype(F32), I32)
    hi = pltpu.bitcast(x[:, w:].astype(BF16).astype(F32), I32)
    return ((lo >> 16) & 0xFFFF) | (hi & -65536)


def _unpack_halves(p):
    return pltpu.bitcast(p << 16, F32), pltpu.bitcast(p & -65536, F32)


def _router_kernel(h_ref, g_ref, rw_ref, rb_ref, xp_o, e_o, w_o, rank_o, cnt_o, carry_ref):
    i = pl.program_id(0)

    @pl.when(i == 0)
    def _():
        carry_ref[...] = jnp.zeros_like(carry_ref)

    x = h_ref[...]
    xn = x * lax.rsqrt(jnp.mean(x * x, axis=-1, keepdims=True) + EPS) * g_ref[...]
    xp_o[...] = _pack_halves(xn)
    logits = jnp.dot(xn.astype(BF16), rw_ref[...], preferred_element_type=F32) + rb_ref[...]
    tm = logits.shape[0]
    lane = lax.broadcasted_iota(I32, (tm, LANES), 1).astype(F32)
    vals, hots = [], []
    cur = logits
    for k in range(TOP_K_EXPERTS):
        m = jnp.max(cur, axis=1, keepdims=True)
        idx = jnp.min(jnp.where(cur == m, lane, float(LANES)), axis=1, keepdims=True)
        hot = lane == idx
        vals.append(m)
        hots.append(hot)
        e_o[:, k:k + 1] = idx.astype(I32)
        cur = jnp.where(hot, -jnp.inf, cur)
    exps = [jnp.exp(vk - vals[0]) for vk in vals]
    denom = exps[0]
    for ek in exps[1:]:
        denom = denom + ek
    for k in range(TOP_K_EXPERTS):
        w_o[:, k:k + 1] = exps[k] / denom
    onehot = jnp.zeros((tm, LANES), F32)
    for hot in hots:
        onehot = onehot + jnp.where(hot, 1.0, 0.0)
    lower = (lax.broadcasted_iota(I32, (tm, tm), 0) > lax.broadcasted_iota(I32, (tm, tm), 1)).astype(BF16)
    before = jnp.dot(lower, onehot.astype(BF16), preferred_element_type=F32) + carry_ref[...]
    for k in range(TOP_K_EXPERTS):
        rank_o[:, k:k + 1] = jnp.sum(jnp.where(hots[k], before, 0.0), axis=1, keepdims=True).astype(I32)
    carry_ref[...] = carry_ref[...] + jnp.sum(onehot, axis=0, keepdims=True)
    cnt_o[...] = carry_ref[...]


def _router(h2, norm2_g, rw, rb):
    n, d = h2.shape
    tm = ROUTE_TILE
    kk = TOP_K_EXPERTS
    return pl.pallas_call(
        _router_kernel,
        grid=(n // tm,),
        in_specs=[pl.BlockSpec((tm, d), lambda i: (i, 0)),
                  pl.BlockSpec((1, d), lambda i: (0, 0)),
                  pl.BlockSpec((d, LANES), lambda i: (0, 0)),
                  pl.BlockSpec((1, LANES), lambda i: (0, 0))],
        out_specs=[pl.BlockSpec((tm, d // 2), lambda i: (i, 0)),
                   pl.BlockSpec((tm, kk), lambda i: (i, 0)),
                   pl.BlockSpec((tm, kk), lambda i: (i, 0)),
                   pl.BlockSpec((tm, kk), lambda i: (i, 0)),
                   pl.BlockSpec((1, LANES), lambda i: (0, 0))],
        out_shape=[jax.ShapeDtypeStruct((n, d // 2), I32),
                   jax.ShapeDtypeStruct((n, kk), I32),
                   jax.ShapeDtypeStruct((n, kk), F32),
                   jax.ShapeDtypeStruct((n, kk), I32),
                   jax.ShapeDtypeStruct((1, LANES), F32)],
        scratch_shapes=[pltpu.VMEM((1, LANES), F32)],
        name="moe_router",
        compiler_params=_params("arbitrary"),
    )(h2, norm2_g.reshape(1, d), rw, rb)


def _row_copy(src_ref, src_row, dst_ref, dst_row, sem):
    return pltpu.make_async_copy(src_ref.at[pl.ds(src_row, 1), :], dst_ref.at[pl.ds(dst_row, 1), :], sem)


def _dispatch_kernel(dest_ref, pad_start_ref, pad_count_ref, xp_ref, xs_ref, zero_ref, sem, zero_sem, tail_sem):
    tm = xp_ref.shape[0]
    base = pl.program_id(0) * tm * TOP_K_EXPERTS

    def issue(r, carry):
        for k in range(TOP_K_EXPERTS):
            _row_copy(xp_ref, r, xs_ref, dest_ref[base + r * TOP_K_EXPERTS + k], sem).start(priority=k % 2)
        return carry

    lax.fori_loop(0, tm, issue, 0, unroll=4)
    for k in range(TOP_K_EXPERTS):
        pltpu.make_async_copy(xp_ref, xs_ref.at[pl.ds(0, tm), :], sem).wait()

    @pl.when(pl.program_id(0) == pl.num_programs(0) - 1)
    def _():
        zero_ref[...] = jnp.zeros_like(zero_ref)

        def per_expert(e, total):
            def zero_row(r, carry):
                _row_copy(zero_ref, 0, xs_ref, pad_start_ref[e] + r, zero_sem).start()
                return carry
            lax.fori_loop(0, pad_count_ref[e], zero_row, 0)
            return total + pad_count_ref[e]

        n_exp = pad_start_ref.shape[0]
        total = lax.fori_loop(0, n_exp, per_expert, 0)

        def zero_done(r, carry):
            _row_copy(zero_ref, 0, xs_ref, 0, zero_sem).wait()
            return carry

        lax.fori_loop(0, total, zero_done, 0)

        tile = zero_ref.shape[0]
        first = (pad_start_ref[n_exp - 1] + pad_count_ref[n_exp - 1]) // tile
        n_tail = xs_ref.shape[0] // tile - first

        def tile_copy(i):
            row0 = pl.multiple_of((first + i) * tile, tile)
            return pltpu.make_async_copy(zero_ref, xs_ref.at[pl.ds(row0, tile), :], tail_sem)

        def zero_tile(i, carry):
            tile_copy(i).start()
            return carry

        def tile_done(i, carry):
            tile_copy(i).wait()
            return carry

        lax.fori_loop(0, n_tail, zero_tile, 0)
        lax.fori_loop(0, n_tail, tile_done, 0)


def _dispatch(dest_flat, pad_start, pad_count, xp, n_slots):
    n, w = xp.shape
    tm = ROUTE_TILE
    return pl.pallas_call(
        _dispatch_kernel,
        grid_spec=pltpu.PrefetchScalarGridSpec(
            num_scalar_prefetch=3,
            grid=(n // tm,),
            in_specs=[pl.BlockSpec((tm, w), lambda i, *_: (i, 0))],
            out_specs=pl.BlockSpec(memory_space=pl.ANY),
            scratch_shapes=[pltpu.VMEM((EXPERT_TILE, w), I32), pltpu.SemaphoreType.DMA(()),
                            pltpu.SemaphoreType.DMA(()), pltpu.SemaphoreType.DMA(())]),
        out_shape=jax.ShapeDtypeStruct((n_slots, w), I32),
        name="moe_dispatch",
        compiler_params=_params("arbitrary"),
    )(dest_flat, pad_start, pad_count, xp)


def _deinterleave_kernel(w_ref, o_ref):
    blk = 2 * LANES
    dst = lax.broadcasted_iota(I32, (blk, blk), 1)
    src = jnp.where(dst < LANES, 2 * dst, 2 * (dst - LANES) + 1)
    perm = (lax.broadcasted_iota(I32, (blk, blk), 0) == src).astype(BF16)
    for b in range(w_ref.shape[2] // blk):
        cols = slice(b * blk, (b + 1) * blk)
        o_ref[0, :, cols] = jnp.dot(w_ref[0, :, cols].astype(BF16), perm,
                                    preferred_element_type=F32).astype(o_ref.dtype)


def _deinterleave_gate_up(w_gate_up):
    n_exp, d, de2 = w_gate_up.shape
    assert de2 % (2 * LANES) == 0
    tk = _tile(d, 512, 16)
    return pl.pallas_call(
        _deinterleave_kernel,
        grid=(n_exp, d // tk),
        in_specs=[pl.BlockSpec((1, tk, de2), lambda e, i: (e, i, 0))],
        out_specs=pl.BlockSpec((1, tk, de2), lambda e, i: (e, i, 0)),
        out_shape=jax.ShapeDtypeStruct((n_exp, d, de2), BF16),
        name="moe_weight_regroup",
        compiler_params=_params("parallel", "arbitrary"),
    )(w_gate_up)


def _expert_up_kernel(te_ref, tv_ref, xs_ref, w_ref, b_ref, o_ref):
    t = pl.program_id(0)

    @pl.when(tv_ref[t] > 0)
    def _():
        lo, hi = _unpack_halves(xs_ref[...])
        x = jnp.concatenate([lo.astype(BF16), hi.astype(BF16)], axis=1)
        gu = jnp.dot(x, w_ref[0], preferred_element_type=F32) + b_ref[0]
        for blk in range(gu.shape[1] // (2 * LANES)):
            g0 = blk * 2 * LANES
            gate = jnp.minimum(gu[:, g0:g0 + LANES], SWIGLU_LIMIT)
            up = jnp.clip(gu[:, g0 + LANES:g0 + 2 * LANES], -SWIGLU_LIMIT, SWIGLU_LIMIT)
            act = (up + 1.0) * gate * jax.nn.sigmoid(SWIGLU_ALPHA * gate)
            o_ref[:, blk * LANES:(blk + 1) * LANES] = act.astype(o_ref.dtype)

    @pl.when(tv_ref[t] == 0)
    def _():
        o_ref[...] = jnp.zeros_like(o_ref)


def _expert_up(tile_expert, tile_valid, xs, w_gu, b_gu):
    n_slots, w = xs.shape
    n_exp, d, de2 = w_gu.shape
    tm = EXPERT_TILE
    return pl.pallas_call(
        _expert_up_kernel,
        grid_spec=pltpu.PrefetchScalarGridSpec(
            num_scalar_prefetch=2,
            grid=(n_slots // tm,),
            in_specs=[pl.BlockSpec((tm, w), lambda t, te, tv: (t, 0)),
                      pl.BlockSpec((1, d, de2), lambda t, te, tv: (te[t], 0, 0)),
                      pl.BlockSpec((1, 1, de2), lambda t, te, tv: (te[t], 0, 0))],
            out_specs=pl.BlockSpec((tm, de2 // 2), lambda t, te, tv: (t, 0))),
        out_shape=jax.ShapeDtypeStruct((n_slots, de2 // 2), BF16),
        name="moe_expert_up",
        compiler_params=_params("arbitrary"),
    )(tile_expert, tile_valid, xs, w_gu, b_gu)


def _expert_down_kernel(te_ref, tv_ref, a_ref, w_ref, b_ref, o_ref, wbf_ref):
    t = pl.program_id(0)

    @pl.when(jnp.logical_or(t == 0, te_ref[t] != te_ref[jnp.maximum(t - 1, 0)]))
    def _():
        wbf_ref[...] = w_ref[0].astype(BF16)

    @pl.when(tv_ref[t] > 0)
    def _():
        y = jnp.dot(a_ref[...], wbf_ref[...], preferred_element_type=F32) + b_ref[0]
        o_ref[...] = _pack_halves(y)

    @pl.when(tv_ref[t] == 0)
    def _():
        o_ref[...] = jnp.zeros_like(o_ref)


def _expert_down(tile_expert, tile_valid, act, w_down, b_down):
    n_slots, de = act.shape
    n_exp, _, d = w_down.shape
    tm = EXPERT_TILE
    return pl.pallas_call(
        _expert_down_kernel,
        grid_spec=pltpu.PrefetchScalarGridSpec(
            num_scalar_prefetch=2,
            grid=(n_slots // tm,),
            in_specs=[pl.BlockSpec((tm, de), lambda t, te, tv: (t, 0)),
                      pl.BlockSpec((1, de, d), lambda t, te, tv: (te[t], 0, 0)),
                      pl.BlockSpec((1, 1, d), lambda t, te, tv: (te[t], 0, 0))],
            out_specs=pl.BlockSpec((tm, d // 2), lambda t, te, tv: (t, 0)),
            scratch_shapes=[pltpu.VMEM((de, d), BF16)]),
        out_shape=jax.ShapeDtypeStruct((n_slots, d // 2), I32),
        name="moe_expert_down",
        compiler_params=_params("arbitrary"),
    )(tile_expert, tile_valid, act, w_down, b_down)


def _combine_kernel(dest_ref, w_ref, h_ref, ye_ref, o_ref, buf_ref, sems, *, nq):
    b = pl.program_id(0)
    i = pl.program_id(1)
    n_i = pl.num_programs(1)
    n_steps = pl.num_programs(0) * n_i
    tm = h_ref.shape[0]
    half = buf_ref.shape[3]
    step = b * n_i + i
    slot = step % 2

    def start_rows(bb, ii, sl):
        base = ((bb * nq + 1 + ii) * tm) * TOP_K_EXPERTS

        def issue(r, carry):
            for k in range(TOP_K_EXPERTS):
                _row_copy(ye_ref, dest_ref[base + r * TOP_K_EXPERTS + k], buf_ref.at[sl, k], r,
                          sems.at[sl]).start(priority=k % 2)
            return carry

        lax.fori_loop(0, tm, issue, 0, unroll=4)

    @pl.when(step == 0)
    def _():
        start_rows(b, i, slot)

    @pl.when(step + 1 < n_steps)
    def _():
        wrap = i + 1 == n_i
        start_rows(jnp.where(wrap, b + 1, b), jnp.where(wrap, 0, i + 1), 1 - slot)

    for k in range(TOP_K_EXPERTS):
        pltpu.make_async_copy(ye_ref.at[pl.ds(0, tm), :], buf_ref.at[slot, k], sems.at[slot]).wait()

    lo_sum = h_ref[:, :half]
    hi_sum = h_ref[:, half:]
    for k in range(TOP_K_EXPERTS):
        lo, hi = _unpack_halves(buf_ref[slot, k])
        wk = w_ref[:, k:k + 1]
        lo_sum = lo_sum + wk * lo
        hi_sum = hi_sum + wk * hi
    o_ref[0, :, :half] = lo_sum
    o_ref[0, :, half:] = hi_sum


def _combine(dest_flat, sel_w, h2, ye, *, batch, seq, tp):
    d = h2.shape[1]
    tm = Q_BLOCK
    nq = tp // tm
    return pl.pallas_call(
        functools.partial(_combine_kernel, nq=nq),
        grid_spec=pltpu.PrefetchScalarGridSpec(
            num_scalar_prefetch=1,
            grid=(batch, seq // tm),
            in_specs=[pl.BlockSpec((tm, TOP_K_EXPERTS), lambda b, i, dest: (b * nq + 1 + i, 0)),
                      pl.BlockSpec((tm, d), lambda b, i, dest: (b * nq + 1 + i, 0)),
                      pl.BlockSpec(memory_space=pl.ANY)],
            out_specs=pl.BlockSpec((1, tm, d), lambda b, i, dest: (b, i, 0)),
            scratch_shapes=[pltpu.VMEM((2, TOP_K_EXPERTS, tm, d // 2), I32), pltpu.SemaphoreType.DMA((2,))]),
        out_shape=jax.ShapeDtypeStruct((batch, seq, d), F32),
        name="moe_combine",
        compiler_params=_params("arbitrary", "arbitrary"),
    )(dest_flat, sel_w, h2, ye)


def _moe(h2, norm2_g, router_w, router_b, w_gate_up, b_gate_up, w_down, b_down, *, batch, seq, tp):
    n, d = h2.shape
    n_exp = router_w.shape[1]
    de = w_down.shape[1]
    assert n_exp <= LANES and n % ROUTE_TILE == 0
    rw = jnp.pad(router_w, ((0, 0), (0, LANES - n_exp))).astype(BF16)
    rb = jnp.pad(router_b, (0, LANES - n_exp), constant_values=NEG).reshape(1, LANES)
    xp, sel_e, sel_w, rank, counts = _router(h2, norm2_g, rw, rb)

    cnt = counts[0, :n_exp].astype(I32)
    padded = ((cnt + EXPERT_TILE - 1) // EXPERT_TILE) * EXPERT_TILE
    e_ids = jnp.arange(n_exp, dtype=I32)
    ends = jnp.sum(jnp.where(e_ids[:, None] <= e_ids[None, :], padded[:, None], 0), axis=0)
    starts = ends - padded
    total = ends[n_exp - 1]
    dest_flat = (starts[sel_e] + rank).reshape(-1)
    n_tiles = (n * TOP_K_EXPERTS) // EXPERT_TILE + n_exp
    tile_start = jnp.arange(n_tiles, dtype=I32) * EXPERT_TILE
    tile_valid = (tile_start < total).astype(I32)
    last_expert = jnp.sum((ends <= total - 1).astype(I32))
    tile_expert = jnp.minimum(jnp.sum((ends[None, :] <= tile_start[:, None]).astype(I32), axis=1), last_expert)

    w_gu = _deinterleave_gate_up(w_gate_up)
    nb = (2 * de) // (2 * LANES)
    b_gu = b_gate_up.reshape(n_exp, nb, LANES, 2).transpose(0, 1, 3, 2).reshape(n_exp, 1, 2 * de)

    xs = _dispatch(dest_flat, starts + cnt, padded - cnt, xp, n_tiles * EXPERT_TILE)
    act = _expert_up(tile_expert, tile_valid, xs, w_gu, b_gu)
    ye = _expert_down(tile_expert, tile_valid, act, w_down, b_down.reshape(n_exp, 1, d))
    return _combine(dest_flat, sel_w, h2, ye, batch=batch, seq=seq, tp=tp)


def kernel(x, meta_tokens, norm1_g, w_in, q_norm_g, w_uq, w_iq, kv_norm_g, q_head_norm_g, k_head_norm_g,
           idx_k_norm_g, w_uv, w_branch_attn, w_pool, pool_scale, w_branch_pool, w_out, norm2_g, router_w,
           router_b, w_gate_up, b_gate_up, w_down, b_down):
    batch, seq, d = x.shape
    n_meta = meta_tokens.shape[0]
    depth = norm1_g.shape[0]
    qr, kvr, di = q_norm_g.shape[1], kv_norm_g.shape[1], idx_k_norm_g.shape[1]
    n_heads, v = w_uv.shape[1], w_uv.shape[3]
    idx_heads = w_iq.shape[2] // di
    pw = pool_scale.shape[1]
    assert depth == 1
    assert n_meta <= CHUNK and seq % Q_BLOCK == 0 and idx_heads <= LANES
    assert qr % kvr == 0 and (qr + kvr) % di == 0 and di % LANES == 0
    pad_front = Q_BLOCK - n_meta
    tp = Q_BLOCK + seq
    tk = -(-tp // KEY_TILE) * KEY_TILE
    k_sel = min(TOPK_MAX, seq // 4)
    attn_scale = kvr ** -0.5
    idx_scale = (di ** -0.5) * (idx_heads ** -0.5)

    head_rows = jnp.concatenate([jnp.zeros((pad_front, d), x.dtype), meta_tokens.astype(x.dtype)], axis=0)

    out = None
    for l in range(depth):
        n_small = qr + kvr + di
        o_pool = n_small + idx_heads
        o_gate = o_pool + pw
        w_small = jnp.pad(w_in[l][:, :o_pool], ((0, 0), (0, LANES - idx_heads))).astype(BF16)
        w_a = w_in[l][:, o_pool:o_gate].astype(BF16)
        w_g = w_in[l][:, o_gate:].astype(BF16)

        h, xn = _embed_norm(x, head_rows, norm1_g[l])
        small = _matmul(xn, w_small, F32, tn=_tile(w_small.shape[1], 1024, LANES), name="in_proj_latents")
        a_pool = _matmul(xn, w_a, BF16, tn=_tile(pw, 1024, LANES), name="in_proj_pool")
        gates = _matmul(xn, w_g, BF16, tn=_tile(2 * d, 1024, LANES), sigmoid=True, name="in_proj_gates")

        y_pool = _pool_mixer(a_pool, w_pool[l].astype(BF16), pool_scale[l], batch=batch, tp=tp,
                             pad_front=pad_front)

        cq, widx = _prep_q(small, q_norm_g[l], qr, n_small, idx_scale)
        lat, kk, ki = _prep_kv(small, kv_norm_g[l], k_head_norm_g[l], idx_k_norm_g[l],
                               batch=batch, tp=tp, tk=tk, qr=qr, kvr=kvr, di=di)
        q = _qproj(cq, w_uq[l].astype(BF16), q_head_norm_g[l], n_heads=n_heads, hd=kvr,
                   heads_per_tile=min(2, n_heads),
                   scale=attn_scale, norm=True, name="q_proj_headnorm")
        qi = _qproj(cq, w_iq[l].astype(BF16), jnp.ones((di,), F32), n_heads=idx_heads, hd=di,
                    heads_per_tile=min(8, idx_heads), scale=1.0, norm=False, name="indexer_q_proj")
        logit_bound = (BOUND_MARGIN * kvr * attn_scale * jnp.max(jnp.abs(q_head_norm_g[l]))
                       * jnp.max(jnp.abs(k_head_norm_g[l])))
        y_attn = _sparse_attention(logit_bound, q, qi, widx, ki, kk, lat, w_uv[l].astype(BF16), batch=batch,
                                   tp=tp, k_sel=k_sel, pad_front=pad_front)

        merged = _merge(y_pool, y_attn, w_branch_pool[l].astype(BF16), w_branch_attn[l].astype(BF16), gates)
        h2 = _out_proj(merged, w_out[l].astype(BF16), h)

        out = _moe(h2, norm2_g[l], router_w[l], router_b[l], w_gate_up[l], b_gate_up[l], w_down[l],
                   b_down[l], batch=batch, seq=seq, tp=tp)
    return out
```

```python
import functools

import jax
import jax.numpy as jnp
from jax import lax
from jax.experimental import pallas as pl
from jax.experimental.pallas import tpu as pltpu

F32 = jnp.float32
BF16 = jnp.bfloat16
I32 = jnp.int32

CHUNK = 64
POOL_WINDOWS = (2, 4, 8, 16)
TOPK_MAX = 256
TOP_K_EXPERTS = 4
SWIGLU_LIMIT = 7.0
SWIGLU_ALPHA = 1.702
EPS = 1e-6
NEG = -1e30

Q_BLOCK = 128
KEY_TILE = 512
HEADS_PER_STEP = 32
INDEXER_HEADS_PER_DOT = 8
HEADS_PER_CHAIN = 4
HALO = 16
EXPERT_TILE = 256
ROUTE_TILE = 512
LANES = 128
INT_MIN = -2 ** 31
MAX_EXP_SPAN = 80.0
BOUND_MARGIN = 1.02
V7X_VMEM_BYTES = 64 * 1024 * 1024
VMEM_LIMIT = V7X_VMEM_BYTES * 7 // 8


def _tile(n, target, mult):
    best = None
    for t in range(mult, min(n, target) + 1, mult):
        if n % t == 0:
            best = t
    assert best is not None, (n, target, mult)
    return best


def _params(*sem):
    return pltpu.CompilerParams(dimension_semantics=sem, vmem_limit_bytes=VMEM_LIMIT)


def _embed_norm_kernel(x_ref, head_ref, g_ref, h_ref, xn_ref):
    rows = jnp.where(pl.program_id(1) == 0, head_ref[...], x_ref[0])
    h_ref[...] = rows
    ms = jnp.mean(rows * rows, axis=-1, keepdims=True)
    xn_ref[...] = (rows * lax.rsqrt(ms + EPS) * g_ref[...]).astype(xn_ref.dtype)


def _embed_norm(x, head_rows, g):
    batch, seq, d = x.shape
    nq = seq // Q_BLOCK + 1
    n = batch * nq * Q_BLOCK
    return pl.pallas_call(
        _embed_norm_kernel,
        grid=(batch, nq),
        in_specs=[pl.BlockSpec((1, Q_BLOCK, d), lambda b, i: (b, jnp.maximum(i - 1, 0), 0)),
                  pl.BlockSpec((Q_BLOCK, d), lambda b, i: (0, 0)),
                  pl.BlockSpec((1, d), lambda b, i: (0, 0))],
        out_specs=[pl.BlockSpec((Q_BLOCK, d), lambda b, i: (b * nq + i, 0)),
                   pl.BlockSpec((Q_BLOCK, d), lambda b, i: (b * nq + i, 0))],
        out_shape=[jax.ShapeDtypeStruct((n, d), x.dtype), jax.ShapeDtypeStruct((n, d), BF16)],
        name="embed_rmsnorm",
        compiler_params=_params("parallel", "arbitrary"),
    )(x, head_rows, g.reshape(1, d))


def _matmul_kernel(x_ref, w_ref, o_ref, *, sigmoid):
    acc = jnp.dot(x_ref[...], w_ref[...], preferred_element_type=F32)
    if sigmoid:
        acc = jax.nn.sigmoid(acc)
    o_ref[...] = acc.astype(o_ref.dtype)


def _matmul(x, w, out_dtype, *, tn, name, sigmoid=False):
    m, k = x.shape
    n = w.shape[1]
    tm = _tile(m, 768, 128)
    return pl.pallas_call(
        functools.partial(_matmul_kernel, sigmoid=sigmoid),
        grid=(m // tm, n // tn),
        in_specs=[pl.BlockSpec((tm, k), lambda i, j: (i, 0)), pl.BlockSpec((k, tn), lambda i, j: (0, j))],
        out_specs=pl.BlockSpec((tm, tn), lambda i, j: (i, j)),
        out_shape=jax.ShapeDtypeStruct((m, n), out_dtype),
        name=name,
        compiler_params=_params("parallel", "arbitrary"),
    )(x, w)


def _prep_q_kernel(cq_ref, wi_ref, g_ref, cq_o, wi_o, *, idx_scale):
    x = cq_ref[...]
    ms = jnp.mean(x * x, axis=-1, keepdims=True)
    cq_o[...] = (x * lax.rsqrt(ms + EPS) * g_ref[...]).astype(cq_o.dtype)
    wi_o[...] = wi_ref[...] * idx_scale


def _prep_q(small, q_norm_g, qr, wi_col, idx_scale):
    n = small.shape[0]
    tm = _tile(n, 768, 128)
    return pl.pallas_call(
        functools.partial(_prep_q_kernel, idx_scale=idx_scale),
        grid=(n // tm,),
        in_specs=[pl.BlockSpec((tm, qr), lambda i: (i, 0)),
                  pl.BlockSpec((tm, LANES), lambda i: (i, wi_col // LANES)),
                  pl.BlockSpec((1, qr), lambda i: (0, 0))],
        out_specs=[pl.BlockSpec((tm, qr), lambda i: (i, 0)), pl.BlockSpec((tm, LANES), lambda i: (i, 0))],
        out_shape=[jax.ShapeDtypeStruct((n, qr), BF16), jax.ShapeDtypeStruct((n, LANES), F32)],
        name="prep_query_latent",
        compiler_params=_params("parallel"),
    )(small, small, q_norm_g.reshape(1, qr))


def _prep_kv_kernel(ckv_ref, kidx_ref, gkv_ref, gkh_ref, gki_ref, lat_o, kk_o, ki_o, *, n_valid):
    i = pl.program_id(1)

    @pl.when(i < n_valid)
    def _():
        c = ckv_ref[...]
        lat = c * lax.rsqrt(jnp.mean(c * c, axis=-1, keepdims=True) + EPS) * gkv_ref[...]
        kk = lat * lax.rsqrt(jnp.mean(lat * lat, axis=-1, keepdims=True) + EPS) * gkh_ref[...]
        k = kidx_ref[...]
        ki = k * lax.rsqrt(jnp.mean(k * k, axis=-1, keepdims=True) + EPS) * gki_ref[...]
        lat_o[0] = lat.astype(lat_o.dtype)
        kk_o[0] = kk.astype(kk_o.dtype)
        ki_o[0] = ki.astype(ki_o.dtype)

    @pl.when(i >= n_valid)
    def _():
        lat_o[...] = jnp.zeros_like(lat_o)
        kk_o[...] = jnp.zeros_like(kk_o)
        ki_o[...] = jnp.zeros_like(ki_o)


def _prep_kv(small, gkv, gkh, gki, *, batch, tp, tk, qr, kvr, di):
    nq = tp // Q_BLOCK
    nk = tk // Q_BLOCK

    def row(b, i):
        return b * nq + jnp.minimum(i, nq - 1)

    return pl.pallas_call(
        functools.partial(_prep_kv_kernel, n_valid=nq),
        grid=(batch, nk),
        in_specs=[pl.BlockSpec((Q_BLOCK, kvr), lambda b, i: (row(b, i), qr // kvr)),
                  pl.BlockSpec((Q_BLOCK, di), lambda b, i: (row(b, i), (qr + kvr) // di)),
                  pl.BlockSpec((1, kvr), lambda b, i: (0, 0)),
                  pl.BlockSpec((1, kvr), lambda b, i: (0, 0)),
                  pl.BlockSpec((1, di), lambda b, i: (0, 0))],
        out_specs=[pl.BlockSpec((1, Q_BLOCK, kvr), lambda b, i: (b, i, 0)),
                   pl.BlockSpec((1, Q_BLOCK, kvr), lambda b, i: (b, i, 0)),
                   pl.BlockSpec((1, Q_BLOCK, di), lambda b, i: (b, i, 0))],
        out_shape=[jax.ShapeDtypeStruct((batch, tk, kvr), BF16),
                   jax.ShapeDtypeStruct((batch, tk, kvr), BF16),
                   jax.ShapeDtypeStruct((batch, tk, di), BF16)],
        name="prep_keys",
        compiler_params=_params("parallel", "arbitrary"),
    )(small, small, gkv.reshape(1, kvr), gkh.reshape(1, kvr), gki.reshape(1, di))


def _pool_kernel(a_ref, halo_ref, w_ref, sc_ref, o_ref, xs_ref, *, tp_tile, pad_front, pg):
    i = pl.program_id(1)
    t = i * tp_tile + lax.broadcasted_iota(I32, (tp_tile, 1), 0) - pad_front
    for g, win in enumerate(POOL_WINDOWS):
        cols = slice(g * pg, (g + 1) * pg)
        xs_ref[0:HALO, :] = halo_ref[0, :, cols].astype(F32)
        xs_ref[HALO:, :] = a_ref[0, :, cols].astype(F32)
        cur = xs_ref[pl.ds(HALO, tp_tile), :]
        acc = cur
        for k in range(1, win):
            acc = acc + xs_ref[pl.ds(HALO - k, tp_tile), :]
        cnt = jnp.clip(t + 1, 1, win).astype(F32)
        pooled = (acc / cnt - cur).astype(BF16)
        y = jnp.dot(pooled, w_ref[g], preferred_element_type=F32) * sc_ref[:, cols]
        o_ref[0, :, cols] = y.astype(o_ref.dtype)


def _pool_mixer(a, w_pool, pool_scale, *, batch, tp, pad_front):
    pw = a.shape[-1]
    n_groups, pg, _ = w_pool.shape
    tpt = _tile(tp, 1536, HALO)
    a3 = a.reshape(batch, tp, pw)
    halo_blocks = tpt // HALO
    out = pl.pallas_call(
        functools.partial(_pool_kernel, tp_tile=tpt, pad_front=pad_front, pg=pg),
        grid=(batch, tp // tpt),
        in_specs=[pl.BlockSpec((1, tpt, pw), lambda b, i: (b, i, 0)),
                  pl.BlockSpec((1, HALO, pw), lambda b, i: (b, jnp.maximum(i * halo_blocks - 1, 0), 0)),
                  pl.BlockSpec((n_groups, pg, pg), lambda b, i: (0, 0, 0)),
                  pl.BlockSpec((1, pw), lambda b, i: (0, 0))],
        out_specs=pl.BlockSpec((1, tpt, pw), lambda b, i: (b, i, 0)),
        out_shape=jax.ShapeDtypeStruct((batch, tp, pw), BF16),
        scratch_shapes=[pltpu.VMEM((tpt + HALO, pg), F32)],
        name="pool_mixer",
        compiler_params=_params("parallel", "arbitrary"),
    )(a3, a3, w_pool, pool_scale.reshape(1, pw))
    return out.reshape(batch * tp, pw)


def _qproj_kernel(x_ref, w_ref, g_ref, o_ref, *, scale, norm, heads, hd):
    acc = jnp.dot(x_ref[...], w_ref[...], preferred_element_type=F32)
    nblk = o_ref.shape[0]
    for hh in range(heads):
        a = acc[:, hh * hd:(hh + 1) * hd]
        if norm:
            a = a * lax.rsqrt(jnp.mean(a * a, axis=-1, keepdims=True) + EPS) * (g_ref[...] * scale)
        a = a.astype(o_ref.dtype)
        for r in range(nblk):
            o_ref[r, hh] = a[r * Q_BLOCK:(r + 1) * Q_BLOCK]


def _qproj(cq, w, g, *, n_heads, hd, heads_per_tile, scale, norm, name):
    n, r = cq.shape
    tm = _tile(n, 1536, Q_BLOCK)
    nblk = tm // Q_BLOCK
    tn = heads_per_tile * hd
    return pl.pallas_call(
        functools.partial(_qproj_kernel, scale=scale, norm=norm, heads=heads_per_tile, hd=hd),
        grid=(n // tm, n_heads // heads_per_tile),
        in_specs=[pl.BlockSpec((tm, r), lambda i, j: (i, 0)),
                  pl.BlockSpec((r, tn), lambda i, j: (0, j)),
                  pl.BlockSpec((1, hd), lambda i, j: (0, 0))],
        out_specs=pl.BlockSpec((nblk, heads_per_tile, Q_BLOCK, hd), lambda i, j: (i, j, 0, 0)),
        out_shape=jax.ShapeDtypeStruct((n // Q_BLOCK, n_heads, Q_BLOCK, hd), BF16),
        name=name,
        compiler_params=_params("parallel", "arbitrary"),
    )(cq, w, g.reshape(1, hd))


def _attn_kernel(shift_ref, q_ref, qi_ref, wi_ref, ki_ref, kk_ref, lat_ref, wuv_ref, o_ref,
                 bias_ref, keyp_ref, acc_ref, m_ref, l_ref, *, k_sel, pad_front, idx_heads):
    j = pl.program_id(1)
    hg = pl.program_id(2)
    shift = shift_ref[0]
    use_shift = shift_ref[1] > 0.5
    hps, qb, c = q_ref.shape[1], q_ref.shape[2], q_ref.shape[3]
    di = qi_ref.shape[3]
    v = wuv_ref.shape[2]
    n_t = (j * qb + qb + KEY_TILE - 1) // KEY_TILE
    nt_dims = (((1,), (1,)), ((), ()))

    @pl.when(hg == 0)
    def _select():
        tq = j * qb + lax.broadcasted_iota(I32, (qb, 1), 0)
        limit = ((jnp.maximum(tq, CHUNK) + CHUNK) // CHUNK) * CHUNK

        def score_tile(kt, carry):
            off = pl.multiple_of(kt * KEY_TILE, KEY_TILE)
            ki_t = ki_ref[0, pl.ds(off, KEY_TILE), :]
            part = jnp.zeros((qb, KEY_TILE), F32)
            hpi = min(INDEXER_HEADS_PER_DOT, idx_heads)
            for g in range(idx_heads // hpi):
                qg = qi_ref[0, g * hpi:(g + 1) * hpi].reshape(hpi * qb, di)
                s = lax.dot_general(qg, ki_t, nt_dims, preferred_element_type=F32)
                s = jnp.maximum(s, 0.0)
                for hh in range(hpi):
                    h = g * hpi + hh
                    part = part + s[hh * qb:(hh + 1) * qb] * wi_ref[:, h:h + 1]
            bits = pltpu.bitcast(part, I32)
            key = bits ^ ((bits >> 31) & 0x7FFFFFFF)
            s_idx = off + lax.broadcasted_iota(I32, (qb, KEY_TILE), 1)
            adm = (s_idx >= pad_front) & (s_idx < limit)
            keyp_ref[:, pl.ds(off, KEY_TILE)] = jnp.where(adm, key, INT_MIN)
            return carry

        lax.fori_loop(0, n_t, score_tile, 0)

        def count_ge(thr):
            def body(kt, cnt):
                off = pl.multiple_of(kt * KEY_TILE, KEY_TILE)
                ge = jnp.where(keyp_ref[:, pl.ds(off, KEY_TILE)] >= thr, 1.0, 0.0)
                for u in range(KEY_TILE // LANES):
                    cnt = cnt + ge[:, u * LANES:(u + 1) * LANES]
                return cnt
            cnt = lax.fori_loop(0, n_t, body, jnp.zeros((qb, LANES), F32))
            return jnp.sum(cnt, axis=1, keepdims=True)

        def bit_step(bi, thr):
            cand = thr + lax.shift_left(jnp.int32(1), 31 - bi)
            return jnp.where(count_ge(cand) >= k_sel, cand, thr)

        thr = lax.fori_loop(0, 32, bit_step, jnp.full((qb, 1), INT_MIN, I32))
        thr = jnp.maximum(thr, INT_MIN + 1)

        def bias_tile(kt, carry):
            off = pl.multiple_of(kt * KEY_TILE, KEY_TILE)
            sel = keyp_ref[:, pl.ds(off, KEY_TILE)] >= thr
            bias_ref[:, pl.ds(off, KEY_TILE)] = jnp.where(sel, -shift, NEG)
            return carry

        lax.fori_loop(0, n_t, bias_tile, 0)

        @pl.when(jnp.max(count_ge(thr)) > k_sel)
        def _ties():
            quota = k_sel - count_ge(thr + 1)
            upper = (lax.broadcasted_iota(I32, (LANES, LANES), 0)
                     < lax.broadcasted_iota(I32, (LANES, LANES), 1)).astype(BF16)

            def chunk(ci, seen):
                off = pl.multiple_of(ci * LANES, LANES)
                kp = keyp_ref[:, pl.ds(off, LANES)]
                tie = jnp.where(kp == thr, 1.0, 0.0)
                rank = jnp.dot(tie.astype(BF16), upper, preferred_element_type=F32) + seen
                sel = jnp.where(kp > thr, 1.0, tie * jnp.where(rank < quota, 1.0, 0.0))
                bias_ref[:, pl.ds(off, LANES)] = jnp.where(sel > 0.5, -shift, NEG)
                return seen + jnp.sum(tie, axis=1, keepdims=True)

            lax.fori_loop(0, n_t * (KEY_TILE // LANES), chunk, jnp.zeros((qb, 1), F32))

    l_ref[...] = jnp.zeros(l_ref.shape, F32)
    acc_ref[...] = jnp.zeros(acc_ref.shape, F32)
    hpc = min(HEADS_PER_CHAIN, hps)
    rc = hpc * qb
    k_slabs = KEY_TILE // LANES
    c_slabs = c // LANES

    def masked_logits(ci, off):
        q_c = q_ref[0, ci * hpc:(ci + 1) * hpc].reshape(rc, c)
        s = lax.dot_general(q_c, kk_ref[0, pl.ds(off, KEY_TILE), :], nt_dims, preferred_element_type=F32)
        s = (s.reshape(hpc, qb, KEY_TILE) + bias_ref[:, pl.ds(off, KEY_TILE)][None]).reshape(rc, KEY_TILE)
        return [s[:, u * LANES:(u + 1) * LANES] for u in range(k_slabs)]

    def project_out(inv_l):
        for hh in range(hps):
            rs = slice(hh * qb, (hh + 1) * qb)
            o = jnp.concatenate([acc_ref[rs, u * LANES:(u + 1) * LANES] * inv_l[rs] for u in range(c_slabs)],
                                axis=1)
            y = jnp.dot(o.astype(BF16), wuv_ref[hh], preferred_element_type=F32)
            o_ref[:, hh * v:(hh + 1) * v] = y.astype(o_ref.dtype)

    @pl.when(use_shift)
    def _static_shift():
        def kv_step(kt, carry):
            off = pl.multiple_of(kt * KEY_TILE, KEY_TILE)
            lat_t = lat_ref[0, pl.ds(off, KEY_TILE), :]
            for ci in range(hps // hpc):
                rs = slice(ci * rc, (ci + 1) * rc)
                ps = [jnp.exp(sl) for sl in masked_logits(ci, off)]
                psum = ps[0]
                for pu in ps[1:]:
                    psum = psum + pu
                l_ref[rs, :] = l_ref[rs, :] + psum
                pv = jnp.dot(jnp.concatenate(ps, axis=1).astype(BF16), lat_t, preferred_element_type=F32)
                for u in range(c_slabs):
                    cs = slice(u * LANES, (u + 1) * LANES)
                    acc_ref[rs, cs] = acc_ref[rs, cs] + pv[:, cs]
            return carry

        lax.fori_loop(0, n_t, kv_step, 0)
        project_out(jnp.broadcast_to(1.0 / jnp.sum(l_ref[...], axis=1, keepdims=True), l_ref.shape))

    @pl.when(jnp.logical_not(use_shift))
    def _online():
        m_ref[...] = jnp.full(m_ref.shape, -3e38, F32)

        def kv_step(kt, carry):
            off = pl.multiple_of(kt * KEY_TILE, KEY_TILE)
            lat_t = lat_ref[0, pl.ds(off, KEY_TILE), :]
            for ci in range(hps // hpc):
                rs = slice(ci * rc, (ci + 1) * rc)
                slabs = masked_logits(ci, off)
                mx = slabs[0]
                for sl in slabs[1:]:
                    mx = jnp.maximum(mx, sl)
                m_prev = m_ref[rs, :]
                m_new = jnp.maximum(m_prev, jnp.max(mx, axis=1, keepdims=True))
                alpha = jnp.exp(m_prev - m_new)
                ps = [jnp.exp(sl - m_new) for sl in slabs]
                psum = ps[0]
                for pu in ps[1:]:
                    psum = psum + pu
                l_ref[rs, :] = alpha * l_ref[rs, :] + jnp.sum(psum, axis=1, keepdims=True)
                pv = jnp.dot(jnp.concatenate(ps, axis=1).astype(BF16), lat_t, preferred_element_type=F32)
                for u in range(c_slabs):
                    cs = slice(u * LANES, (u + 1) * LANES)
                    acc_ref[rs, cs] = alpha * acc_ref[rs, cs] + pv[:, cs]
                m_ref[rs, :] = m_new
            return carry

        lax.fori_loop(0, n_t, kv_step, 0)
        project_out(1.0 / l_ref[...])


def _sparse_attention(logit_bound, q, qi, widx, ki, kk, lat, w_uv, *, batch, tp, k_sel, pad_front):
    nblk, n_heads, qb, c = q.shape
    usable = 2.0 * logit_bound <= MAX_EXP_SPAN
    shift_info = jnp.stack([jnp.where(usable, logit_bound, 0.0), usable.astype(F32)]).astype(F32)
    idx_heads, di = qi.shape[1], qi.shape[3]
    tk = ki.shape[1]
    v = w_uv.shape[2]
    nq = tp // qb
    hps = min(HEADS_PER_STEP, n_heads)
    rows = hps * qb
    return pl.pallas_call(
        functools.partial(_attn_kernel, k_sel=k_sel, pad_front=pad_front, idx_heads=idx_heads),
        grid=(batch, nq, n_heads // hps),
        in_specs=[pl.BlockSpec(memory_space=pltpu.SMEM),
                  pl.BlockSpec((1, hps, qb, c), lambda b, j, h: (b * nq + j, h, 0, 0)),
                  pl.BlockSpec((1, idx_heads, qb, di), lambda b, j, h: (b * nq + j, 0, 0, 0)),
                  pl.BlockSpec((qb, LANES), lambda b, j, h: (b * nq + j, 0)),
                  pl.BlockSpec((1, tk, di), lambda b, j, h: (b, 0, 0), pipeline_mode=pl.Buffered(1)),
                  pl.BlockSpec((1, tk, c), lambda b, j, h: (b, 0, 0), pipeline_mode=pl.Buffered(1)),
                  pl.BlockSpec((1, tk, c), lambda b, j, h: (b, 0, 0), pipeline_mode=pl.Buffered(1)),
                  pl.BlockSpec((hps, c, v), lambda b, j, h: (h, 0, 0))],
        out_specs=pl.BlockSpec((qb, hps * v), lambda b, j, h: (b * nq + j, h)),
        out_shape=jax.ShapeDtypeStruct((nblk * qb, n_heads * v), BF16),
        scratch_shapes=[pltpu.VMEM((qb, tk), F32), pltpu.VMEM((qb, tk), I32),
                        pltpu.VMEM((rows, c), F32), pltpu.VMEM((rows, LANES), F32),
                        pltpu.VMEM((rows, LANES), F32)],
        name="sparse_attention",
        compiler_params=_params("parallel", "arbitrary", "arbitrary"),
    )(shift_info, q, qi, widx, ki, kk, lat, w_uv)


def _merge_kernel(yp_ref, ya_ref, wp_ref, wa_ref, gp_ref, ga_ref, o_ref):
    pool = jnp.dot(yp_ref[...], wp_ref[...], preferred_element_type=F32)
    attn = jnp.dot(ya_ref[...], wa_ref[...], preferred_element_type=F32)
    o_ref[...] = (gp_ref[...].astype(F32) * pool + ga_ref[...].astype(F32) * attn).astype(o_ref.dtype)


def _merge(y_pool, y_attn, w_bp, w_ba, gates):
    n, pw = y_pool.shape
    aw = y_attn.shape[1]
    d = w_bp.shape[1]
    tm = _tile(n, 768, 128)
    tn = 512
    return pl.pallas_call(
        _merge_kernel,
        grid=(n // tm, d // tn),
        in_specs=[pl.BlockSpec((tm, pw), lambda i, j: (i, 0)),
                  pl.BlockSpec((tm, aw), lambda i, j: (i, 0)),
                  pl.BlockSpec((pw, tn), lambda i, j: (0, j)),
                  pl.BlockSpec((aw, tn), lambda i, j: (0, j)),
                  pl.BlockSpec((tm, tn), lambda i, j: (i, j)),
                  pl.BlockSpec((tm, tn), lambda i, j: (i, j + d // tn))],
        out_specs=pl.BlockSpec((tm, tn), lambda i, j: (i, j)),
        out_shape=jax.ShapeDtypeStruct((n, d), BF16),
        name="branch_merge",
        compiler_params=_params("parallel", "arbitrary"),
    )(y_pool, y_attn, w_bp, w_ba, gates, gates)


def _out_proj_kernel(x_ref, w_ref, r_ref, o_ref):
    o_ref[...] = r_ref[...] + jnp.dot(x_ref[...], w_ref[...], preferred_element_type=F32)


def _out_proj(merged, w_out, resid):
    n, k = merged.shape
    d = w_out.shape[1]
    tm = _tile(n, 768, 128)
    tn = _tile(d, 1024, LANES)
    return pl.pallas_call(
        _out_proj_kernel,
        grid=(n // tm, d // tn),
        in_specs=[pl.BlockSpec((tm, k), lambda i, j: (i, 0)),
                  pl.BlockSpec((k, tn), lambda i, j: (0, j)),
                  pl.BlockSpec((tm, tn), lambda i, j: (i, j))],
        out_specs=pl.BlockSpec((tm, tn), lambda i, j: (i, j)),
        out_shape=jax.ShapeDtypeStruct((n, d), F32),
        name="out_proj_residual",
        compiler_params=_params("parallel", "arbitrary"),
    )(merged, w_out, resid)


def _pack_halves(x):
    w = x.shape[1] // 2
    lo = pltpu.bitcast(x[:, :w].astype(BF16).astype(F32), I32)
    hi = pltpu.bitcast(x[:, w:].astype(BF16).astype(F32), I32)
    return ((lo >> 16) & 0xFFFF) | (hi & -65536)


def _unpack_halves(p):
    return pltpu.bitcast(p << 16, F32), pltpu.bitcast(p & -65536, F32)


def _router_kernel(h_ref, g_ref, rw_ref, rb_ref, xp_o, e_o, w_o, rank_o, cnt_o, carry_ref):
    i = pl.program_id(0)

    @pl.when(i == 0)
    def _():
        carry_ref[...] = jnp.zeros_like(carry_ref)

    x = h_ref[...]
    xn = x * lax.rsqrt(jnp.mean(x * x, axis=-1, keepdims=True) + EPS) * g_ref[...]
    xp_o[...] = _pack_halves(xn)
    logits = jnp.dot(xn.astype(BF16), rw_ref[...], preferred_element_type=F32) + rb_ref[...]
    tm = logits.shape[0]
    lane = lax.broadcasted_iota(I32, (tm, LANES), 1).astype(F32)
    vals, hots = [], []
    cur = logits
    for k in range(TOP_K_EXPERTS):
        m = jnp.max(cur, axis=1, keepdims=True)
        idx = jnp.min(jnp.where(cur == m, lane, float(LANES)), axis=1, keepdims=True)
        hot = lane == idx
        vals.append(m)
        hots.append(hot)
        e_o[:, k:k + 1] = idx.astype(I32)
        cur = jnp.where(hot, -jnp.inf, cur)
    exps = [jnp.exp(vk - vals[0]) for vk in vals]
    denom = exps[0]
    for ek in exps[1:]:
        denom = denom + ek
    for k in range(TOP_K_EXPERTS):
        w_o[:, k:k + 1] = exps[k] / denom
    onehot = jnp.zeros((tm, LANES), F32)
    for hot in hots:
        onehot = onehot + jnp.where(hot, 1.0, 0.0)
    lower = (lax.broadcasted_iota(I32, (tm, tm), 0) > lax.broadcasted_iota(I32, (tm, tm), 1)).astype(BF16)
    before = jnp.dot(lower, onehot.astype(BF16), preferred_element_type=F32) + carry_ref[...]
    for k in range(TOP_K_EXPERTS):
        rank_o[:, k:k + 1] = jnp.sum(jnp.where(hots[k], before, 0.0), axis=1, keepdims=True).astype(I32)
    carry_ref[...] = carry_ref[...] + jnp.sum(onehot, axis=0, keepdims=True)
    cnt_o[...] = carry_ref[...]


def _router(h2, norm2_g, rw, rb):
    n, d = h2.shape
    tm = _tile(n, ROUTE_TILE, Q_BLOCK)
    kk = TOP_K_EXPERTS
    return pl.pallas_call(
        _router_kernel,
        grid=(n // tm,),
        in_specs=[pl.BlockSpec((tm, d), lambda i: (i, 0)),
                  pl.BlockSpec((1, d), lambda i: (0, 0)),
                  pl.BlockSpec((d, LANES), lambda i: (0, 0)),
                  pl.BlockSpec((1, LANES), lambda i: (0, 0))],
        out_specs=[pl.BlockSpec((tm, d // 2), lambda i: (i, 0)),
                   pl.BlockSpec((tm, kk), lambda i: (i, 0)),
                   pl.BlockSpec((tm, kk), lambda i: (i, 0)),
                   pl.BlockSpec((tm, kk), lambda i: (i, 0)),
                   pl.BlockSpec((1, LANES), lambda i: (0, 0))],
        out_shape=[jax.ShapeDtypeStruct((n, d // 2), I32),
                   jax.ShapeDtypeStruct((n, kk), I32),
                   jax.ShapeDtypeStruct((n, kk), F32),
                   jax.ShapeDtypeStruct((n, kk), I32),
                   jax.ShapeDtypeStruct((1, LANES), F32)],
        scratch_shapes=[pltpu.VMEM((1, LANES), F32)],
        name="moe_router",
        compiler_params=_params("arbitrary"),
    )(h2, norm2_g.reshape(1, d), rw, rb)


def _row_copy(src_ref, src_row, dst_ref, dst_row, sem):
    return pltpu.make_async_copy(src_ref.at[pl.ds(src_row, 1), :], dst_ref.at[pl.ds(dst_row, 1), :], sem)


def _dispatch_kernel(dest_ref, pad_start_ref, pad_count_ref, xp_ref, xs_ref, zero_ref, sem, zero_sem, tail_sem):
    tm = xp_ref.shape[0]
    base = pl.program_id(0) * tm * TOP_K_EXPERTS

    def issue(r, carry):
        for k in range(TOP_K_EXPERTS):
            _row_copy(xp_ref, r, xs_ref, dest_ref[base + r * TOP_K_EXPERTS + k], sem).start(priority=k % 2)
        return carry

    lax.fori_loop(0, tm, issue, 0, unroll=4)
    for k in range(TOP_K_EXPERTS):
        pltpu.make_async_copy(xp_ref, xs_ref.at[pl.ds(0, tm), :], sem).wait()

    @pl.when(pl.program_id(0) == pl.num_programs(0) - 1)
    def _():
        zero_ref[...] = jnp.zeros_like(zero_ref)

        def per_expert(e, total):
            def zero_row(r, carry):
                _row_copy(zero_ref, 0, xs_ref, pad_start_ref[e] + r, zero_sem).start()
                return carry
            lax.fori_loop(0, pad_count_ref[e], zero_row, 0)
            return total + pad_count_ref[e]

        n_exp = pad_start_ref.shape[0]
        total = lax.fori_loop(0, n_exp, per_expert, 0)

        def zero_done(r, carry):
            _row_copy(zero_ref, 0, xs_ref, 0, zero_sem).wait()
            return carry

        lax.fori_loop(0, total, zero_done, 0)

        tile = zero_ref.shape[0]
        first = (pad_start_ref[n_exp - 1] + pad_count_ref[n_exp - 1]) // tile
        n_tail = xs_ref.shape[0] // tile - first

        def tile_copy(i):
            row0 = pl.multiple_of((first + i) * tile, tile)
            return pltpu.make_async_copy(zero_ref, xs_ref.at[pl.ds(row0, tile), :], tail_sem)

        def zero_tile(i, carry):
            tile_copy(i).start()
            return carry

        def tile_done(i, carry):
            tile_copy(i).wait()
            return carry

        lax.fori_loop(0, n_tail, zero_tile, 0)
        lax.fori_loop(0, n_tail, tile_done, 0)


def _dispatch(dest_flat, pad_start, pad_count, xp, n_slots):
    n, w = xp.shape
    tm = _tile(n, ROUTE_TILE, Q_BLOCK)
    return pl.pallas_call(
        _dispatch_kernel,
        grid_spec=pltpu.PrefetchScalarGridSpec(
            num_scalar_prefetch=3,
            grid=(n // tm,),
            in_specs=[pl.BlockSpec((tm, w), lambda i, *_: (i, 0))],
            out_specs=pl.BlockSpec(memory_space=pl.ANY),
            scratch_shapes=[pltpu.VMEM((EXPERT_TILE, w), I32), pltpu.SemaphoreType.DMA(()),
                            pltpu.SemaphoreType.DMA(()), pltpu.SemaphoreType.DMA(())]),
        out_shape=jax.ShapeDtypeStruct((n_slots, w), I32),
        name="moe_dispatch",
        compiler_params=_params("arbitrary"),
    )(dest_flat, pad_start, pad_count, xp)


def _deinterleave_kernel(w_ref, o_ref):
    blk = 2 * LANES
    dst = lax.broadcasted_iota(I32, (blk, blk), 1)
    src = jnp.where(dst < LANES, 2 * dst, 2 * (dst - LANES) + 1)
    perm = (lax.broadcasted_iota(I32, (blk, blk), 0) == src).astype(BF16)
    for b in range(w_ref.shape[2] // blk):
        cols = slice(b * blk, (b + 1) * blk)
        o_ref[0, :, cols] = jnp.dot(w_ref[0, :, cols].astype(BF16), perm,
                                    preferred_element_type=F32).astype(o_ref.dtype)


def _deinterleave_gate_up(w_gate_up):
    n_exp, d, de2 = w_gate_up.shape
    assert de2 % (2 * LANES) == 0
    tk = _tile(d, 512, 16)
    return pl.pallas_call(
        _deinterleave_kernel,
        grid=(n_exp, d // tk),
        in_specs=[pl.BlockSpec((1, tk, de2), lambda e, i: (e, i, 0))],
        out_specs=pl.BlockSpec((1, tk, de2), lambda e, i: (e, i, 0)),
        out_shape=jax.ShapeDtypeStruct((n_exp, d, de2), BF16),
        name="moe_weight_regroup",
        compiler_params=_params("parallel", "arbitrary"),
    )(w_gate_up)


def _expert_up_kernel(te_ref, tv_ref, xs_ref, w_ref, b_ref, o_ref):
    t = pl.program_id(0)

    @pl.when(tv_ref[t] > 0)
    def _():
        lo, hi = _unpack_halves(xs_ref[...])
        x = jnp.concatenate([lo.astype(BF16), hi.astype(BF16)], axis=1)
        gu = jnp.dot(x, w_ref[0], preferred_element_type=F32) + b_ref[0]
        for blk in range(gu.shape[1] // (2 * LANES)):
            g0 = blk * 2 * LANES
            gate = jnp.minimum(gu[:, g0:g0 + LANES], SWIGLU_LIMIT)
            up = jnp.clip(gu[:, g0 + LANES:g0 + 2 * LANES], -SWIGLU_LIMIT, SWIGLU_LIMIT)
            act = (up + 1.0) * gate * jax.nn.sigmoid(SWIGLU_ALPHA * gate)
            o_ref[:, blk * LANES:(blk + 1) * LANES] = act.astype(o_ref.dtype)

    @pl.when(tv_ref[t] == 0)
    def _():
        o_ref[...] = jnp.zeros_like(o_ref)


def _expert_up(tile_expert, tile_valid, xs, w_gu, b_gu):
    n_slots, w = xs.shape
    n_exp, d, de2 = w_gu.shape
    tm = EXPERT_TILE
    return pl.pallas_call(
        _expert_up_kernel,
        grid_spec=pltpu.PrefetchScalarGridSpec(
            num_scalar_prefetch=2,
            grid=(n_slots // tm,),
            in_specs=[pl.BlockSpec((tm, w), lambda t, te, tv: (t, 0)),
                      pl.BlockSpec((1, d, de2), lambda t, te, tv: (te[t], 0, 0)),
                      pl.BlockSpec((1, 1, de2), lambda t, te, tv: (te[t], 0, 0))],
            out_specs=pl.BlockSpec((tm, de2 // 2), lambda t, te, tv: (t, 0))),
        out_shape=jax.ShapeDtypeStruct((n_slots, de2 // 2), BF16),
        name="moe_expert_up",
        compiler_params=_params("arbitrary"),
    )(tile_expert, tile_valid, xs, w_gu, b_gu)


def _expert_down_kernel(te_ref, tv_ref, a_ref, w_ref, b_ref, o_ref, wbf_ref):
    t = pl.program_id(0)

    @pl.when(jnp.logical_or(t == 0, te_ref[t] != te_ref[jnp.maximum(t - 1, 0)]))
    def _():
        wbf_ref[...] = w_ref[0].astype(BF16)

    @pl.when(tv_ref[t] > 0)
    def _():
        y = jnp.dot(a_ref[...], wbf_ref[...], preferred_element_type=F32) + b_ref[0]
        o_ref[...] = _pack_halves(y)

    @pl.when(tv_ref[t] == 0)
    def _():
        o_ref[...] = jnp.zeros_like(o_ref)


def _expert_down(tile_expert, tile_valid, act, w_down, b_down):
    n_slots, de = act.shape
    n_exp, _, d = w_down.shape
    tm = EXPERT_TILE
    return pl.pallas_call(
        _expert_down_kernel,
        grid_spec=pltpu.PrefetchScalarGridSpec(
            num_scalar_prefetch=2,
            grid=(n_slots // tm,),
            in_specs=[pl.BlockSpec((tm, de), lambda t, te, tv: (t, 0)),
                      pl.BlockSpec((1, de, d), lambda t, te, tv: (te[t], 0, 0)),
                      pl.BlockSpec((1, 1, d), lambda t, te, tv: (te[t], 0, 0))],
            out_specs=pl.BlockSpec((tm, d // 2), lambda t, te, tv: (t, 0)),
            scratch_shapes=[pltpu.VMEM((de, d), BF16)]),
        out_shape=jax.ShapeDtypeStruct((n_slots, d // 2), I32),
        name="moe_expert_down",
        compiler_params=_params("arbitrary"),
    )(tile_expert, tile_valid, act, w_down, b_down)


def _combine_kernel(dest_ref, w_ref, h_ref, ye_ref, o_ref, buf_ref, sems, *, nq):
    b = pl.program_id(0)
    i = pl.program_id(1)
    n_i = pl.num_programs(1)
    n_steps = pl.num_programs(0) * n_i
    tm = h_ref.shape[0]
    half = buf_ref.shape[3]
    step = b * n_i + i
    slot = step % 2

    def start_rows(bb, ii, sl):
        base = ((bb * nq + 1 + ii) * tm) * TOP_K_EXPERTS

        def issue(r, carry):
            for k in range(TOP_K_EXPERTS):
                _row_copy(ye_ref, dest_ref[base + r * TOP_K_EXPERTS + k], buf_ref.at[sl, k], r,
                          sems.at[sl]).start(priority=k % 2)
            return carry

        lax.fori_loop(0, tm, issue, 0, unroll=4)

    @pl.when(step == 0)
    def _():
        start_rows(b, i, slot)

    @pl.when(step + 1 < n_steps)
    def _():
        wrap = i + 1 == n_i
        start_rows(jnp.where(wrap, b + 1, b), jnp.where(wrap, 0, i + 1), 1 - slot)

    for k in range(TOP_K_EXPERTS):
        pltpu.make_async_copy(ye_ref.at[pl.ds(0, tm), :], buf_ref.at[slot, k], sems.at[slot]).wait()

    lo_sum = h_ref[:, :half]
    hi_sum = h_ref[:, half:]
    for k in range(TOP_K_EXPERTS):
        lo, hi = _unpack_halves(buf_ref[slot, k])
        wk = w_ref[:, k:k + 1]
        lo_sum = lo_sum + wk * lo
        hi_sum = hi_sum + wk * hi
    o_ref[0, :, :half] = lo_sum
    o_ref[0, :, half:] = hi_sum


def _combine(dest_flat, sel_w, h2, ye, *, batch, seq, tp):
    d = h2.shape[1]
    tm = Q_BLOCK
    nq = tp // tm
    return pl.pallas_call(
        functools.partial(_combine_kernel, nq=nq),
        grid_spec=pltpu.PrefetchScalarGridSpec(
            num_scalar_prefetch=1,
            grid=(batch, seq // tm),
            in_specs=[pl.BlockSpec((tm, TOP_K_EXPERTS), lambda b, i, dest: (b * nq + 1 + i, 0)),
                      pl.BlockSpec((tm, d), lambda b, i, dest: (b * nq + 1 + i, 0)),
                      pl.BlockSpec(memory_space=pl.ANY)],
            out_specs=pl.BlockSpec((1, tm, d), lambda b, i, dest: (b, i, 0)),
            scratch_shapes=[pltpu.VMEM((2, TOP_K_EXPERTS, tm, d // 2), I32), pltpu.SemaphoreType.DMA((2,))]),
        out_shape=jax.ShapeDtypeStruct((batch, seq, d), F32),
        name="moe_combine",
        compiler_params=_params("arbitrary", "arbitrary"),
    )(dest_flat, sel_w, h2, ye)


def _moe(h2, norm2_g, router_w, router_b, w_gate_up, b_gate_up, w_down, b_down, *, batch, seq, tp):
    n, d = h2.shape
    n_exp = router_w.shape[1]
    de = w_down.shape[1]
    assert n_exp <= LANES
    rw = jnp.pad(router_w, ((0, 0), (0, LANES - n_exp))).astype(BF16)
    rb = jnp.pad(router_b, (0, LANES - n_exp), constant_values=NEG).reshape(1, LANES)
    xp, sel_e, sel_w, rank, counts = _router(h2, norm2_g, rw, rb)

    cnt = counts[0, :n_exp].astype(I32)
    padded = ((cnt + EXPERT_TILE - 1) // EXPERT_TILE) * EXPERT_TILE
    e_ids = jnp.arange(n_exp, dtype=I32)
    ends = jnp.sum(jnp.where(e_ids[:, None] <= e_ids[None, :], padded[:, None], 0), axis=0)
    starts = ends - padded
    total = ends[n_exp - 1]
    dest_flat = (starts[sel_e] + rank).reshape(-1)
    n_tiles = (n * TOP_K_EXPERTS) // EXPERT_TILE + n_exp
    tile_start = jnp.arange(n_tiles, dtype=I32) * EXPERT_TILE
    tile_valid = (tile_start < total).astype(I32)
    last_expert = jnp.sum((ends <= total - 1).astype(I32))
    tile_expert = jnp.minimum(jnp.sum((ends[None, :] <= tile_start[:, None]).astype(I32), axis=1), last_expert)

    w_gu = _deinterleave_gate_up(w_gate_up)
    nb = (2 * de) // (2 * LANES)
    b_gu = b_gate_up.reshape(n_exp, nb, LANES, 2).transpose(0, 1, 3, 2).reshape(n_exp, 1, 2 * de)

    xs = _dispatch(dest_flat, starts + cnt, padded - cnt, xp, n_tiles * EXPERT_TILE)
    act = _expert_up(tile_expert, tile_valid, xs, w_gu, b_gu)
    ye = _expert_down(tile_expert, tile_valid, act, w_down, b_down.reshape(n_exp, 1, d))
    return _combine(dest_flat, sel_w, h2, ye, batch=batch, seq=seq, tp=tp)


def kernel(x, meta_tokens, norm1_g, w_in, q_norm_g, w_uq, w_iq, kv_norm_g, q_head_norm_g, k_head_norm_g,
           idx_k_norm_g, w_uv, w_branch_attn, w_pool, pool_scale, w_branch_pool, w_out, norm2_g, router_w,
           router_b, w_gate_up, b_gate_up, w_down, b_down):
    batch, seq, d = x.shape
    n_meta = meta_tokens.shape[0]
    depth = norm1_g.shape[0]
    qr, kvr, di = q_norm_g.shape[1], kv_norm_g.shape[1], idx_k_norm_g.shape[1]
    n_heads, v = w_uv.shape[1], w_uv.shape[3]
    idx_heads = w_iq.shape[2] // di
    pw = pool_scale.shape[1]
    assert depth == 1
    assert n_meta <= CHUNK and seq % Q_BLOCK == 0 and idx_heads <= LANES
    assert qr % kvr == 0 and (qr + kvr) % di == 0 and di % LANES == 0
    pad_front = Q_BLOCK - n_meta
    tp = Q_BLOCK + seq
    tk = -(-tp // KEY_TILE) * KEY_TILE
    k_sel = min(TOPK_MAX, seq // 4)
    attn_scale = kvr ** -0.5
    idx_scale = (di ** -0.5) * (idx_heads ** -0.5)

    head_rows = jnp.concatenate([jnp.zeros((pad_front, d), x.dtype), meta_tokens.astype(x.dtype)], axis=0)

    out = None
    for l in range(depth):
        n_small = qr + kvr + di
        o_pool = n_small + idx_heads
        o_gate = o_pool + pw
        w_small = jnp.pad(w_in[l][:, :o_pool], ((0, 0), (0, LANES - idx_heads))).astype(BF16)
        w_a = w_in[l][:, o_pool:o_gate].astype(BF16)
        w_g = w_in[l][:, o_gate:].astype(BF16)

        h, xn = _embed_norm(x, head_rows, norm1_g[l])
        small = _matmul(xn, w_small, F32, tn=_tile(w_small.shape[1], 1024, LANES), name="in_proj_latents")
        a_pool = _matmul(xn, w_a, BF16, tn=_tile(pw, 1024, LANES), name="in_proj_pool")
        gates = _matmul(xn, w_g, BF16, tn=_tile(2 * d, 1024, LANES), sigmoid=True, name="in_proj_gates")

        y_pool = _pool_mixer(a_pool, w_pool[l].astype(BF16), pool_scale[l], batch=batch, tp=tp,
                             pad_front=pad_front)

        cq, widx = _prep_q(small, q_norm_g[l], qr, n_small, idx_scale)
        lat, kk, ki = _prep_kv(small, kv_norm_g[l], k_head_norm_g[l], idx_k_norm_g[l],
                               batch=batch, tp=tp, tk=tk, qr=qr, kvr=kvr, di=di)
        q = _qproj(cq, w_uq[l].astype(BF16), q_head_norm_g[l], n_heads=n_heads, hd=kvr,
                   heads_per_tile=min(2, n_heads),
                   scale=attn_scale, norm=True, name="q_proj_headnorm")
        qi = _qproj(cq, w_iq[l].astype(BF16), jnp.ones((di,), F32), n_heads=idx_heads, hd=di,
                    heads_per_tile=min(8, idx_heads), scale=1.0, norm=False, name="indexer_q_proj")
        logit_bound = (BOUND_MARGIN * kvr * attn_scale * jnp.max(jnp.abs(q_head_norm_g[l]))
                       * jnp.max(jnp.abs(k_head_norm_g[l])))
        y_attn = _sparse_attention(logit_bound, q, qi, widx, ki, kk, lat, w_uv[l].astype(BF16), batch=batch,
                                   tp=tp, k_sel=k_sel, pad_front=pad_front)

        merged = _merge(y_pool, y_attn, w_branch_pool[l].astype(BF16), w_branch_attn[l].astype(BF16), gates)
        h2 = _out_proj(merged, w_out[l].astype(BF16), h)

        out = _moe(h2, norm2_g[l], router_w[l], router_b[l], w_gate_up[l], b_gate_up[l], w_down[l],
                   b_down[l], batch=batch, seq=seq, tp=tp)
    return out
```

```python
import functools

import jax
import jax.numpy as jnp
from jax import lax
from jax.experimental import pallas as pl
from jax.experimental.pallas import tpu as pltpu

F32 = jnp.float32
BF16 = jnp.bfloat16
I32 = jnp.int32

CHUNK = 64
POOL_WINDOWS = (2, 4, 8, 16)
TOPK_MAX = 256
TOP_K_EXPERTS = 4
SWIGLU_LIMIT = 7.0
SWIGLU_ALPHA = 1.702
EPS = 1e-6
NEG = -1e30

Q_BLOCK = 128
KEY_TILE = 512
HEADS_PER_STEP = 32
INDEXER_HEADS_PER_DOT = 8
SELECT_ROWS = 384
HEADS_PER_CHAIN = 4
HALO = 16
EXPERT_TILE = 256
ROUTE_TILE = 512
LANES = 128
INT_MIN = -2 ** 31
MAX_EXP_SPAN = 80.0
BOUND_MARGIN = 1.02
V7X_VMEM_BYTES = 64 * 1024 * 1024
VMEM_LIMIT = V7X_VMEM_BYTES * 7 // 8


def _tile(n, target, mult):
    best = None
    for t in range(mult, min(n, target) + 1, mult):
        if n % t == 0:
            best = t
    assert best is not None, (n, target, mult)
    return best


def _params(*sem):
    return pltpu.CompilerParams(dimension_semantics=sem, vmem_limit_bytes=VMEM_LIMIT)


def _embed_norm_kernel(x_ref, head_ref, g_ref, h_ref, xn_ref):
    rows = jnp.where(pl.program_id(1) == 0, head_ref[...], x_ref[0])
    h_ref[...] = rows
    ms = jnp.mean(rows * rows, axis=-1, keepdims=True)
    xn_ref[...] = (rows * lax.rsqrt(ms + EPS) * g_ref[...]).astype(xn_ref.dtype)


def _embed_norm(x, head_rows, g):
    batch, seq, d = x.shape
    nq = seq // Q_BLOCK + 1
    n = batch * nq * Q_BLOCK
    return pl.pallas_call(
        _embed_norm_kernel,
        grid=(batch, nq),
        in_specs=[pl.BlockSpec((1, Q_BLOCK, d), lambda b, i: (b, jnp.maximum(i - 1, 0), 0)),
                  pl.BlockSpec((Q_BLOCK, d), lambda b, i: (0, 0)),
                  pl.BlockSpec((1, d), lambda b, i: (0, 0))],
        out_specs=[pl.BlockSpec((Q_BLOCK, d), lambda b, i: (b * nq + i, 0)),
                   pl.BlockSpec((Q_BLOCK, d), lambda b, i: (b * nq + i, 0))],
        out_shape=[jax.ShapeDtypeStruct((n, d), x.dtype), jax.ShapeDtypeStruct((n, d), BF16)],
        name="embed_rmsnorm",
        compiler_params=_params("parallel", "arbitrary"),
    )(x, head_rows, g.reshape(1, d))


def _matmul_kernel(x_ref, w_ref, o_ref, *, sigmoid):
    acc = jnp.dot(x_ref[...], w_ref[...], preferred_element_type=F32)
    if sigmoid:
        acc = jax.nn.sigmoid(acc)
    o_ref[...] = acc.astype(o_ref.dtype)


def _matmul(x, w, out_dtype, *, tn, name, sigmoid=False):
    m, k = x.shape
    n = w.shape[1]
    tm = _tile(m, 768, 128)
    return pl.pallas_call(
        functools.partial(_matmul_kernel, sigmoid=sigmoid),
        grid=(m // tm, n // tn),
        in_specs=[pl.BlockSpec((tm, k), lambda i, j: (i, 0)), pl.BlockSpec((k, tn), lambda i, j: (0, j))],
        out_specs=pl.BlockSpec((tm, tn), lambda i, j: (i, j)),
        out_shape=jax.ShapeDtypeStruct((m, n), out_dtype),
        name=name,
        compiler_params=_params("parallel", "arbitrary"),
    )(x, w)


def _prep_q_kernel(cq_ref, wi_ref, g_ref, cq_o, wi_o, *, idx_scale):
    x = cq_ref[...]
    ms = jnp.mean(x * x, axis=-1, keepdims=True)
    cq_o[...] = (x * lax.rsqrt(ms + EPS) * g_ref[...]).astype(cq_o.dtype)
    wi_o[...] = wi_ref[...] * idx_scale


def _prep_q(small, q_norm_g, qr, wi_col, idx_scale):
    n = small.shape[0]
    tm = _tile(n, 768, 128)
    return pl.pallas_call(
        functools.partial(_prep_q_kernel, idx_scale=idx_scale),
        grid=(n // tm,),
        in_specs=[pl.BlockSpec((tm, qr), lambda i: (i, 0)),
                  pl.BlockSpec((tm, LANES), lambda i: (i, wi_col // LANES)),
                  pl.BlockSpec((1, qr), lambda i: (0, 0))],
        out_specs=[pl.BlockSpec((tm, qr), lambda i: (i, 0)), pl.BlockSpec((tm, LANES), lambda i: (i, 0))],
        out_shape=[jax.ShapeDtypeStruct((n, qr), BF16), jax.ShapeDtypeStruct((n, LANES), F32)],
        name="prep_query_latent",
        compiler_params=_params("parallel"),
    )(small, small, q_norm_g.reshape(1, qr))


def _prep_kv_kernel(ckv_ref, kidx_ref, gkv_ref, gkh_ref, gki_ref, lat_o, kk_o, ki_o, *, n_valid):
    i = pl.program_id(1)

    @pl.when(i < n_valid)
    def _():
        c = ckv_ref[...]
        lat = c * lax.rsqrt(jnp.mean(c * c, axis=-1, keepdims=True) + EPS) * gkv_ref[...]
        kk = lat * lax.rsqrt(jnp.mean(lat * lat, axis=-1, keepdims=True) + EPS) * gkh_ref[...]
        k = kidx_ref[...]
        ki = k * lax.rsqrt(jnp.mean(k * k, axis=-1, keepdims=True) + EPS) * gki_ref[...]
        lat_o[0] = lat.astype(lat_o.dtype)
        kk_o[0] = kk.astype(kk_o.dtype)
        ki_o[0] = ki.astype(ki_o.dtype)

    @pl.when(i >= n_valid)
    def _():
        lat_o[...] = jnp.zeros_like(lat_o)
        kk_o[...] = jnp.zeros_like(kk_o)
        ki_o[...] = jnp.zeros_like(ki_o)


def _prep_kv(small, gkv, gkh, gki, *, batch, tp, tk, qr, kvr, di):
    nq = tp // Q_BLOCK
    nk = tk // Q_BLOCK

    def row(b, i):
        return b * nq + jnp.minimum(i, nq - 1)

    return pl.pallas_call(
        functools.partial(_prep_kv_kernel, n_valid=nq),
        grid=(batch, nk),
        in_specs=[pl.BlockSpec((Q_BLOCK, kvr), lambda b, i: (row(b, i), qr // kvr)),
                  pl.BlockSpec((Q_BLOCK, di), lambda b, i: (row(b, i), (qr + kvr) // di)),
                  pl.BlockSpec((1, kvr), lambda b, i: (0, 0)),
                  pl.BlockSpec((1, kvr), lambda b, i: (0, 0)),
                  pl.BlockSpec((1, di), lambda b, i: (0, 0))],
        out_specs=[pl.BlockSpec((1, Q_BLOCK, kvr), lambda b, i: (b, i, 0)),
                   pl.BlockSpec((1, Q_BLOCK, kvr), lambda b, i: (b, i, 0)),
                   pl.BlockSpec((1, Q_BLOCK, di), lambda b, i: (b, i, 0))],
        out_shape=[jax.ShapeDtypeStruct((batch, tk, kvr), BF16),
                   jax.ShapeDtypeStruct((batch, tk, kvr), BF16),
                   jax.ShapeDtypeStruct((batch, tk, di), BF16)],
        name="prep_keys",
        compiler_params=_params("parallel", "arbitrary"),
    )(small, small, gkv.reshape(1, kvr), gkh.reshape(1, kvr), gki.reshape(1, di))


def _pool_kernel(a_ref, halo_ref, w_ref, sc_ref, o_ref, xs_ref, *, tp_tile, pad_front, pg):
    i = pl.program_id(1)
    t = i * tp_tile + lax.broadcasted_iota(I32, (tp_tile, 1), 0) - pad_front
    for g, win in enumerate(POOL_WINDOWS):
        cols = slice(g * pg, (g + 1) * pg)
        xs_ref[0:HALO, :] = halo_ref[0, :, cols].astype(F32)
        xs_ref[HALO:, :] = a_ref[0, :, cols].astype(F32)
        cur = xs_ref[pl.ds(HALO, tp_tile), :]
        acc = cur
        for k in range(1, win):
            acc = acc + xs_ref[pl.ds(HALO - k, tp_tile), :]
        cnt = jnp.clip(t + 1, 1, win).astype(F32)
        pooled = (acc / cnt - cur).astype(BF16)
        y = jnp.dot(pooled, w_ref[g], preferred_element_type=F32) * sc_ref[:, cols]
        o_ref[0, :, cols] = y.astype(o_ref.dtype)


def _pool_mixer(a, w_pool, pool_scale, *, batch, tp, pad_front):
    pw = a.shape[-1]
    n_groups, pg, _ = w_pool.shape
    tpt = _tile(tp, 1536, HALO)
    a3 = a.reshape(batch, tp, pw)
    halo_blocks = tpt // HALO
    out = pl.pallas_call(
        functools.partial(_pool_kernel, tp_tile=tpt, pad_front=pad_front, pg=pg),
        grid=(batch, tp // tpt),
        in_specs=[pl.BlockSpec((1, tpt, pw), lambda b, i: (b, i, 0)),
                  pl.BlockSpec((1, HALO, pw), lambda b, i: (b, jnp.maximum(i * halo_blocks - 1, 0), 0)),
                  pl.BlockSpec((n_groups, pg, pg), lambda b, i: (0, 0, 0)),
                  pl.BlockSpec((1, pw), lambda b, i: (0, 0))],
        out_specs=pl.BlockSpec((1, tpt, pw), lambda b, i: (b, i, 0)),
        out_shape=jax.ShapeDtypeStruct((batch, tp, pw), BF16),
        scratch_shapes=[pltpu.VMEM((tpt + HALO, pg), F32)],
        name="pool_mixer",
        compiler_params=_params("parallel", "arbitrary"),
    )(a3, a3, w_pool, pool_scale.reshape(1, pw))
    return out.reshape(batch * tp, pw)


def _qproj_kernel(x_ref, w_ref, g_ref, o_ref, *, scale, norm, heads, hd):
    acc = jnp.dot(x_ref[...], w_ref[...], preferred_element_type=F32)
    nblk = o_ref.shape[0]
    for hh in range(heads):
        a = acc[:, hh * hd:(hh + 1) * hd]
        if norm:
            a = a * lax.rsqrt(jnp.mean(a * a, axis=-1, keepdims=True) + EPS) * (g_ref[...] * scale)
        a = a.astype(o_ref.dtype)
        for r in range(nblk):
            o_ref[r, hh] = a[r * Q_BLOCK:(r + 1) * Q_BLOCK]


def _qproj(cq, w, g, *, n_heads, hd, heads_per_tile, scale, norm, name):
    n, r = cq.shape
    tm = _tile(n, 1536, Q_BLOCK)
    nblk = tm // Q_BLOCK
    tn = heads_per_tile * hd
    return pl.pallas_call(
        functools.partial(_qproj_kernel, scale=scale, norm=norm, heads=heads_per_tile, hd=hd),
        grid=(n // tm, n_heads // heads_per_tile),
        in_specs=[pl.BlockSpec((tm, r), lambda i, j: (i, 0)),
                  pl.BlockSpec((r, tn), lambda i, j: (0, j)),
                  pl.BlockSpec((1, hd), lambda i, j: (0, 0))],
        out_specs=pl.BlockSpec((nblk, heads_per_tile, Q_BLOCK, hd), lambda i, j: (i, j, 0, 0)),
        out_shape=jax.ShapeDtypeStruct((n // Q_BLOCK, n_heads, Q_BLOCK, hd), BF16),
        name=name,
        compiler_params=_params("parallel", "arbitrary"),
    )(cq, w, g.reshape(1, hd))


def _select_kernel(shift_ref, qi_ref, wi_ref, ki_ref, bias_ref, keyp_ref, *, k_sel, pad_front, idx_heads):
    j = pl.program_id(1)
    shift = shift_ref[0]
    nsub, qb, di = qi_ref.shape[0], qi_ref.shape[2], qi_ref.shape[3]
    rows = nsub * qb
    n_tiles = keyp_ref.shape[1] // KEY_TILE
    n_t = (j * rows + rows + KEY_TILE - 1) // KEY_TILE
    nt_dims = (((1,), (1,)), ((), ()))
    tq = j * rows + lax.broadcasted_iota(I32, (rows, 1), 0)
    limit = ((jnp.maximum(tq, CHUNK) + CHUNK) // CHUNK) * CHUNK

    def score_tile(kt, carry):
        off = pl.multiple_of(kt * KEY_TILE, KEY_TILE)
        ki_t = ki_ref[0, pl.ds(off, KEY_TILE), :]
        hpi = min(INDEXER_HEADS_PER_DOT, idx_heads)
        parts = []
        for sb in range(nsub):
            part = jnp.zeros((qb, KEY_TILE), F32)
            for g in range(idx_heads // hpi):
                qg = qi_ref[sb, g * hpi:(g + 1) * hpi].reshape(hpi * qb, di)
                s = lax.dot_general(qg, ki_t, nt_dims, preferred_element_type=F32)
                s = jnp.maximum(s, 0.0)
                for hh in range(hpi):
                    h = g * hpi + hh
                    part = part + s[hh * qb:(hh + 1) * qb] * wi_ref[sb * qb:(sb + 1) * qb, h:h + 1]
            parts.append(part)
        bits = pltpu.bitcast(jnp.concatenate(parts, axis=0), I32)
        key = bits ^ ((bits >> 31) & 0x7FFFFFFF)
        s_idx = off + lax.broadcasted_iota(I32, (rows, KEY_TILE), 1)
        adm = (s_idx >= pad_front) & (s_idx < limit)
        keyp_ref[:, pl.ds(off, KEY_TILE)] = jnp.where(adm, key, INT_MIN)
        return carry

    lax.fori_loop(0, n_t, score_tile, 0)

    def count_ge(thr):
        cnts = []
        for sb in range(nsub):
            rs = slice(sb * qb, (sb + 1) * qb)
            thr_sb = thr[rs]

            def body(kt, cnt, rs=rs, thr_sb=thr_sb):
                off = pl.multiple_of(kt * KEY_TILE, KEY_TILE)
                ge = jnp.where(keyp_ref[rs, pl.ds(off, KEY_TILE)] >= thr_sb, 1.0, 0.0)
                for u in range(KEY_TILE // LANES):
                    cnt = cnt + ge[:, u * LANES:(u + 1) * LANES]
                return cnt

            cnts.append(lax.fori_loop(0, n_t, body, jnp.zeros((qb, LANES), F32)))
        return jnp.sum(jnp.concatenate(cnts, axis=0), axis=1, keepdims=True)

    def bit_step(bi, thr):
        cand = thr + lax.shift_left(jnp.int32(1), 31 - bi)
        return jnp.where(count_ge(cand) >= k_sel, cand, thr)

    thr = lax.fori_loop(0, 32, bit_step, jnp.full((rows, 1), INT_MIN, I32))
    thr = jnp.maximum(thr, INT_MIN + 1)

    def bias_tile(kt, carry):
        off = pl.multiple_of(kt * KEY_TILE, KEY_TILE)
        sel = keyp_ref[:, pl.ds(off, KEY_TILE)] >= thr
        bias_ref[:, pl.ds(off, KEY_TILE)] = jnp.where(sel, -shift, NEG)
        return carry

    lax.fori_loop(0, n_t, bias_tile, 0)

    def masked_tile(kt, carry):
        off = pl.multiple_of(kt * KEY_TILE, KEY_TILE)
        bias_ref[:, pl.ds(off, KEY_TILE)] = jnp.full((rows, KEY_TILE), NEG, F32)
        return carry

    lax.fori_loop(n_t, n_tiles, masked_tile, 0)

    @pl.when(jnp.max(count_ge(thr)) > k_sel)
    def _ties():
        quota = k_sel - count_ge(thr + 1)
        upper = (lax.broadcasted_iota(I32, (LANES, LANES), 0)
                 < lax.broadcasted_iota(I32, (LANES, LANES), 1)).astype(BF16)

        def chunk(ci, seen):
            off = pl.multiple_of(ci * LANES, LANES)
            kp = keyp_ref[:, pl.ds(off, LANES)]
            tie = jnp.where(kp == thr, 1.0, 0.0)
            rank = jnp.dot(tie.astype(BF16), upper, preferred_element_type=F32) + seen
            sel = jnp.where(kp > thr, 1.0, tie * jnp.where(rank < quota, 1.0, 0.0))
            bias_ref[:, pl.ds(off, LANES)] = jnp.where(sel > 0.5, -shift, NEG)
            return seen + jnp.sum(tie, axis=1, keepdims=True)

        lax.fori_loop(0, n_t * (KEY_TILE // LANES), chunk, jnp.zeros((rows, 1), F32))


def _select_bias(shift_info, qi, widx, ki, *, batch, tp, k_sel, pad_front):
    nblk, idx_heads, qb, di = qi.shape
    tk = ki.shape[1]
    rows = _tile(tp, SELECT_ROWS, qb)
    nsub = rows // qb
    nsel = tp // rows
    return pl.pallas_call(
        functools.partial(_select_kernel, k_sel=k_sel, pad_front=pad_front, idx_heads=idx_heads),
        grid=(batch, nsel),
        in_specs=[pl.BlockSpec(memory_space=pltpu.SMEM),
                  pl.BlockSpec((nsub, idx_heads, qb, di), lambda b, j: (b * nsel + j, 0, 0, 0)),
                  pl.BlockSpec((rows, LANES), lambda b, j: (b * nsel + j, 0)),
                  pl.BlockSpec((1, tk, di), lambda b, j: (b, 0, 0))],
        out_specs=pl.BlockSpec((rows, tk), lambda b, j: (b * nsel + j, 0)),
        out_shape=jax.ShapeDtypeStruct((nblk * qb, tk), F32),
        scratch_shapes=[pltpu.VMEM((rows, tk), I32)],
        name="indexer_select",
        compiler_params=_params("parallel", "arbitrary"),
    )(shift_info, qi, widx, ki)


def _attn_kernel(shift_ref, q_ref, bias_ref, kk_ref, lat_ref, wuv_ref, o_ref, acc_ref, m_ref, l_ref):
    j = pl.program_id(1)
    use_shift = shift_ref[1] > 0.5
    hps, qb, c = q_ref.shape[1], q_ref.shape[2], q_ref.shape[3]
    v = wuv_ref.shape[2]
    n_t = (j * qb + qb + KEY_TILE - 1) // KEY_TILE
    nt_dims = (((1,), (1,)), ((), ()))

    l_ref[...] = jnp.zeros(l_ref.shape, F32)
    acc_ref[...] = jnp.zeros(acc_ref.shape, F32)
    hpc = min(HEADS_PER_CHAIN, hps)
    rc = hpc * qb
    k_slabs = KEY_TILE // LANES
    c_slabs = c // LANES

    def masked_logits(ci, off):
        q_c = q_ref[0, ci * hpc:(ci + 1) * hpc].reshape(rc, c)
        s = lax.dot_general(q_c, kk_ref[0, pl.ds(off, KEY_TILE), :], nt_dims, preferred_element_type=F32)
        s = (s.reshape(hpc, qb, KEY_TILE) + bias_ref[:, pl.ds(off, KEY_TILE)][None]).reshape(rc, KEY_TILE)
        return [s[:, u * LANES:(u + 1) * LANES] for u in range(k_slabs)]

    def project_out(inv_l):
        for hh in range(hps):
            rs = slice(hh * qb, (hh + 1) * qb)
            o = jnp.concatenate([acc_ref[rs, u * LANES:(u + 1) * LANES] * inv_l[rs] for u in range(c_slabs)],
                                axis=1)
            y = jnp.dot(o.astype(BF16), wuv_ref[hh], preferred_element_type=F32)
            o_ref[:, hh * v:(hh + 1) * v] = y.astype(o_ref.dtype)

    @pl.when(use_shift)
    def _static_shift():
        def kv_step(kt, carry):
            off = pl.multiple_of(kt * KEY_TILE, KEY_TILE)
            lat_t = lat_ref[0, pl.ds(off, KEY_TILE), :]
            for ci in range(hps // hpc):
                rs = slice(ci * rc, (ci + 1) * rc)
                ps = [jnp.exp(sl) for sl in masked_logits(ci, off)]
                psum = ps[0]
                for pu in ps[1:]:
                    psum = psum + pu
                l_ref[rs, :] = l_ref[rs, :] + psum
                pv = jnp.dot(jnp.concatenate(ps, axis=1).astype(BF16), lat_t, preferred_element_type=F32)
                for u in range(c_slabs):
                    cs = slice(u * LANES, (u + 1) * LANES)
                    acc_ref[rs, cs] = acc_ref[rs, cs] + pv[:, cs]
            return carry

        lax.fori_loop(0, n_t, kv_step, 0)
        project_out(jnp.broadcast_to(1.0 / jnp.sum(l_ref[...], axis=1, keepdims=True), l_ref.shape))

    @pl.when(jnp.logical_not(use_shift))
    def _online():
        m_ref[...] = jnp.full(m_ref.shape, -3e38, F32)

        def kv_step(kt, carry):
            off = pl.multiple_of(kt * KEY_TILE, KEY_TILE)
            lat_t = lat_ref[0, pl.ds(off, KEY_TILE), :]
            for ci in range(hps // hpc):
                rs = slice(ci * rc, (ci + 1) * rc)
                slabs = masked_logits(ci, off)
                mx = slabs[0]
                for sl in slabs[1:]:
                    mx = jnp.maximum(mx, sl)
                m_prev = m_ref[rs, :]
                m_new = jnp.maximum(m_prev, jnp.max(mx, axis=1, keepdims=True))
                alpha = jnp.exp(m_prev - m_new)
                ps = [jnp.exp(sl - m_new) for sl in slabs]
                psum = ps[0]
                for pu in ps[1:]:
                    psum = psum + pu
                l_ref[rs, :] = alpha * l_ref[rs, :] + jnp.sum(psum, axis=1, keepdims=True)
                pv = jnp.dot(jnp.concatenate(ps, axis=1).astype(BF16), lat_t, preferred_element_type=F32)
                for u in range(c_slabs):
                    cs = slice(u * LANES, (u + 1) * LANES)
                    acc_ref[rs, cs] = alpha * acc_ref[rs, cs] + pv[:, cs]
                m_ref[rs, :] = m_new
            return carry

        lax.fori_loop(0, n_t, kv_step, 0)
        project_out(1.0 / l_ref[...])


def _sparse_attention(logit_bound, q, qi, widx, ki, kk, lat, w_uv, *, batch, tp, k_sel, pad_front):
    nblk, n_heads, qb, c = q.shape
    usable = 2.0 * logit_bound <= MAX_EXP_SPAN
    shift_info = jnp.stack([jnp.where(usable, logit_bound, 0.0), usable.astype(F32)]).astype(F32)
    tk = ki.shape[1]
    v = w_uv.shape[2]
    nq = tp // qb
    hps = min(HEADS_PER_STEP, n_heads)
    rows = hps * qb
    bias = _select_bias(shift_info, qi, widx, ki, batch=batch, tp=tp, k_sel=k_sel, pad_front=pad_front)
    return pl.pallas_call(
        _attn_kernel,
        grid=(batch, nq, n_heads // hps),
        in_specs=[pl.BlockSpec(memory_space=pltpu.SMEM),
                  pl.BlockSpec((1, hps, qb, c), lambda b, j, h: (b * nq + j, h, 0, 0)),
                  pl.BlockSpec((qb, tk), lambda b, j, h: (b * nq + j, 0)),
                  pl.BlockSpec((1, tk, c), lambda b, j, h: (b, 0, 0), pipeline_mode=pl.Buffered(1)),
                  pl.BlockSpec((1, tk, c), lambda b, j, h: (b, 0, 0), pipeline_mode=pl.Buffered(1)),
                  pl.BlockSpec((hps, c, v), lambda b, j, h: (h, 0, 0))],
        out_specs=pl.BlockSpec((qb, hps * v), lambda b, j, h: (b * nq + j, h)),
        out_shape=jax.ShapeDtypeStruct((nblk * qb, n_heads * v), BF16),
        scratch_shapes=[pltpu.VMEM((rows, c), F32), pltpu.VMEM((rows, LANES), F32),
                        pltpu.VMEM((rows, LANES), F32)],
        name="sparse_attention",
        compiler_params=_params("parallel", "arbitrary", "arbitrary"),
    )(shift_info, q, bias, kk, lat, w_uv)


def _merge_kernel(yp_ref, ya_ref, wp_ref, wa_ref, gp_ref, ga_ref, o_ref):
    pool = jnp.dot(yp_ref[...], wp_ref[...], preferred_element_type=F32)
    attn = jnp.dot(ya_ref[...], wa_ref[...], preferred_element_type=F32)
    o_ref[...] = (gp_ref[...].astype(F32) * pool + ga_ref[...].astype(F32) * attn).astype(o_ref.dtype)


def _merge(y_pool, y_attn, w_bp, w_ba, gates):
    n, pw = y_pool.shape
    aw = y_attn.shape[1]
    d = w_bp.shape[1]
    tm = _tile(n, 768, 128)
    tn = 512
    return pl.pallas_call(
        _merge_kernel,
        grid=(n // tm, d // tn),
        in_specs=[pl.BlockSpec((tm, pw), lambda i, j: (i, 0)),
                  pl.BlockSpec((tm, aw), lambda i, j: (i, 0)),
                  pl.BlockSpec((pw, tn), lambda i, j: (0, j)),
                  pl.BlockSpec((aw, tn), lambda i, j: (0, j)),
                  pl.BlockSpec((tm, tn), lambda i, j: (i, j)),
                  pl.BlockSpec((tm, tn), lambda i, j: (i, j + d // tn))],
        out_specs=pl.BlockSpec((tm, tn), lambda i, j: (i, j)),
        out_shape=jax.ShapeDtypeStruct((n, d), BF16),
        name="branch_merge",
        compiler_params=_params("parallel", "arbitrary"),
    )(y_pool, y_attn, w_bp, w_ba, gates, gates)


def _out_proj_kernel(x_ref, w_ref, r_ref, o_ref):
    o_ref[...] = r_ref[...] + jnp.dot(x_ref[...], w_ref[...], preferred_element_type=F32)


def _out_proj(merged, w_out, resid):
    n, k = merged.shape
    d = w_out.shape[1]
    tm = _tile(n, 768, 128)
    tn = _tile(d, 1024, LANES)
    return pl.pallas_call(
        _out_proj_kernel,
        grid=(n // tm, d // tn),
        in_specs=[pl.BlockSpec((tm, k), lambda i, j: (i, 0)),
                  pl.BlockSpec((k, tn), lambda i, j: (0, j)),
                  pl.BlockSpec((tm, tn), lambda i, j: (i, j))],
        out_specs=pl.BlockSpec((tm, tn), lambda i, j: (i, j)),
        out_shape=jax.ShapeDtypeStruct((n, d), F32),
        name="out_proj_residual",
        compiler_params=_params("parallel", "arbitrary"),
    )(merged, w_out, resid)


def _pack_halves(x):
    w = x.shape[1] // 2
    lo = pltpu.bitcast(x[:, :w].astype(BF16).astype(F32), I32)
    hi = pltpu.bitcast(x[:, w:].astype(BF16).astype(F32), I32)
    return ((lo >> 16) & 0xFFFF) | (hi & -65536)


def _unpack_halves(p):
    return pltpu.bitcast(p << 16, F32), pltpu.bitcast(p & -65536, F32)


def _router_kernel(h_ref, g_ref, rw_ref, rb_ref, xp_o, e_o, w_o, rank_o, cnt_o, carry_ref):
    i = pl.program_id(0)

    @pl.when(i == 0)
    def _():
        carry_ref[...] = jnp.zeros_like(carry_ref)

    x = h_ref[...]
    xn = x * lax.rsqrt(jnp.mean(x * x, axis=-1, keepdims=True) + EPS) * g_ref[...]
    xp_o[...] = _pack_halves(xn)
    logits = jnp.dot(xn.astype(BF16), rw_ref[...], preferred_element_type=F32) + rb_ref[...]
    tm = logits.shape[0]
    lane = lax.broadcasted_iota(I32, (tm, LANES), 1).astype(F32)
    vals, hots = [], []
    cur = logits
    for k in range(TOP_K_EXPERTS):
        m = jnp.max(cur, axis=1, keepdims=True)
        idx = jnp.min(jnp.where(cur == m, lane, float(LANES)), axis=1, keepdims=True)
        hot = lane == idx
        vals.append(m)
        hots.append(hot)
        e_o[:, k:k + 1] = idx.astype(I32)
        cur = jnp.where(hot, -jnp.inf, cur)
    exps = [jnp.exp(vk - vals[0]) for vk in vals]
    denom = exps[0]
    for ek in exps[1:]:
        denom = denom + ek
    for k in range(TOP_K_EXPERTS):
        w_o[:, k:k + 1] = exps[k] / denom
    onehot = jnp.zeros((tm, LANES), F32)
    for hot in hots:
        onehot = onehot + jnp.where(hot, 1.0, 0.0)
    lower = (lax.broadcasted_iota(I32, (tm, tm), 0) > lax.broadcasted_iota(I32, (tm, tm), 1)).astype(BF16)
    before = jnp.dot(lower, onehot.astype(BF16), preferred_element_type=F32) + carry_ref[...]
    for k in range(TOP_K_EXPERTS):
        rank_o[:, k:k + 1] = jnp.sum(jnp.where(hots[k], before, 0.0), axis=1, keepdims=True).astype(I32)
    carry_ref[...] = carry_ref[...] + jnp.sum(onehot, axis=0, keepdims=True)
    cnt_o[...] = carry_ref[...]


def _router(h2, norm2_g, rw, rb):
    n, d = h2.shape
    tm = _tile(n, ROUTE_TILE, Q_BLOCK)
    kk = TOP_K_EXPERTS
    return pl.pallas_call(
        _router_kernel,
        grid=(n // tm,),
        in_specs=[pl.BlockSpec((tm, d), lambda i: (i, 0)),
                  pl.BlockSpec((1, d), lambda i: (0, 0)),
                  pl.BlockSpec((d, LANES), lambda i: (0, 0)),
                  pl.BlockSpec((1, LANES), lambda i: (0, 0))],
        out_specs=[pl.BlockSpec((tm, d // 2), lambda i: (i, 0)),
                   pl.BlockSpec((tm, kk), lambda i: (i, 0)),
                   pl.BlockSpec((tm, kk), lambda i: (i, 0)),
                   pl.BlockSpec((tm, kk), lambda i: (i, 0)),
                   pl.BlockSpec((1, LANES), lambda i: (0, 0))],
        out_shape=[jax.ShapeDtypeStruct((n, d // 2), I32),
                   jax.ShapeDtypeStruct((n, kk), I32),
                   jax.ShapeDtypeStruct((n, kk), F32),
                   jax.ShapeDtypeStruct((n, kk), I32),
                   jax.ShapeDtypeStruct((1, LANES), F32)],
        scratch_shapes=[pltpu.VMEM((1, LANES), F32)],
        name="moe_router",
        compiler_params=_params("arbitrary"),
    )(h2, norm2_g.reshape(1, d), rw, rb)


def _row_copy(src_ref, src_row, dst_ref, dst_row, sem):
    return pltpu.make_async_copy(src_ref.at[pl.ds(src_row, 1), :], dst_ref.at[pl.ds(dst_row, 1), :], sem)


def _dispatch_kernel(dest_ref, pad_start_ref, pad_count_ref, xp_ref, xs_ref, zero_ref, sem, zero_sem, tail_sem):
    tm = xp_ref.shape[0]
    base = pl.program_id(0) * tm * TOP_K_EXPERTS

    def issue(r, carry):
        for k in range(TOP_K_EXPERTS):
            _row_copy(xp_ref, r, xs_ref, dest_ref[base + r * TOP_K_EXPERTS + k], sem).start(priority=k % 2)
        return carry

    lax.fori_loop(0, tm, issue, 0, unroll=4)
    for k in range(TOP_K_EXPERTS):
        pltpu.make_async_copy(xp_ref, xs_ref.at[pl.ds(0, tm), :], sem).wait()

    @pl.when(pl.program_id(0) == pl.num_programs(0) - 1)
    def _():
        zero_ref[...] = jnp.zeros_like(zero_ref)

        def per_expert(e, total):
            def zero_row(r, carry):
                _row_copy(zero_ref, 0, xs_ref, pad_start_ref[e] + r, zero_sem).start()
                return carry
            lax.fori_loop(0, pad_count_ref[e], zero_row, 0)
            return total + pad_count_ref[e]

        n_exp = pad_start_ref.shape[0]
        total = lax.fori_loop(0, n_exp, per_expert, 0)

        def zero_done(r, carry):
            _row_copy(zero_ref, 0, xs_ref, 0, zero_sem).wait()
            return carry

        lax.fori_loop(0, total, zero_done, 0)

        tile = zero_ref.shape[0]
        first = (pad_start_ref[n_exp - 1] + pad_count_ref[n_exp - 1]) // tile
        n_tail = xs_ref.shape[0] // tile - first

        def tile_copy(i):
            row0 = pl.multiple_of((first + i) * tile, tile)
            return pltpu.make_async_copy(zero_ref, xs_ref.at[pl.ds(row0, tile), :], tail_sem)

        def zero_tile(i, carry):
            tile_copy(i).start()
            return carry

        def tile_done(i, carry):
            tile_copy(i).wait()
            return carry

        lax.fori_loop(0, n_tail, zero_tile, 0)
        lax.fori_loop(0, n_tail, tile_done, 0)


def _dispatch(dest_flat, pad_start, pad_count, xp, n_slots):
    n, w = xp.shape
    tm = _tile(n, ROUTE_TILE, Q_BLOCK)
    return pl.pallas_call(
        _dispatch_kernel,
        grid_spec=pltpu.PrefetchScalarGridSpec(
            num_scalar_prefetch=3,
            grid=(n // tm,),
            in_specs=[pl.BlockSpec((tm, w), lambda i, *_: (i, 0))],
            out_specs=pl.BlockSpec(memory_space=pl.ANY),
            scratch_shapes=[pltpu.VMEM((EXPERT_TILE, w), I32), pltpu.SemaphoreType.DMA(()),
                            pltpu.SemaphoreType.DMA(()), pltpu.SemaphoreType.DMA(())]),
        out_shape=jax.ShapeDtypeStruct((n_slots, w), I32),
        name="moe_dispatch",
        compiler_params=_params("arbitrary"),
    )(dest_flat, pad_start, pad_count, xp)


def _deinterleave_kernel(w_ref, o_ref):
    blk = 2 * LANES
    dst = lax.broadcasted_iota(I32, (blk, blk), 1)
    src = jnp.where(dst < LANES, 2 * dst, 2 * (dst - LANES) + 1)
    perm = (lax.broadcasted_iota(I32, (blk, blk), 0) == src).astype(BF16)
    for b in range(w_ref.shape[2] // blk):
        cols = slice(b * blk, (b + 1) * blk)
        o_ref[0, :, cols] = jnp.dot(w_ref[0, :, cols].astype(BF16), perm,
                                    preferred_element_type=F32).astype(o_ref.dtype)


def _deinterleave_gate_up(w_gate_up):
    n_exp, d, de2 = w_gate_up.shape
    assert de2 % (2 * LANES) == 0
    tk = _tile(d, 512, 16)
    return pl.pallas_call(
        _deinterleave_kernel,
        grid=(n_exp, d // tk),
        in_specs=[pl.BlockSpec((1, tk, de2), lambda e, i: (e, i, 0))],
        out_specs=pl.BlockSpec((1, tk, de2), lambda e, i: (e, i, 0)),
        out_shape=jax.ShapeDtypeStruct((n_exp, d, de2), BF16),
        name="moe_weight_regroup",
        compiler_params=_params("parallel", "arbitrary"),
    )(w_gate_up)


def _expert_up_kernel(te_ref, tv_ref, xs_ref, w_ref, b_ref, o_ref):
    t = pl.program_id(0)

    @pl.when(tv_ref[t] > 0)
    def _():
        lo, hi = _unpack_halves(xs_ref[...])
        x = jnp.concatenate([lo.astype(BF16), hi.astype(BF16)], axis=1)
        gu = jnp.dot(x, w_ref[0], preferred_element_type=F32) + b_ref[0]
        for blk in range(gu.shape[1] // (2 * LANES)):
            g0 = blk * 2 * LANES
            gate = jnp.minimum(gu[:, g0:g0 + LANES], SWIGLU_LIMIT)
            up = jnp.clip(gu[:, g0 + LANES:g0 + 2 * LANES], -SWIGLU_LIMIT, SWIGLU_LIMIT)
            act = (up + 1.0) * gate * jax.nn.sigmoid(SWIGLU_ALPHA * gate)
            o_ref[:, blk * LANES:(blk + 1) * LANES] = act.astype(o_ref.dtype)

    @pl.when(tv_ref[t] == 0)
    def _():
        o_ref[...] = jnp.zeros_like(o_ref)


def _expert_up(tile_expert, tile_valid, xs, w_gu, b_gu):
    n_slots, w = xs.shape
    n_exp, d, de2 = w_gu.shape
    tm = EXPERT_TILE
    return pl.pallas_call(
        _expert_up_kernel,
        grid_spec=pltpu.PrefetchScalarGridSpec(
            num_scalar_prefetch=2,
            grid=(n_slots // tm,),
            in_specs=[pl.BlockSpec((tm, w), lambda t, te, tv: (t, 0)),
                      pl.BlockSpec((1, d, de2), lambda t, te, tv: (te[t], 0, 0)),
                      pl.BlockSpec((1, 1, de2), lambda t, te, tv: (te[t], 0, 0))],
            out_specs=pl.BlockSpec((tm, de2 // 2), lambda t, te, tv: (t, 0))),
        out_shape=jax.ShapeDtypeStruct((n_slots, de2 // 2), BF16),
        name="moe_expert_up",
        compiler_params=_params("arbitrary"),
    )(tile_expert, tile_valid, xs, w_gu, b_gu)


def _expert_down_kernel(te_ref, tv_ref, a_ref, w_ref, b_ref, o_ref, wbf_ref):
    t = pl.program_id(0)

    @pl.when(jnp.logical_or(t == 0, te_ref[t] != te_ref[jnp.maximum(t - 1, 0)]))
    def _():
        wbf_ref[...] = w_ref[0].astype(BF16)

    @pl.when(tv_ref[t] > 0)
    def _():
        y = jnp.dot(a_ref[...], wbf_ref[...], preferred_element_type=F32) + b_ref[0]
        o_ref[...] = _pack_halves(y)

    @pl.when(tv_ref[t] == 0)
    def _():
        o_ref[...] = jnp.zeros_like(o_ref)


def _expert_down(tile_expert, tile_valid, act, w_down, b_down):
    n_slots, de = act.shape
    n_exp, _, d = w_down.shape
    tm = EXPERT_TILE
    return pl.pallas_call(
        _expert_down_kernel,
        grid_spec=pltpu.PrefetchScalarGridSpec(
            num_scalar_prefetch=2,
            grid=(n_slots // tm,),
            in_specs=[pl.BlockSpec((tm, de), lambda t, te, tv: (t, 0)),
                      pl.BlockSpec((1, de, d), lambda t, te, tv: (te[t], 0, 0)),
                      pl.BlockSpec((1, 1, d), lambda t, te, tv: (te[t], 0, 0))],
            out_specs=pl.BlockSpec((tm, d // 2), lambda t, te, tv: (t, 0)),
            scratch_shapes=[pltpu.VMEM((de, d), BF16)]),
        out_shape=jax.ShapeDtypeStruct((n_slots, d // 2), I32),
        name="moe_expert_down",
        compiler_params=_params("arbitrary"),
    )(tile_expert, tile_valid, act, w_down, b_down)


def _combine_kernel(dest_ref, w_ref, h_ref, ye_ref, o_ref, buf_ref, sems, *, nq):
    b = pl.program_id(0)
    i = pl.program_id(1)
    n_i = pl.num_programs(1)
    n_steps = pl.num_programs(0) * n_i
    tm = h_ref.shape[0]
    half = buf_ref.shape[3]
    step = b * n_i + i
    slot = step % 2

    def start_rows(bb, ii, sl):
        base = ((bb * nq + 1 + ii) * tm) * TOP_K_EXPERTS

        def issue(r, carry):
            for k in range(TOP_K_EXPERTS):
                _row_copy(ye_ref, dest_ref[base + r * TOP_K_EXPERTS + k], buf_ref.at[sl, k], r,
                          sems.at[sl]).start(priority=k % 2)
            return carry

        lax.fori_loop(0, tm, issue, 0, unroll=4)

    @pl.when(step == 0)
    def _():
        start_rows(b, i, slot)

    @pl.when(step + 1 < n_steps)
    def _():
        wrap = i + 1 == n_i
        start_rows(jnp.where(wrap, b + 1, b), jnp.where(wrap, 0, i + 1), 1 - slot)

    for k in range(TOP_K_EXPERTS):
        pltpu.make_async_copy(ye_ref.at[pl.ds(0, tm), :], buf_ref.at[slot, k], sems.at[slot]).wait()

    lo_sum = h_ref[:, :half]
    hi_sum = h_ref[:, half:]
    for k in range(TOP_K_EXPERTS):
        lo, hi = _unpack_halves(buf_ref[slot, k])
        wk = w_ref[:, k:k + 1]
        lo_sum = lo_sum + wk * lo
        hi_sum = hi_sum + wk * hi
    o_ref[0, :, :half] = lo_sum
    o_ref[0, :, half:] = hi_sum


def _combine(dest_flat, sel_w, h2, ye, *, batch, seq, tp):
    d = h2.shape[1]
    tm = Q_BLOCK
    nq = tp // tm
    return pl.pallas_call(
        functools.partial(_combine_kernel, nq=nq),
        grid_spec=pltpu.PrefetchScalarGridSpec(
            num_scalar_prefetch=1,
            grid=(batch, seq // tm),
            in_specs=[pl.BlockSpec((tm, TOP_K_EXPERTS), lambda b, i, dest: (b * nq + 1 + i, 0)),
                      pl.BlockSpec((tm, d), lambda b, i, dest: (b * nq + 1 + i, 0)),
                      pl.BlockSpec(memory_space=pl.ANY)],
            out_specs=pl.BlockSpec((1, tm, d), lambda b, i, dest: (b, i, 0)),
            scratch_shapes=[pltpu.VMEM((2, TOP_K_EXPERTS, tm, d // 2), I32), pltpu.SemaphoreType.DMA((2,))]),
        out_shape=jax.ShapeDtypeStruct((batch, seq, d), F32),
        name="moe_combine",
        compiler_params=_params("arbitrary", "arbitrary"),
    )(dest_flat, sel_w, h2, ye)


def _moe(h2, norm2_g, router_w, router_b, w_gate_up, b_gate_up, w_down, b_down, *, batch, seq, tp):
    n, d = h2.shape
    n_exp = router_w.shape[1]
    de = w_down.shape[1]
    assert n_exp <= LANES
    rw = jnp.pad(router_w, ((0, 0), (0, LANES - n_exp))).astype(BF16)
    rb = jnp.pad(router_b, (0, LANES - n_exp), constant_values=NEG).reshape(1, LANES)
    xp, sel_e, sel_w, rank, counts = _router(h2, norm2_g, rw, rb)

    cnt = counts[0, :n_exp].astype(I32)
    padded = ((cnt + EXPERT_TILE - 1) // EXPERT_TILE) * EXPERT_TILE
    e_ids = jnp.arange(n_exp, dtype=I32)
    ends = jnp.sum(jnp.where(e_ids[:, None] <= e_ids[None, :], padded[:, None], 0), axis=0)
    starts = ends - padded
    total = ends[n_exp - 1]
    dest_flat = (starts[sel_e] + rank).reshape(-1)
    n_tiles = (n * TOP_K_EXPERTS) // EXPERT_TILE + n_exp
    tile_start = jnp.arange(n_tiles, dtype=I32) * EXPERT_TILE
    tile_valid = (tile_start < total).astype(I32)
    last_expert = jnp.sum((ends <= total - 1).astype(I32))
    tile_expert = jnp.minimum(jnp.sum((ends[None, :] <= tile_start[:, None]).astype(I32), axis=1), last_expert)

    w_gu = _deinterleave_gate_up(w_gate_up)
    nb = (2 * de) // (2 * LANES)
    b_gu = b_gate_up.reshape(n_exp, nb, LANES, 2).transpose(0, 1, 3, 2).reshape(n_exp, 1, 2 * de)

    xs = _dispatch(dest_flat, starts + cnt, padded - cnt, xp, n_tiles * EXPERT_TILE)
    act = _expert_up(tile_expert, tile_valid, xs, w_gu, b_gu)
    ye = _expert_down(tile_expert, tile_valid, act, w_down, b_down.reshape(n_exp, 1, d))
    return _combine(dest_flat, sel_w, h2, ye, batch=batch, seq=seq, tp=tp)


def kernel(x, meta_tokens, norm1_g, w_in, q_norm_g, w_uq, w_iq, kv_norm_g, q_head_norm_g, k_head_norm_g,
           idx_k_norm_g, w_uv, w_branch_attn, w_pool, pool_scale, w_branch_pool, w_out, norm2_g, router_w,
           router_b, w_gate_up, b_gate_up, w_down, b_down):
    batch, seq, d = x.shape
    n_meta = meta_tokens.shape[0]
    depth = norm1_g.shape[0]
    qr, kvr, di = q_norm_g.shape[1], kv_norm_g.shape[1], idx_k_norm_g.shape[1]
    n_heads, v = w_uv.shape[1], w_uv.shape[3]
    idx_heads = w_iq.shape[2] // di
    pw = pool_scale.shape[1]
    assert depth == 1
    assert n_meta <= CHUNK and seq % Q_BLOCK == 0 and idx_heads <= LANES
    assert qr % kvr == 0 and (qr + kvr) % di == 0 and di % LANES == 0
    pad_front = Q_BLOCK - n_meta
    tp = Q_BLOCK + seq
    tk = -(-tp // KEY_TILE) * KEY_TILE
    k_sel = min(TOPK_MAX, seq // 4)
    attn_scale = kvr ** -0.5
    idx_scale = (di ** -0.5) * (idx_heads ** -0.5)

    head_rows = jnp.concatenate([jnp.zeros((pad_front, d), x.dtype), meta_tokens.astype(x.dtype)], axis=0)

    out = None
    for l in range(depth):
        n_small = qr + kvr + di
        o_pool = n_small + idx_heads
        o_gate = o_pool + pw
        w_small = jnp.pad(w_in[l][:, :o_pool], ((0, 0), (0, LANES - idx_heads))).astype(BF16)
        w_a = w_in[l][:, o_pool:o_gate].astype(BF16)
        w_g = w_in[l][:, o_gate:].astype(BF16)

        h, xn = _embed_norm(x, head_rows, norm1_g[l])
        small = _matmul(xn, w_small, F32, tn=_tile(w_small.shape[1], 1024, LANES), name="in_proj_latents")
        a_pool = _matmul(xn, w_a, BF16, tn=_tile(pw, 1024, LANES), name="in_proj_pool")
        gates = _matmul(xn, w_g, BF16, tn=_tile(2 * d, 1024, LANES), sigmoid=True, name="in_proj_gates")

        y_pool = _pool_mixer(a_pool, w_pool[l].astype(BF16), pool_scale[l], batch=batch, tp=tp,
                             pad_front=pad_front)

        cq, widx = _prep_q(small, q_norm_g[l], qr, n_small, idx_scale)
        lat, kk, ki = _prep_kv(small, kv_norm_g[l], k_head_norm_g[l], idx_k_norm_g[l],
                               batch=batch, tp=tp, tk=tk, qr=qr, kvr=kvr, di=di)
        q = _qproj(cq, w_uq[l].astype(BF16), q_head_norm_g[l], n_heads=n_heads, hd=kvr,
                   heads_per_tile=min(2, n_heads),
                   scale=attn_scale, norm=True, name="q_proj_headnorm")
        qi = _qproj(cq, w_iq[l].astype(BF16), jnp.ones((di,), F32), n_heads=idx_heads, hd=di,
                    heads_per_tile=min(8, idx_heads), scale=1.0, norm=False, name="indexer_q_proj")
        logit_bound = (BOUND_MARGIN * kvr * attn_scale * jnp.max(jnp.abs(q_head_norm_g[l]))
                       * jnp.max(jnp.abs(k_head_norm_g[l])))
        y_attn = _sparse_attention(logit_bound, q, qi, widx, ki, kk, lat, w_uv[l].astype(BF16), batch=batch,
                                   tp=tp, k_sel=k_sel, pad_front=pad_front)

        merged = _merge(y_pool, y_attn, w_branch_pool[l].astype(BF16), w_branch_attn[l].astype(BF16), gates)
        h2 = _out_proj(merged, w_out[l].astype(BF16), h)

        out = _moe(h2, norm2_g[l], router_w[l], router_b[l], w_gate_up[l], b_gate_up[l], w_down[l],
                   b_down[l], batch=batch, seq=seq, tp=tp)
    return out
```

```python
import functools

import jax
import jax.numpy as jnp
from jax import lax
from jax.experimental import pallas as pl
from jax.experimental.pallas import tpu as pltpu

F32 = jnp.float32
BF16 = jnp.bfloat16
I32 = jnp.int32

CHUNK = 64
POOL_WINDOWS = (2, 4, 8, 16)
TOPK_MAX = 256
TOP_K_EXPERTS = 4
SWIGLU_LIMIT = 7.0
SWIGLU_ALPHA = 1.702
EPS = 1e-6
NEG = -1e30

Q_BLOCK = 128
KEY_TILE = 512
HEADS_PER_STEP = 32
INDEXER_HEADS_PER_DOT = 8
SELECT_ROWS = 384
HEADS_PER_CHAIN = 4
HALO = 16
EXPERT_TILE = 256
ROUTE_TILE = 512
LANES = 128
INT_MIN = -2 ** 31
MAX_EXP_SPAN = 80.0
BOUND_MARGIN = 1.02
V7X_VMEM_BYTES = 64 * 1024 * 1024
VMEM_LIMIT = V7X_VMEM_BYTES * 7 // 8


def _tile(n, target, mult):
    best = None
    for t in range(mult, min(n, target) + 1, mult):
        if n % t == 0:
            best = t
    assert best is not None, (n, target, mult)
    return best


def _params(*sem):
    return pltpu.CompilerParams(dimension_semantics=sem, vmem_limit_bytes=VMEM_LIMIT)


def _embed_norm_kernel(x_ref, head_ref, g_ref, h_ref, xn_ref):
    rows = jnp.where(pl.program_id(1) == 0, head_ref[...], x_ref[0])
    h_ref[...] = rows
    ms = jnp.mean(rows * rows, axis=-1, keepdims=True)
    xn_ref[...] = (rows * lax.rsqrt(ms + EPS) * g_ref[...]).astype(xn_ref.dtype)


def _embed_norm(x, head_rows, g):
    batch, seq, d = x.shape
    nq = seq // Q_BLOCK + 1
    n = batch * nq * Q_BLOCK
    return pl.pallas_call(
        _embed_norm_kernel,
        grid=(batch, nq),
        in_specs=[pl.BlockSpec((1, Q_BLOCK, d), lambda b, i: (b, jnp.maximum(i - 1, 0), 0)),
                  pl.BlockSpec((Q_BLOCK, d), lambda b, i: (0, 0)),
                  pl.BlockSpec((1, d), lambda b, i: (0, 0))],
        out_specs=[pl.BlockSpec((Q_BLOCK, d), lambda b, i: (b * nq + i, 0)),
                   pl.BlockSpec((Q_BLOCK, d), lambda b, i: (b * nq + i, 0))],
        out_shape=[jax.ShapeDtypeStruct((n, d), x.dtype), jax.ShapeDtypeStruct((n, d), BF16)],
        name="embed_rmsnorm",
        compiler_params=_params("parallel", "arbitrary"),
    )(x, head_rows, g.reshape(1, d))


def _matmul_kernel(x_ref, w_ref, o_ref, *, sigmoid):
    acc = jnp.dot(x_ref[...], w_ref[...], preferred_element_type=F32)
    if sigmoid:
        acc = jax.nn.sigmoid(acc)
    o_ref[...] = acc.astype(o_ref.dtype)


def _matmul(x, w, out_dtype, *, tn, name, sigmoid=False):
    m, k = x.shape
    n = w.shape[1]
    tm = _tile(m, 768, 128)
    return pl.pallas_call(
        functools.partial(_matmul_kernel, sigmoid=sigmoid),
        grid=(m // tm, n // tn),
        in_specs=[pl.BlockSpec((tm, k), lambda i, j: (i, 0)), pl.BlockSpec((k, tn), lambda i, j: (0, j))],
        out_specs=pl.BlockSpec((tm, tn), lambda i, j: (i, j)),
        out_shape=jax.ShapeDtypeStruct((m, n), out_dtype),
        name=name,
        compiler_params=_params("parallel", "arbitrary"),
    )(x, w)


def _prep_q_kernel(cq_ref, wi_ref, g_ref, cq_o, wi_o, *, idx_scale):
    x = cq_ref[...]
    ms = jnp.mean(x * x, axis=-1, keepdims=True)
    cq_o[...] = (x * lax.rsqrt(ms + EPS) * g_ref[...]).astype(cq_o.dtype)
    wi_o[...] = wi_ref[...] * idx_scale


def _prep_q(small, q_norm_g, qr, wi_col, idx_scale):
    n = small.shape[0]
    tm = _tile(n, 768, 128)
    return pl.pallas_call(
        functools.partial(_prep_q_kernel, idx_scale=idx_scale),
        grid=(n // tm,),
        in_specs=[pl.BlockSpec((tm, qr), lambda i: (i, 0)),
                  pl.BlockSpec((tm, LANES), lambda i: (i, wi_col // LANES)),
                  pl.BlockSpec((1, qr), lambda i: (0, 0))],
        out_specs=[pl.BlockSpec((tm, qr), lambda i: (i, 0)), pl.BlockSpec((tm, LANES), lambda i: (i, 0))],
        out_shape=[jax.ShapeDtypeStruct((n, qr), BF16), jax.ShapeDtypeStruct((n, LANES), F32)],
        name="prep_query_latent",
        compiler_params=_params("parallel"),
    )(small, small, q_norm_g.reshape(1, qr))


def _prep_kv_kernel(ckv_ref, kidx_ref, gkv_ref, gkh_ref, gki_ref, lat_o, kk_o, ki_o, *, n_valid):
    i = pl.program_id(1)

    @pl.when(i < n_valid)
    def _():
        c = ckv_ref[...]
        lat = c * lax.rsqrt(jnp.mean(c * c, axis=-1, keepdims=True) + EPS) * gkv_ref[...]
        kk = lat * lax.rsqrt(jnp.mean(lat * lat, axis=-1, keepdims=True) + EPS) * gkh_ref[...]
        k = kidx_ref[...]
        ki = k * lax.rsqrt(jnp.mean(k * k, axis=-1, keepdims=True) + EPS) * gki_ref[...]
        lat_o[0] = lat.astype(lat_o.dtype)
        kk_o[0] = kk.astype(kk_o.dtype)
        ki_o[0] = ki.astype(ki_o.dtype)

    @pl.when(i >= n_valid)
    def _():
        lat_o[...] = jnp.zeros_like(lat_o)
        kk_o[...] = jnp.zeros_like(kk_o)
        ki_o[...] = jnp.zeros_like(ki_o)


def _prep_kv(small, gkv, gkh, gki, *, batch, tp, tk, qr, kvr, di):
    nq = tp // Q_BLOCK
    nk = tk // Q_BLOCK

    def row(b, i):
        return b * nq + jnp.minimum(i, nq - 1)

    return pl.pallas_call(
        functools.partial(_prep_kv_kernel, n_valid=nq),
        grid=(batch, nk),
        in_specs=[pl.BlockSpec((Q_BLOCK, kvr), lambda b, i: (row(b, i), qr // kvr)),
                  pl.BlockSpec((Q_BLOCK, di), lambda b, i: (row(b, i), (qr + kvr) // di)),
                  pl.BlockSpec((1, kvr), lambda b, i: (0, 0)),
                  pl.BlockSpec((1, kvr), lambda b, i: (0, 0)),
                  pl.BlockSpec((1, di), lambda b, i: (0, 0))],
        out_specs=[pl.BlockSpec((1, Q_BLOCK, kvr), lambda b, i: (b, i, 0)),
                   pl.BlockSpec((1, Q_BLOCK, kvr), lambda b, i: (b, i, 0)),
                   pl.BlockSpec((1, Q_BLOCK, di), lambda b, i: (b, i, 0))],
        out_shape=[jax.ShapeDtypeStruct((batch, tk, kvr), BF16),
                   jax.ShapeDtypeStruct((batch, tk, kvr), BF16),
                   jax.ShapeDtypeStruct((batch, tk, di), BF16)],
        name="prep_keys",
        compiler_params=_params("parallel", "arbitrary"),
    )(small, small, gkv.reshape(1, kvr), gkh.reshape(1, kvr), gki.reshape(1, di))


def _pool_kernel(a_ref, halo_ref, w_ref, sc_ref, o_ref, xs_ref, *, tp_tile, pad_front, pg):
    i = pl.program_id(1)
    t = i * tp_tile + lax.broadcasted_iota(I32, (tp_tile, 1), 0) - pad_front
    for g, win in enumerate(POOL_WINDOWS):
        cols = slice(g * pg, (g + 1) * pg)
        xs_ref[0:HALO, :] = halo_ref[0, :, cols].astype(F32)
        xs_ref[HALO:, :] = a_ref[0, :, cols].astype(F32)
        cur = xs_ref[pl.ds(HALO, tp_tile), :]
        acc = cur
        for k in range(1, win):
            acc = acc + xs_ref[pl.ds(HALO - k, tp_tile), :]
        cnt = jnp.clip(t + 1, 1, win).astype(F32)
        pooled = (acc / cnt - cur).astype(BF16)
        y = jnp.dot(pooled, w_ref[g], preferred_element_type=F32) * sc_ref[:, cols]
        o_ref[0, :, cols] = y.astype(o_ref.dtype)


def _pool_mixer(a, w_pool, pool_scale, *, batch, tp, pad_front):
    pw = a.shape[-1]
    n_groups, pg, _ = w_pool.shape
    tpt = _tile(tp, 1536, HALO)
    a3 = a.reshape(batch, tp, pw)
    halo_blocks = tpt // HALO
    out = pl.pallas_call(
        functools.partial(_pool_kernel, tp_tile=tpt, pad_front=pad_front, pg=pg),
        grid=(batch, tp // tpt),
        in_specs=[pl.BlockSpec((1, tpt, pw), lambda b, i: (b, i, 0)),
                  pl.BlockSpec((1, HALO, pw), lambda b, i: (b, jnp.maximum(i * halo_blocks - 1, 0), 0)),
                  pl.BlockSpec((n_groups, pg, pg), lambda b, i: (0, 0, 0)),
                  pl.BlockSpec((1, pw), lambda b, i: (0, 0))],
        out_specs=pl.BlockSpec((1, tpt, pw), lambda b, i: (b, i, 0)),
        out_shape=jax.ShapeDtypeStruct((batch, tp, pw), BF16),
        scratch_shapes=[pltpu.VMEM((tpt + HALO, pg), F32)],
        name="pool_mixer",
        compiler_params=_params("parallel", "arbitrary"),
    )(a3, a3, w_pool, pool_scale.reshape(1, pw))
    return out.reshape(batch * tp, pw)


def _qproj_kernel(x_ref, w_ref, g_ref, o_ref, *, scale, norm, heads, hd):
    acc = jnp.dot(x_ref[...], w_ref[...], preferred_element_type=F32)
    nblk = o_ref.shape[0]
    for hh in range(heads):
        a = acc[:, hh * hd:(hh + 1) * hd]
        if norm:
            a = a * lax.rsqrt(jnp.mean(a * a, axis=-1, keepdims=True) + EPS) * (g_ref[...] * scale)
        a = a.astype(o_ref.dtype)
        for r in range(nblk):
            o_ref[r, hh] = a[r * Q_BLOCK:(r + 1) * Q_BLOCK]


def _qproj(cq, w, g, *, n_heads, hd, heads_per_tile, scale, norm, name):
    n, r = cq.shape
    tm = _tile(n, 1536, Q_BLOCK)
    nblk = tm // Q_BLOCK
    tn = heads_per_tile * hd
    return pl.pallas_call(
        functools.partial(_qproj_kernel, scale=scale, norm=norm, heads=heads_per_tile, hd=hd),
        grid=(n // tm, n_heads // heads_per_tile),
        in_specs=[pl.BlockSpec((tm, r), lambda i, j: (i, 0)),
                  pl.BlockSpec((r, tn), lambda i, j: (0, j)),
                  pl.BlockSpec((1, hd), lambda i, j: (0, 0))],
        out_specs=pl.BlockSpec((nblk, heads_per_tile, Q_BLOCK, hd), lambda i, j: (i, j, 0, 0)),
        out_shape=jax.ShapeDtypeStruct((n // Q_BLOCK, n_heads, Q_BLOCK, hd), BF16),
        name=name,
        compiler_params=_params("parallel", "arbitrary"),
    )(cq, w, g.reshape(1, hd))


def _select_kernel(shift_ref, qi_ref, wi_ref, ki_ref, bias_ref, keyp_ref, *, k_sel, pad_front, idx_heads):
    j = pl.program_id(1)
    shift = shift_ref[0]
    nsub, qb, di = qi_ref.shape[0], qi_ref.shape[2], qi_ref.shape[3]
    rows = nsub * qb
    n_tiles = keyp_ref.shape[1] // KEY_TILE
    n_t = (j * rows + rows + KEY_TILE - 1) // KEY_TILE
    nt_dims = (((1,), (1,)), ((), ()))
    tq = j * rows + lax.broadcasted_iota(I32, (rows, 1), 0)
    limit = ((jnp.maximum(tq, CHUNK) + CHUNK) // CHUNK) * CHUNK

    def score_tile(kt, carry):
        off = pl.multiple_of(kt * KEY_TILE, KEY_TILE)
        ki_t = ki_ref[0, pl.ds(off, KEY_TILE), :]
        hpi = min(INDEXER_HEADS_PER_DOT, idx_heads)
        parts = []
        for sb in range(nsub):
            part = jnp.zeros((qb, KEY_TILE), F32)
            for g in range(idx_heads // hpi):
                qg = qi_ref[sb, g * hpi:(g + 1) * hpi].reshape(hpi * qb, di)
                s = lax.dot_general(qg, ki_t, nt_dims, preferred_element_type=F32)
                s = jnp.maximum(s, 0.0)
                for hh in range(hpi):
                    h = g * hpi + hh
                    part = part + s[hh * qb:(hh + 1) * qb] * wi_ref[sb * qb:(sb + 1) * qb, h:h + 1]
            parts.append(part)
        bits = pltpu.bitcast(jnp.concatenate(parts, axis=0), I32)
        key = bits ^ ((bits >> 31) & 0x7FFFFFFF)
        s_idx = off + lax.broadcasted_iota(I32, (rows, KEY_TILE), 1)
        adm = (s_idx >= pad_front) & (s_idx < limit)
        keyp_ref[:, pl.ds(off, KEY_TILE)] = jnp.where(adm, key, INT_MIN)
        return carry

    lax.fori_loop(0, n_t, score_tile, 0)

    def count_ge(thr):
        cnts = []
        for sb in range(nsub):
            rs = slice(sb * qb, (sb + 1) * qb)
            thr_sb = thr[rs]

            def body(kt, cnt, rs=rs, thr_sb=thr_sb):
                off = pl.multiple_of(kt * KEY_TILE, KEY_TILE)
                ge = jnp.where(keyp_ref[rs, pl.ds(off, KEY_TILE)] >= thr_sb, 1.0, 0.0)
                for u in range(KEY_TILE // LANES):
                    cnt = cnt + ge[:, u * LANES:(u + 1) * LANES]
                return cnt

            n_t_sb = (j * rows + (sb + 1) * qb + KEY_TILE - 1) // KEY_TILE
            cnts.append(lax.fori_loop(0, n_t_sb, body, jnp.zeros((qb, LANES), F32)))
        return jnp.sum(jnp.concatenate(cnts, axis=0), axis=1, keepdims=True)

    def bit_step(bi, thr):
        cand = thr + lax.shift_left(jnp.int32(1), 31 - bi)
        return jnp.where(count_ge(cand) >= k_sel, cand, thr)

    thr = lax.fori_loop(0, 32, bit_step, jnp.full((rows, 1), INT_MIN, I32))
    thr = jnp.maximum(thr, INT_MIN + 1)

    def bias_tile(kt, carry):
        off = pl.multiple_of(kt * KEY_TILE, KEY_TILE)
        sel = keyp_ref[:, pl.ds(off, KEY_TILE)] >= thr
        bias_ref[:, pl.ds(off, KEY_TILE)] = jnp.where(sel, -shift, NEG)
        return carry

    lax.fori_loop(0, n_t, bias_tile, 0)

    def masked_tile(kt, carry):
        off = pl.multiple_of(kt * KEY_TILE, KEY_TILE)
        bias_ref[:, pl.ds(off, KEY_TILE)] = jnp.full((rows, KEY_TILE), NEG, F32)
        return carry

    lax.fori_loop(n_t, n_tiles, masked_tile, 0)

    @pl.when(jnp.max(count_ge(thr)) > k_sel)
    def _ties():
        quota = k_sel - count_ge(thr + 1)
        upper = (lax.broadcasted_iota(I32, (LANES, LANES), 0)
                 < lax.broadcasted_iota(I32, (LANES, LANES), 1)).astype(BF16)

        def chunk(ci, seen):
            off = pl.multiple_of(ci * LANES, LANES)
            kp = keyp_ref[:, pl.ds(off, LANES)]
            tie = jnp.where(kp == thr, 1.0, 0.0)
            rank = jnp.dot(tie.astype(BF16), upper, preferred_element_type=F32) + seen
            sel = jnp.where(kp > thr, 1.0, tie * jnp.where(rank < quota, 1.0, 0.0))
            bias_ref[:, pl.ds(off, LANES)] = jnp.where(sel > 0.5, -shift, NEG)
            return seen + jnp.sum(tie, axis=1, keepdims=True)

        lax.fori_loop(0, n_t * (KEY_TILE // LANES), chunk, jnp.zeros((rows, 1), F32))


def _select_bias(shift_info, qi, widx, ki, *, batch, tp, k_sel, pad_front):
    nblk, idx_heads, qb, di = qi.shape
    tk = ki.shape[1]
    rows = _tile(tp, SELECT_ROWS, qb)
    nsub = rows // qb
    nsel = tp // rows
    return pl.pallas_call(
        functools.partial(_select_kernel, k_sel=k_sel, pad_front=pad_front, idx_heads=idx_heads),
        grid=(batch, nsel),
        in_specs=[pl.BlockSpec(memory_space=pltpu.SMEM),
                  pl.BlockSpec((nsub, idx_heads, qb, di), lambda b, j: (b * nsel + j, 0, 0, 0)),
                  pl.BlockSpec((rows, LANES), lambda b, j: (b * nsel + j, 0)),
                  pl.BlockSpec((1, tk, di), lambda b, j: (b, 0, 0))],
        out_specs=pl.BlockSpec((rows, tk), lambda b, j: (b * nsel + j, 0)),
        out_shape=jax.ShapeDtypeStruct((nblk * qb, tk), F32),
        scratch_shapes=[pltpu.VMEM((rows, tk), I32)],
        name="indexer_select",
        compiler_params=_params("parallel", "arbitrary"),
    )(shift_info, qi, widx, ki)


def _attn_kernel(shift_ref, q_ref, bias_ref, kk_ref, lat_ref, wuv_ref, o_ref, acc_ref, m_ref, l_ref):
    j = pl.program_id(1)
    use_shift = shift_ref[1] > 0.5
    hps, qb, c = q_ref.shape[1], q_ref.shape[2], q_ref.shape[3]
    v = wuv_ref.shape[2]
    n_t = (j * qb + qb + KEY_TILE - 1) // KEY_TILE
    nt_dims = (((1,), (1,)), ((), ()))

    l_ref[...] = jnp.zeros(l_ref.shape, F32)
    acc_ref[...] = jnp.zeros(acc_ref.shape, F32)
    hpc = min(HEADS_PER_CHAIN, hps)
    rc = hpc * qb
    k_slabs = KEY_TILE // LANES
    c_slabs = c // LANES

    def masked_logits(ci, off):
        q_c = q_ref[0, ci * hpc:(ci + 1) * hpc].reshape(rc, c)
        s = lax.dot_general(q_c, kk_ref[0, pl.ds(off, KEY_TILE), :], nt_dims, preferred_element_type=F32)
        s = (s.reshape(hpc, qb, KEY_TILE) + bias_ref[:, pl.ds(off, KEY_TILE)][None]).reshape(rc, KEY_TILE)
        return [s[:, u * LANES:(u + 1) * LANES] for u in range(k_slabs)]

    def project_out(inv_l):
        for hh in range(hps):
            rs = slice(hh * qb, (hh + 1) * qb)
            o = jnp.concatenate([acc_ref[rs, u * LANES:(u + 1) * LANES] * inv_l[rs] for u in range(c_slabs)],
                                axis=1)
            y = jnp.dot(o.astype(BF16), wuv_ref[hh], preferred_element_type=F32)
            o_ref[:, hh * v:(hh + 1) * v] = y.astype(o_ref.dtype)

    @pl.when(use_shift)
    def _static_shift():
        def kv_step(kt, carry):
            off = pl.multiple_of(kt * KEY_TILE, KEY_TILE)
            lat_t = lat_ref[0, pl.ds(off, KEY_TILE), :]
            for ci in range(hps // hpc):
                rs = slice(ci * rc, (ci + 1) * rc)
                ps = [jnp.exp(sl) for sl in masked_logits(ci, off)]
                psum = ps[0]
                for pu in ps[1:]:
                    psum = psum + pu
                l_ref[rs, :] = l_ref[rs, :] + psum
                pv = jnp.dot(jnp.concatenate(ps, axis=1).astype(BF16), lat_t, preferred_element_type=F32)
                for u in range(c_slabs):
                    cs = slice(u * LANES, (u + 1) * LANES)
                    acc_ref[rs, cs] = acc_ref[rs, cs] + pv[:, cs]
            return carry

        lax.fori_loop(0, n_t, kv_step, 0)
        project_out(jnp.broadcast_to(1.0 / jnp.sum(l_ref[...], axis=1, keepdims=True), l_ref.shape))

    @pl.when(jnp.logical_not(use_shift))
    def _online():
        m_ref[...] = jnp.full(m_ref.shape, -3e38, F32)

        def kv_step(kt, carry):
            off = pl.multiple_of(kt * KEY_TILE, KEY_TILE)
            lat_t = lat_ref[0, pl.ds(off, KEY_TILE), :]
            for ci in range(hps // hpc):
                rs = slice(ci * rc, (ci + 1) * rc)
                slabs = masked_logits(ci, off)
                mx = slabs[0]
                for sl in slabs[1:]:
                    mx = jnp.maximum(mx, sl)
                m_prev = m_ref[rs, :]
                m_new = jnp.maximum(m_prev, jnp.max(mx, axis=1, keepdims=True))
                alpha = jnp.exp(m_prev - m_new)
                ps = [jnp.exp(sl - m_new) for sl in slabs]
                psum = ps[0]
                for pu in ps[1:]:
                    psum = psum + pu
                l_ref[rs, :] = alpha * l_ref[rs, :] + jnp.sum(psum, axis=1, keepdims=True)
                pv = jnp.dot(jnp.concatenate(ps, axis=1).astype(BF16), lat_t, preferred_element_type=F32)
                for u in range(c_slabs):
                    cs = slice(u * LANES, (u + 1) * LANES)
                    acc_ref[rs, cs] = alpha * acc_ref[rs, cs] + pv[:, cs]
                m_ref[rs, :] = m_new
            return carry

        lax.fori_loop(0, n_t, kv_step, 0)
        project_out(1.0 / l_ref[...])


def _sparse_attention(logit_bound, q, qi, widx, ki, kk, lat, w_uv, *, batch, tp, k_sel, pad_front):
    nblk, n_heads, qb, c = q.shape
    usable = 2.0 * logit_bound <= MAX_EXP_SPAN
    shift_info = jnp.stack([jnp.where(usable, logit_bound, 0.0), usable.astype(F32)]).astype(F32)
    tk = ki.shape[1]
    v = w_uv.shape[2]
    nq = tp // qb
    hps = min(HEADS_PER_STEP, n_heads)
    rows = hps * qb
    bias = _select_bias(shift_info, qi, widx, ki, batch=batch, tp=tp, k_sel=k_sel, pad_front=pad_front)
    return pl.pallas_call(
        _attn_kernel,
        grid=(batch, nq, n_heads // hps),
        in_specs=[pl.BlockSpec(memory_space=pltpu.SMEM),
                  pl.BlockSpec((1, hps, qb, c), lambda b, j, h: (b * nq + j, h, 0, 0)),
                  pl.BlockSpec((qb, tk), lambda b, j, h: (b * nq + j, 0)),
                  pl.BlockSpec((1, tk, c), lambda b, j, h: (b, 0, 0), pipeline_mode=pl.Buffered(1)),
                  pl.BlockSpec((1, tk, c), lambda b, j, h: (b, 0, 0), pipeline_mode=pl.Buffered(1)),
                  pl.BlockSpec((hps, c, v), lambda b, j, h: (h, 0, 0))],
        out_specs=pl.BlockSpec((qb, hps * v), lambda b, j, h: (b * nq + j, h)),
        out_shape=jax.ShapeDtypeStruct((nblk * qb, n_heads * v), BF16),
        scratch_shapes=[pltpu.VMEM((rows, c), F32), pltpu.VMEM((rows, LANES), F32),
                        pltpu.VMEM((rows, LANES), F32)],
        name="sparse_attention",
        compiler_params=_params("parallel", "arbitrary", "arbitrary"),
    )(shift_info, q, bias, kk, lat, w_uv)


def _merge_kernel(yp_ref, ya_ref, wp_ref, wa_ref, gp_ref, ga_ref, o_ref):
    pool = jnp.dot(yp_ref[...], wp_ref[...], preferred_element_type=F32)
    attn = jnp.dot(ya_ref[...], wa_ref[...], preferred_element_type=F32)
    o_ref[...] = (gp_ref[...].astype(F32) * pool + ga_ref[...].astype(F32) * attn).astype(o_ref.dtype)


def _merge(y_pool, y_attn, w_bp, w_ba, gates):
    n, pw = y_pool.shape
    aw = y_attn.shape[1]
    d = w_bp.shape[1]
    tm = _tile(n, 768, 128)
    tn = 512
    return pl.pallas_call(
        _merge_kernel,
        grid=(n // tm, d // tn),
        in_specs=[pl.BlockSpec((tm, pw), lambda i, j: (i, 0)),
                  pl.BlockSpec((tm, aw), lambda i, j: (i, 0)),
                  pl.BlockSpec((pw, tn), lambda i, j: (0, j)),
                  pl.BlockSpec((aw, tn), lambda i, j: (0, j)),
                  pl.BlockSpec((tm, tn), lambda i, j: (i, j)),
                  pl.BlockSpec((tm, tn), lambda i, j: (i, j + d // tn))],
        out_specs=pl.BlockSpec((tm, tn), lambda i, j: (i, j)),
        out_shape=jax.ShapeDtypeStruct((n, d), BF16),
        name="branch_merge",
        compiler_params=_params("parallel", "arbitrary"),
    )(y_pool, y_attn, w_bp, w_ba, gates, gates)


def _out_proj_kernel(x_ref, w_ref, r_ref, o_ref):
    o_ref[...] = r_ref[...] + jnp.dot(x_ref[...], w_ref[...], preferred_element_type=F32)


def _out_proj(merged, w_out, resid):
    n, k = merged.shape
    d = w_out.shape[1]
    tm = _tile(n, 768, 128)
    tn = _tile(d, 1024, LANES)
    return pl.pallas_call(
        _out_proj_kernel,
        grid=(n // tm, d // tn),
        in_specs=[pl.BlockSpec((tm, k), lambda i, j: (i, 0)),
                  pl.BlockSpec((k, tn), lambda i, j: (0, j)),
                  pl.BlockSpec((tm, tn), lambda i, j: (i, j))],
        out_specs=pl.BlockSpec((tm, tn), lambda i, j: (i, j)),
        out_shape=jax.ShapeDtypeStruct((n, d), F32),
        name="out_proj_residual",
        compiler_params=_params("parallel", "arbitrary"),
    )(merged, w_out, resid)


def _pack_halves(x):
    w = x.shape[1] // 2
    lo = pltpu.bitcast(x[:, :w].astype(BF16).astype(F32), I32)
    hi = pltpu.bitcast(x[:, w:].astype(BF16).astype(F32), I32)
    return ((lo >> 16) & 0xFFFF) | (hi & -65536)


def _unpack_halves(p):
    return pltpu.bitcast(p << 16, F32), pltpu.bitcast(p & -65536, F32)


def _router_kernel(h_ref, g_ref, rw_ref, rb_ref, xp_o, e_o, w_o, rank_o, cnt_o, carry_ref):
    i = pl.program_id(0)

    @pl.when(i == 0)
    def _():
        carry_ref[...] = jnp.zeros_like(carry_ref)

    x = h_ref[...]
    xn = x * lax.rsqrt(jnp.mean(x * x, axis=-1, keepdims=True) + EPS) * g_ref[...]
    xp_o[...] = _pack_halves(xn)
    logits = jnp.dot(xn.astype(BF16), rw_ref[...], preferred_element_type=F32) + rb_ref[...]
    tm = logits.shape[0]
    lane = lax.broadcasted_iota(I32, (tm, LANES), 1).astype(F32)
    vals, hots = [], []
    cur = logits
    for k in range(TOP_K_EXPERTS):
        m = jnp.max(cur, axis=1, keepdims=True)
        idx = jnp.min(jnp.where(cur == m, lane, float(LANES)), axis=1, keepdims=True)
        hot = lane == idx
        vals.append(m)
        hots.append(hot)
        e_o[:, k:k + 1] = idx.astype(I32)
        cur = jnp.where(hot, -jnp.inf, cur)
    exps = [jnp.exp(vk - vals[0]) for vk in vals]
    denom = exps[0]
    for ek in exps[1:]:
        denom = denom + ek
    for k in range(TOP_K_EXPERTS):
        w_o[:, k:k + 1] = exps[k] / denom
    onehot = jnp.zeros((tm, LANES), F32)
    for hot in hots:
        onehot = onehot + jnp.where(hot, 1.0, 0.0)
    lower = (lax.broadcasted_iota(I32, (tm, tm), 0) > lax.broadcasted_iota(I32, (tm, tm), 1)).astype(BF16)
    before = jnp.dot(lower, onehot.astype(BF16), preferred_element_type=F32) + carry_ref[...]
    for k in range(TOP_K_EXPERTS):
        rank_o[:, k:k + 1] = jnp.sum(jnp.where(hots[k], before, 0.0), axis=1, keepdims=True).astype(I32)
    carry_ref[...] = carry_ref[...] + jnp.sum(onehot, axis=0, keepdims=True)
    cnt_o[...] = carry_ref[...]


def _router(h2, norm2_g, rw, rb):
    n, d = h2.shape
    tm = _tile(n, ROUTE_TILE, Q_BLOCK)
    kk = TOP_K_EXPERTS
    return pl.pallas_call(
        _router_kernel,
        grid=(n // tm,),
        in_specs=[pl.BlockSpec((tm, d), lambda i: (i, 0)),
                  pl.BlockSpec((1, d), lambda i: (0, 0)),
                  pl.BlockSpec((d, LANES), lambda i: (0, 0)),
                  pl.BlockSpec((1, LANES), lambda i: (0, 0))],
        out_specs=[pl.BlockSpec((tm, d // 2), lambda i: (i, 0)),
                   pl.BlockSpec((tm, kk), lambda i: (i, 0)),
                   pl.BlockSpec((tm, kk), lambda i: (i, 0)),
                   pl.BlockSpec((tm, kk), lambda i: (i, 0)),
                   pl.BlockSpec((1, LANES), lambda i: (0, 0))],
        out_shape=[jax.ShapeDtypeStruct((n, d // 2), I32),
                   jax.ShapeDtypeStruct((n, kk), I32),
                   jax.ShapeDtypeStruct((n, kk), F32),
                   jax.ShapeDtypeStruct((n, kk), I32),
                   jax.ShapeDtypeStruct((1, LANES), F32)],
        scratch_shapes=[pltpu.VMEM((1, LANES), F32)],
        name="moe_router",
        compiler_params=_params("arbitrary"),
    )(h2, norm2_g.reshape(1, d), rw, rb)


def _row_copy(src_ref, src_row, dst_ref, dst_row, sem):
    return pltpu.make_async_copy(src_ref.at[pl.ds(src_row, 1), :], dst_ref.at[pl.ds(dst_row, 1), :], sem)


def _dispatch_kernel(dest_ref, pad_start_ref, pad_count_ref, xp_ref, xs_ref, zero_ref, sem, zero_sem, tail_sem):
    tm = xp_ref.shape[0]
    base = pl.program_id(0) * tm * TOP_K_EXPERTS

    def issue(r, carry):
        for k in range(TOP_K_EXPERTS):
            _row_copy(xp_ref, r, xs_ref, dest_ref[base + r * TOP_K_EXPERTS + k], sem).start(priority=k % 2)
        return carry

    lax.fori_loop(0, tm, issue, 0, unroll=4)
    for k in range(TOP_K_EXPERTS):
        pltpu.make_async_copy(xp_ref, xs_ref.at[pl.ds(0, tm), :], sem).wait()

    @pl.when(pl.program_id(0) == pl.num_programs(0) - 1)
    def _():
        zero_ref[...] = jnp.zeros_like(zero_ref)

        def per_expert(e, total):
            def zero_row(r, carry):
                _row_copy(zero_ref, 0, xs_ref, pad_start_ref[e] + r, zero_sem).start()
                return carry
            lax.fori_loop(0, pad_count_ref[e], zero_row, 0)
            return total + pad_count_ref[e]

        n_exp = pad_start_ref.shape[0]
        total = lax.fori_loop(0, n_exp, per_expert, 0)

        def zero_done(r, carry):
            _row_copy(zero_ref, 0, xs_ref, 0, zero_sem).wait()
            return carry

        lax.fori_loop(0, total, zero_done, 0)

        tile = zero_ref.shape[0]
        first = (pad_start_ref[n_exp - 1] + pad_count_ref[n_exp - 1]) // tile
        n_tail = xs_ref.shape[0] // tile - first

        def tile_copy(i):
            row0 = pl.multiple_of((first + i) * tile, tile)
            return pltpu.make_async_copy(zero_ref, xs_ref.at[pl.ds(row0, tile), :], tail_sem)

        def zero_tile(i, carry):
            tile_copy(i).start()
            return carry

        def tile_done(i, carry):
            tile_copy(i).wait()
            return carry

        lax.fori_loop(0, n_tail, zero_tile, 0)
        lax.fori_loop(0, n_tail, tile_done, 0)


def _dispatch(dest_flat, pad_start, pad_count, xp, n_slots):
    n, w = xp.shape
    tm = _tile(n, ROUTE_TILE, Q_BLOCK)
    return pl.pallas_call(
        _dispatch_kernel,
        grid_spec=pltpu.PrefetchScalarGridSpec(
            num_scalar_prefetch=3,
            grid=(n // tm,),
            in_specs=[pl.BlockSpec((tm, w), lambda i, *_: (i, 0))],
            out_specs=pl.BlockSpec(memory_space=pl.ANY),
            scratch_shapes=[pltpu.VMEM((EXPERT_TILE, w), I32), pltpu.SemaphoreType.DMA(()),
                            pltpu.SemaphoreType.DMA(()), pltpu.SemaphoreType.DMA(())]),
        out_shape=jax.ShapeDtypeStruct((n_slots, w), I32),
        name="moe_dispatch",
        compiler_params=_params("arbitrary"),
    )(dest_flat, pad_start, pad_count, xp)


def _deinterleave_kernel(w_ref, o_ref):
    blk = 2 * LANES
    dst = lax.broadcasted_iota(I32, (blk, blk), 1)
    src = jnp.where(dst < LANES, 2 * dst, 2 * (dst - LANES) + 1)
    perm = (lax.broadcasted_iota(I32, (blk, blk), 0) == src).astype(BF16)
    for b in range(w_ref.shape[2] // blk):
        cols = slice(b * blk, (b + 1) * blk)
        o_ref[0, :, cols] = jnp.dot(w_ref[0, :, cols].astype(BF16), perm,
                                    preferred_element_type=F32).astype(o_ref.dtype)


def _deinterleave_gate_up(w_gate_up):
    n_exp, d, de2 = w_gate_up.shape
    assert de2 % (2 * LANES) == 0
    tk = _tile(d, 512, 16)
    return pl.pallas_call(
        _deinterleave_kernel,
        grid=(n_exp, d // tk),
        in_specs=[pl.BlockSpec((1, tk, de2), lambda e, i: (e, i, 0))],
        out_specs=pl.BlockSpec((1, tk, de2), lambda e, i: (e, i, 0)),
        out_shape=jax.ShapeDtypeStruct((n_exp, d, de2), BF16),
        name="moe_weight_regroup",
        compiler_params=_params("parallel", "arbitrary"),
    )(w_gate_up)


def _expert_up_kernel(te_ref, tv_ref, xs_ref, w_ref, b_ref, o_ref):
    t = pl.program_id(0)

    @pl.when(tv_ref[t] > 0)
    def _():
        lo, hi = _unpack_halves(xs_ref[...])
        x = jnp.concatenate([lo.astype(BF16), hi.astype(BF16)], axis=1)
        gu = jnp.dot(x, w_ref[0], preferred_element_type=F32) + b_ref[0]
        for blk in range(gu.shape[1] // (2 * LANES)):
            g0 = blk * 2 * LANES
            gate = jnp.minimum(gu[:, g0:g0 + LANES], SWIGLU_LIMIT)
            up = jnp.clip(gu[:, g0 + LANES:g0 + 2 * LANES], -SWIGLU_LIMIT, SWIGLU_LIMIT)
            act = (up + 1.0) * gate * jax.nn.sigmoid(SWIGLU_ALPHA * gate)
            o_ref[:, blk * LANES:(blk + 1) * LANES] = act.astype(o_ref.dtype)

    @pl.when(tv_ref[t] == 0)
    def _():
        o_ref[...] = jnp.zeros_like(o_ref)


def _expert_up(tile_expert, tile_valid, xs, w_gu, b_gu):
    n_slots, w = xs.shape
    n_exp, d, de2 = w_gu.shape
    tm = EXPERT_TILE
    return pl.pallas_call(
        _expert_up_kernel,
        grid_spec=pltpu.PrefetchScalarGridSpec(
            num_scalar_prefetch=2,
            grid=(n_slots // tm,),
            in_specs=[pl.BlockSpec((tm, w), lambda t, te, tv: (t, 0)),
                      pl.BlockSpec((1, d, de2), lambda t, te, tv: (te[t], 0, 0)),
                      pl.BlockSpec((1, 1, de2), lambda t, te, tv: (te[t], 0, 0))],
            out_specs=pl.BlockSpec((tm, de2 // 2), lambda t, te, tv: (t, 0))),
        out_shape=jax.ShapeDtypeStruct((n_slots, de2 // 2), BF16),
        name="moe_expert_up",
        compiler_params=_params("arbitrary"),
    )(tile_expert, tile_valid, xs, w_gu, b_gu)


def _expert_down_kernel(te_ref, tv_ref, a_ref, w_ref, b_ref, o_ref, wbf_ref):
    t = pl.program_id(0)

    @pl.when(jnp.logical_or(t == 0, te_ref[t] != te_ref[jnp.maximum(t - 1, 0)]))
    def _():
        wbf_ref[...] = w_ref[0].astype(BF16)

    @pl.when(tv_ref[t] > 0)
    def _():
        y = jnp.dot(a_ref[...], wbf_ref[...], preferred_element_type=F32) + b_ref[0]
        o_ref[...] = _pack_halves(y)

    @pl.when(tv_ref[t] == 0)
    def _():
        o_ref[...] = jnp.zeros_like(o_ref)


def _expert_down(tile_expert, tile_valid, act, w_down, b_down):
    n_slots, de = act.shape
    n_exp, _, d = w_down.shape
    tm = EXPERT_TILE
    return pl.pallas_call(
        _expert_down_kernel,
        grid_spec=pltpu.PrefetchScalarGridSpec(
            num_scalar_prefetch=2,
            grid=(n_slots // tm,),
            in_specs=[pl.BlockSpec((tm, de), lambda t, te, tv: (t, 0)),
                      pl.BlockSpec((1, de, d), lambda t, te, tv: (te[t], 0, 0)),
                      pl.BlockSpec((1, 1, d), lambda t, te, tv: (te[t], 0, 0))],
            out_specs=pl.BlockSpec((tm, d // 2), lambda t, te, tv: (t, 0)),
            scratch_shapes=[pltpu.VMEM((de, d), BF16)]),
        out_shape=jax.ShapeDtypeStruct((n_slots, d // 2), I32),
        name="moe_expert_down",
        compiler_params=_params("arbitrary"),
    )(tile_expert, tile_valid, act, w_down, b_down)


def _combine_kernel(dest_ref, w_ref, h_ref, ye_ref, o_ref, buf_ref, sems, *, nq):
    b = pl.program_id(0)
    i = pl.program_id(1)
    n_i = pl.num_programs(1)
    n_steps = pl.num_programs(0) * n_i
    tm = h_ref.shape[0]
    half = buf_ref.shape[3]
    step = b * n_i + i
    slot = step % 2

    def start_rows(bb, ii, sl):
        base = ((bb * nq + 1 + ii) * tm) * TOP_K_EXPERTS

        def issue(r, carry):
            for k in range(TOP_K_EXPERTS):
                _row_copy(ye_ref, dest_ref[base + r * TOP_K_EXPERTS + k], buf_ref.at[sl, k], r,
                          sems.at[sl]).start(priority=k % 2)
            return carry

        lax.fori_loop(0, tm, issue, 0, unroll=4)

    @pl.when(step == 0)
    def _():
        start_rows(b, i, slot)

    @pl.when(step + 1 < n_steps)
    def _():
        wrap = i + 1 == n_i
        start_rows(jnp.where(wrap, b + 1, b), jnp.where(wrap, 0, i + 1), 1 - slot)

    for k in range(TOP_K_EXPERTS):
        pltpu.make_async_copy(ye_ref.at[pl.ds(0, tm), :], buf_ref.at[slot, k], sems.at[slot]).wait()

    lo_sum = h_ref[:, :half]
    hi_sum = h_ref[:, half:]
    for k in range(TOP_K_EXPERTS):
        lo, hi = _unpack_halves(buf_ref[slot, k])
        wk = w_ref[:, k:k + 1]
        lo_sum = lo_sum + wk * lo
        hi_sum = hi_sum + wk * hi
    o_ref[0, :, :half] = lo_sum
    o_ref[0, :, half:] = hi_sum


def _combine(dest_flat, sel_w, h2, ye, *, batch, seq, tp):
    d = h2.shape[1]
    tm = Q_BLOCK
    nq = tp // tm
    return pl.pallas_call(
        functools.partial(_combine_kernel, nq=nq),
        grid_spec=pltpu.PrefetchScalarGridSpec(
            num_scalar_prefetch=1,
            grid=(batch, seq // tm),
            in_specs=[pl.BlockSpec((tm, TOP_K_EXPERTS), lambda b, i, dest: (b * nq + 1 + i, 0)),
                      pl.BlockSpec((tm, d), lambda b, i, dest: (b * nq + 1 + i, 0)),
                      pl.BlockSpec(memory_space=pl.ANY)],
            out_specs=pl.BlockSpec((1, tm, d), lambda b, i, dest: (b, i, 0)),
            scratch_shapes=[pltpu.VMEM((2, TOP_K_EXPERTS, tm, d // 2), I32), pltpu.SemaphoreType.DMA((2,))]),
        out_shape=jax.ShapeDtypeStruct((batch, seq, d), F32),
        name="moe_combine",
        compiler_params=_params("arbitrary", "arbitrary"),
    )(dest_flat, sel_w, h2, ye)


def _moe(h2, norm2_g, router_w, router_b, w_gate_up, b_gate_up, w_down, b_down, *, batch, seq, tp):
    n, d = h2.shape
    n_exp = router_w.shape[1]
    de = w_down.shape[1]
    assert n_exp <= LANES
    rw = jnp.pad(router_w, ((0, 0), (0, LANES - n_exp))).astype(BF16)
    rb = jnp.pad(router_b, (0, LANES - n_exp), constant_values=NEG).reshape(1, LANES)
    xp, sel_e, sel_w, rank, counts = _router(h2, norm2_g, rw, rb)

    cnt = counts[0, :n_exp].astype(I32)
    padded = ((cnt + EXPERT_TILE - 1) // EXPERT_TILE) * EXPERT_TILE
    e_ids = jnp.arange(n_exp, dtype=I32)
    ends = jnp.sum(jnp.where(e_ids[:, None] <= e_ids[None, :], padded[:, None], 0), axis=0)
    starts = ends - padded
    total = ends[n_exp - 1]
    dest_flat = (starts[sel_e] + rank).reshape(-1)
    n_tiles = (n * TOP_K_EXPERTS) // EXPERT_TILE + n_exp
    tile_start = jnp.arange(n_tiles, dtype=I32) * EXPERT_TILE
    tile_valid = (tile_start < total).astype(I32)
    last_expert = jnp.sum((ends <= total - 1).astype(I32))
    tile_expert = jnp.minimum(jnp.sum((ends[None, :] <= tile_start[:, None]).astype(I32), axis=1), last_expert)

    w_gu = _deinterleave_gate_up(w_gate_up)
    nb = (2 * de) // (2 * LANES)
    b_gu = b_gate_up.reshape(n_exp, nb, LANES, 2).transpose(0, 1, 3, 2).reshape(n_exp, 1, 2 * de)

    xs = _dispatch(dest_flat, starts + cnt, padded - cnt, xp, n_tiles * EXPERT_TILE)
    act = _expert_up(tile_expert, tile_valid, xs, w_gu, b_gu)
    ye = _expert_down(tile_expert, tile_valid, act, w_down, b_down.reshape(n_exp, 1, d))
    return _combine(dest_flat, sel_w, h2, ye, batch=batch, seq=seq, tp=tp)


def kernel(x, meta_tokens, norm1_g, w_in, q_norm_g, w_uq, w_iq, kv_norm_g, q_head_norm_g, k_head_norm_g,
           idx_k_norm_g, w_uv, w_branch_attn, w_pool, pool_scale, w_branch_pool, w_out, norm2_g, router_w,
           router_b, w_gate_up, b_gate_up, w_down, b_down):
    batch, seq, d = x.shape
    n_meta = meta_tokens.shape[0]
    depth = norm1_g.shape[0]
    qr, kvr, di = q_norm_g.shape[1], kv_norm_g.shape[1], idx_k_norm_g.shape[1]
    n_heads, v = w_uv.shape[1], w_uv.shape[3]
    idx_heads = w_iq.shape[2] // di
    pw = pool_scale.shape[1]
    assert depth == 1
    assert n_meta <= CHUNK and seq % Q_BLOCK == 0 and idx_heads <= LANES
    assert qr % kvr == 0 and (qr + kvr) % di == 0 and di % LANES == 0
    pad_front = Q_BLOCK - n_meta
    tp = Q_BLOCK + seq
    tk = -(-tp // KEY_TILE) * KEY_TILE
    k_sel = min(TOPK_MAX, seq // 4)
    attn_scale = kvr ** -0.5
    idx_scale = (di ** -0.5) * (idx_heads ** -0.5)

    head_rows = jnp.concatenate([jnp.zeros((pad_front, d), x.dtype), meta_tokens.astype(x.dtype)], axis=0)

    out = None
    for l in range(depth):
        n_small = qr + kvr + di
        o_pool = n_small + idx_heads
        o_gate = o_pool + pw
        w_small = jnp.pad(w_in[l][:, :o_pool], ((0, 0), (0, LANES - idx_heads))).astype(BF16)
        w_a = w_in[l][:, o_pool:o_gate].astype(BF16)
        w_g = w_in[l][:, o_gate:].astype(BF16)

        h, xn = _embed_norm(x, head_rows, norm1_g[l])
        small = _matmul(xn, w_small, F32, tn=_tile(w_small.shape[1], 1024, LANES), name="in_proj_latents")
        a_pool = _matmul(xn, w_a, BF16, tn=_tile(pw, 1024, LANES), name="in_proj_pool")
        gates = _matmul(xn, w_g, BF16, tn=_tile(2 * d, 1024, LANES), sigmoid=True, name="in_proj_gates")

        y_pool = _pool_mixer(a_pool, w_pool[l].astype(BF16), pool_scale[l], batch=batch, tp=tp,
                             pad_front=pad_front)

        cq, widx = _prep_q(small, q_norm_g[l], qr, n_small, idx_scale)
        lat, kk, ki = _prep_kv(small, kv_norm_g[l], k_head_norm_g[l], idx_k_norm_g[l],
                               batch=batch, tp=tp, tk=tk, qr=qr, kvr=kvr, di=di)
        q = _qproj(cq, w_uq[l].astype(BF16), q_head_norm_g[l], n_heads=n_heads, hd=kvr,
                   heads_per_tile=min(2, n_heads),
                   scale=attn_scale, norm=True, name="q_proj_headnorm")
        qi = _qproj(cq, w_iq[l].astype(BF16), jnp.ones((di,), F32), n_heads=idx_heads, hd=di,
                    heads_per_tile=min(8, idx_heads), scale=1.0, norm=False, name="indexer_q_proj")
        logit_bound = (BOUND_MARGIN * kvr * attn_scale * jnp.max(jnp.abs(q_head_norm_g[l]))
                       * jnp.max(jnp.abs(k_head_norm_g[l])))
        y_attn = _sparse_attention(logit_bound, q, qi, widx, ki, kk, lat, w_uv[l].astype(BF16), batch=batch,
                                   tp=tp, k_sel=k_sel, pad_front=pad_front)

        merged = _merge(y_pool, y_attn, w_branch_pool[l].astype(BF16), w_branch_attn[l].astype(BF16), gates)
        h2 = _out_proj(merged, w_out[l].astype(BF16), h)

        out = _moe(h2, norm2_g[l], router_w[l], router_b[l], w_gate_up[l], b_gate_up[l], w_down[l],
                   b_down[l], batch=batch, seq=seq, tp=tp)
    return out
```
